```python
import jax, jax.numpy as jnp
from jax import lax
import numpy as np

D_MODEL = 1024
BATCH = 8
SEQ = 4096
DEPTH = 2

MLA_HEADS = 8
QK_NOPE_DIM = 64
QK_ROPE_DIM = 32
V_HEAD_DIM = 64
QK_HEAD_DIM = QK_NOPE_DIM + QK_ROPE_DIM
Q_LORA_RANK = D_MODEL // 4
KV_LORA_RANK = D_MODEL // 8
ROPE_BASE = 10000.0
Q_BLOCK = 128
LRU_WIDTH = D_MODEL // 2
LRU_HEADS = 8
LRU_HEAD_DIM = LRU_WIDTH // LRU_HEADS
CONV_WIDTH = 4
LRU_C = 8.0
IN_SPLITS = (Q_LORA_RANK, KV_LORA_RANK, QK_ROPE_DIM, LRU_WIDTH, LRU_WIDTH, D_MODEL, D_MODEL)
D_IN = sum(IN_SPLITS)
D_FF = 11 * D_MODEL // 4
N_EXPERTS = 8
TOP_K = 2
D_FF_EXPERT = 7 * D_MODEL // 2
MOE_BLOCK = 256
N_DENSE = (DEPTH + 1) // 2
N_MOE = DEPTH // 2
RMS_EPS = 1e-6

kernel_name = 'hybrid_mla_rglru_moe_block'


def rms_norm(x, g):
    xf = x.astype(jnp.float32)
    y = xf * lax.rsqrt(jnp.mean(xf * xf, axis=-1, keepdims=True) + RMS_EPS)
    return (y * g.astype(jnp.float32)).astype(x.dtype)


def rope_tables(positions):
    inv_freq = ROPE_BASE ** (-jnp.arange(0, QK_ROPE_DIM, 2, dtype=jnp.float32) / QK_ROPE_DIM)
    ang = positions.astype(jnp.float32)[..., None] * inv_freq
    return jnp.cos(ang)[:, :, None, :], jnp.sin(ang)[:, :, None, :]


def apply_rope(x, cos, sin):
    x1, x2 = jnp.split(x.astype(jnp.float32), 2, axis=-1)
    return jnp.concatenate([x1 * cos - x2 * sin, x2 * cos + x1 * sin], axis=-1).astype(x.dtype)


def causal_block_attention(q, k, v):
    b, s, h, dq = q.shape
    n_blk = s // Q_BLOCK
    scale = QK_HEAD_DIM ** -0.5
    q_blocks = q.reshape(b, n_blk, Q_BLOCK, h, dq).transpose(1, 0, 2, 3, 4)
    key_idx = jnp.arange(s)

    def one_block(args):
        q_blk, i = args
        sc = jnp.einsum('bqhd,bkhd->bhqk', q_blk, k, preferred_element_type=jnp.float32) * scale
        q_idx = i * Q_BLOCK + jnp.arange(Q_BLOCK)
        sc = jnp.where(key_idx[None, :] <= q_idx[:, None], sc, -jnp.inf)
        p = jax.nn.softmax(sc, axis=-1).astype(v.dtype)
        return jnp.einsum('bhqk,bkhd->bqhd', p, v)

    o = lax.map(one_block, (q_blocks, jnp.arange(n_blk)))
    return o.transpose(1, 0, 2, 3, 4).reshape(b, s, h, v.shape[-1])


def linear_recurrence(a, b):
    def combine(left, right):
        a_l, b_l = left
        a_r, b_r = right
        return a_l * a_r, a_r * b_l + b_r
    _, h = lax.associative_scan(combine, (a, b), axis=1)
    return h


def hybrid_mixer(h, cos, sin, w_in, q_norm_g, w_uq, kv_norm_g, w_ukv, qk_q_g, qk_k_g, w_up_attn,
                 conv_w, conv_b, w_rg, b_rg, w_ig, b_ig, lru_lambda, w_up_lru, w_o):
    b, s, _ = h.shape
    proj = h @ w_in
    c_q, c_kv, k_r, x_lru, g_lru, gate_a, gate_b = jnp.split(
        proj, list(np.cumsum(IN_SPLITS)[:-1]), axis=-1)

    q = (rms_norm(c_q, q_norm_g) @ w_uq).reshape(b, s, MLA_HEADS, QK_HEAD_DIM)
    kv = (rms_norm(c_kv, kv_norm_g) @ w_ukv).reshape(b, s, MLA_HEADS, QK_NOPE_DIM + V_HEAD_DIM)
    k_nope, v = jnp.split(kv, [QK_NOPE_DIM], axis=-1)
    k_rope = jnp.broadcast_to(k_r[:, :, None, :], (b, s, MLA_HEADS, QK_ROPE_DIM))
    k = jnp.concatenate([k_nope, k_rope], axis=-1)
    q = rms_norm(q, qk_q_g)
    k = rms_norm(k, qk_k_g)
    q = jnp.concatenate([q[..., :QK_NOPE_DIM], apply_rope(q[..., QK_NOPE_DIM:], cos, sin)], axis=-1)
    k = jnp.concatenate([k[..., :QK_NOPE_DIM], apply_rope(k[..., QK_NOPE_DIM:], cos, sin)], axis=-1)
    o_attn = causal_block_attention(q, k, v).reshape(b, s, MLA_HEADS * V_HEAD_DIM)
    u_a = o_attn @ w_up_attn

    xc = lax.conv_general_dilated(
        x_lru, conv_w[:, None, :], window_strides=(1,), padding=[(CONV_WIDTH - 1, 0)],
        dimension_numbers=('NWC', 'WIO', 'NWC'), feature_group_count=LRU_WIDTH) + conv_b
    xh = xc.reshape(b, s, LRU_HEADS, LRU_HEAD_DIM)
    r = jax.nn.sigmoid((jnp.einsum('bshi,hij->bshj', xh, w_rg).reshape(b, s, LRU_WIDTH) + b_rg).astype(jnp.float32))
    i = jax.nn.sigmoid((jnp.einsum('bshi,hij->bshj', xh, w_ig).reshape(b, s, LRU_WIDTH) + b_ig).astype(jnp.float32))
    log_a = -LRU_C * r * jax.nn.softplus(-lru_lambda.astype(jnp.float32))
    a = jnp.exp(log_a)
    mult = jnp.sqrt(-jnp.expm1(2.0 * log_a))
    hs = linear_recurrence(a, mult * i * xc.astype(jnp.float32))
    y_lru = (hs * jax.nn.gelu(g_lru.astype(jnp.float32), approximate=True)).astype(h.dtype)
    u_b = y_lru @ w_up_lru

    merged = jax.nn.sigmoid(gate_a) * u_a + jax.nn.sigmoid(gate_b) * u_b
    return merged @ w_o


def swiglu(h, wg, wu, wd):
    return (jax.nn.silu(h @ wg) * (h @ wu)) @ wd


def moe_swiglu(h2, router, wg, wu, wd):
    n_tok, d = h2.shape
    n_assign = n_tok * TOP_K
    logits = jnp.matmul(h2, router, preferred_element_type=jnp.float32)
    top_logit, top_e = lax.top_k(logits, TOP_K)
    top_w = jax.nn.softmax(top_logit, axis=-1)
    flat_e = top_e.reshape(-1).astype(jnp.int32)
    flat_tok = jnp.repeat(jnp.arange(n_tok, dtype=jnp.int32), TOP_K)
    flat_w = top_w.reshape(-1)
    se, stok, sw = lax.sort((flat_e, flat_tok, flat_w), num_keys=1, is_stable=True)
    counts = jnp.bincount(flat_e, length=N_EXPERTS)
    padded = (counts + MOE_BLOCK - 1) // MOE_BLOCK * MOE_BLOCK
    start = jnp.cumsum(counts) - counts
    pstart = jnp.cumsum(padded) - padded
    pend = pstart + padded
    dest = pstart[se] + jnp.arange(n_assign, dtype=jnp.int32) - start[se]
    n_blocks = -(-(n_assign + N_EXPERTS * (MOE_BLOCK - 1)) // MOE_BLOCK)
    n_rows = n_blocks * MOE_BLOCK
    tok_buf = jnp.full((n_rows,), n_tok, jnp.int32).at[dest].set(stok)
    w_buf = jnp.zeros((n_rows,), jnp.float32).at[dest].set(sw)
    block_rows = jnp.arange(n_blocks, dtype=jnp.int32) * MOE_BLOCK
    block_e = jnp.minimum(jnp.sum(block_rows[:, None] >= pend[None, :], axis=1), N_EXPERTS - 1)
    xg = jnp.take(h2, tok_buf, axis=0, mode='fill', fill_value=0)

    def expert_block(args):
        xb, e = args
        return swiglu(xb, wg[e], wu[e], wd[e])

    yb = lax.map(expert_block, (xg.reshape(n_blocks, MOE_BLOCK, d), block_e)).reshape(n_rows, d)
    return jnp.zeros_like(h2).at[tok_buf].add(yb * w_buf[:, None].astype(h2.dtype), mode='drop')


def setup_inputs(seed: int = 0) -> dict:
    key = jax.random.key(seed)
    ks = jax.random.split(key, 32)
    f32 = jnp.float32
    nrm = lambda k, shape, fan_in: jax.random.normal(k, shape, f32) * (fan_in ** -0.5)
    gain = lambda k, shape: 1.0 + 0.01 * jax.random.normal(k, shape, f32)
    small = lambda k, shape: 0.01 * jax.random.normal(k, shape, f32)
    res_scale = (2 * DEPTH) ** -0.5
    x = jax.random.normal(ks[0], (BATCH, SEQ, D_MODEL), f32)
    offsets = jax.random.randint(ks[1], (BATCH, 1), 0, 1024, dtype=jnp.int32)
    positions = offsets + jnp.arange(SEQ, dtype=jnp.int32)[None, :]
    u = jax.random.uniform(ks[2], (DEPTH, LRU_WIDTH), f32, minval=0.9, maxval=0.999)
    a0 = u ** (1.0 / LRU_C)
    lru_lambda = jnp.log(a0) - jnp.log1p(-a0)
    return {
        'x': x,
        'positions': positions,
        'norm1_g': gain(ks[3], (DEPTH, D_MODEL)),
        'w_in': nrm(ks[4], (DEPTH, D_MODEL, D_IN), D_MODEL),
        'q_norm_g': gain(ks[5], (DEPTH, Q_LORA_RANK)),
        'w_uq': nrm(ks[6], (DEPTH, Q_LORA_RANK, MLA_HEADS * QK_HEAD_DIM), Q_LORA_RANK),
        'kv_norm_g': gain(ks[7], (DEPTH, KV_LORA_RANK)),
        'w_ukv': nrm(ks[8], (DEPTH, KV_LORA_RANK, MLA_HEADS * (QK_NOPE_DIM + V_HEAD_DIM)), KV_LORA_RANK),
        'qk_q_g': gain(ks[9], (DEPTH, QK_HEAD_DIM)),
        'qk_k_g': gain(ks[10], (DEPTH, QK_HEAD_DIM)),
        'w_up_attn': nrm(ks[11], (DEPTH, MLA_HEADS * V_HEAD_DIM, D_MODEL), MLA_HEADS * V_HEAD_DIM),
        'conv_w': nrm(ks[12], (DEPTH, CONV_WIDTH, LRU_WIDTH), CONV_WIDTH),
        'conv_b': small(ks[13], (DEPTH, LRU_WIDTH)),
        'w_rg': nrm(ks[14], (DEPTH, LRU_HEADS, LRU_HEAD_DIM, LRU_HEAD_DIM), LRU_HEAD_DIM),
        'b_rg': small(ks[15], (DEPTH, LRU_WIDTH)),
        'w_ig': nrm(ks[16], (DEPTH, LRU_HEADS, LRU_HEAD_DIM, LRU_HEAD_DIM), LRU_HEAD_DIM),
        'b_ig': small(ks[17], (DEPTH, LRU_WIDTH)),
        'lru_lambda': lru_lambda,
        'w_up_lru': nrm(ks[18], (DEPTH, LRU_WIDTH, D_MODEL), LRU_WIDTH),
        'w_o': nrm(ks[19], (DEPTH, D_MODEL, D_MODEL), D_MODEL) * res_scale,
        'norm2_g': gain(ks[20], (DEPTH, D_MODEL)),
        'ffn_w_gate': nrm(ks[21], (N_DENSE, D_MODEL, D_FF), D_MODEL),
        'ffn_w_up': nrm(ks[22], (N_DENSE, D_MODEL, D_FF), D_MODEL),
        'ffn_w_down': nrm(ks[23], (N_DENSE, D_FF, D_MODEL), D_FF) * res_scale,
        'moe_router': nrm(ks[24], (N_MOE, D_MODEL, N_EXPERTS), D_MODEL),
        'moe_w_gate': nrm(ks[25], (N_MOE, N_EXPERTS, D_MODEL, D_FF_EXPERT), D_MODEL),
        'moe_w_up': nrm(ks[26], (N_MOE, N_EXPERTS, D_MODEL, D_FF_EXPERT), D_MODEL),
        'moe_w_down': nrm(ks[27], (N_MOE, N_EXPERTS, D_FF_EXPERT, D_MODEL), D_FF_EXPERT) * res_scale,
    }


def reference(x, positions, norm1_g, w_in, q_norm_g, w_uq, kv_norm_g, w_ukv, qk_q_g, qk_k_g,
              w_up_attn, conv_w, conv_b, w_rg, b_rg, w_ig, b_ig, lru_lambda, w_up_lru, w_o,
              norm2_g, ffn_w_gate, ffn_w_up, ffn_w_down, moe_router, moe_w_gate, moe_w_up,
              moe_w_down):
    cos, sin = rope_tables(positions)
    for layer in range(DEPTH):
        h = rms_norm(x, norm1_g[layer])
        x = x + hybrid_mixer(
            h, cos, sin, w_in[layer], q_norm_g[layer], w_uq[layer], kv_norm_g[layer], w_ukv[layer],
            qk_q_g[layer], qk_k_g[layer], w_up_attn[layer], conv_w[layer], conv_b[layer],
            w_rg[layer], b_rg[layer], w_ig[layer], b_ig[layer], lru_lambda[layer],
            w_up_lru[layer], w_o[layer])
        h = rms_norm(x, norm2_g[layer])
        j = layer // 2
        if layer % 2 == 0:
            x = x + swiglu(h, ffn_w_gate[j], ffn_w_up[j], ffn_w_down[j])
        else:
            y = moe_swiglu(h.reshape(-1, D_MODEL), moe_router[j], moe_w_gate[j], moe_w_up[j], moe_w_down[j])
            x = x + y.reshape(x.shape)
    return x
```

```python
import functools

import numpy as np
import jax
import jax.numpy as jnp
from jax import lax
from jax.experimental import pallas as pl
from jax.experimental.pallas import tpu as pltpu

BF16 = jnp.bfloat16
F32 = jnp.float32

MLA_HEADS = 8
QK_NOPE_DIM = 64
QK_ROPE_DIM = 32
V_HEAD_DIM = 64
QK_HEAD_DIM = QK_NOPE_DIM + QK_ROPE_DIM
ROPE_BASE = 10000.0
LRU_HEADS = 8
CONV_WIDTH = 4
LRU_C = 8.0
N_EXPERTS = 8
TOP_K = 2
RMS_EPS = 1e-6

LANE = 128
VMEM_LIMIT = 48 * 1024 * 1024


def _params(*sem):
    return pltpu.CompilerParams(dimension_semantics=sem, vmem_limit_bytes=VMEM_LIMIT)


def _rms(x, g):
    ms = jnp.mean(x * x, axis=-1, keepdims=True)
    return x * lax.rsqrt(ms + RMS_EPS) * g


def _dot(a, b):
    return jnp.dot(a, b, preferred_element_type=F32)


def _in_proj_body(x_ref, g_ref, w_ref, seg_ref, xl_ref, gl_ref, ga_ref, gb_ref, *, widths):
    h = _rms(x_ref[...], g_ref[...]).astype(BF16)
    o = np.cumsum((0,) + widths)
    seg_ref[...] = _dot(h, w_ref[:, o[0]:o[1]])
    xl_ref[...] = _dot(h, w_ref[:, o[1]:o[2]])
    gl_ref[...] = jax.nn.gelu(_dot(h, w_ref[:, o[2]:o[3]]), approximate=True).astype(BF16)
    ga_ref[...] = jax.nn.sigmoid(_dot(h, w_ref[:, o[3]:o[4]])).astype(BF16)
    gb_ref[...] = jax.nn.sigmoid(_dot(h, w_ref[:, o[4]:o[5]])).astype(BF16)


def _in_proj(x, g, w, widths, tm):
    n, d = x.shape
    dts = (F32, F32, BF16, BF16, BF16)
    row = lambda wd: pl.BlockSpec((tm, wd), lambda i: (i, 0))
    return pl.pallas_call(
        functools.partial(_in_proj_body, widths=widths),
        grid=(n // tm,),
        in_specs=[row(d), pl.BlockSpec((1, d), lambda i: (0, 0)),
                  pl.BlockSpec(w.shape, lambda i: (0, 0))],
        out_specs=[row(wd) for wd in widths],
        out_shape=[jax.ShapeDtypeStruct((n, wd), dt) for wd, dt in zip(widths, dts)],
        compiler_params=_params("parallel"),
        name="in_proj",
    )(x, g, w)


def _qkv_body(seg_ref, c_ref, s_ref, gq_ref, gkv_ref, wq_ref, wqr_ref, wk_ref, wkr_ref, wv_ref,
              gqm_ref, gqr_ref, gkm_ref, gkr_ref, q_ref, k_ref, v_ref, *, q_rank, kv_rank):
    cqn = _rms(seg_ref[:, 0:q_rank], gq_ref[...]).astype(BF16)
    ckvn = _rms(seg_ref[:, q_rank:q_rank + kv_rank], gkv_ref[...])
    ck = jnp.concatenate([ckvn, seg_ref[:, q_rank + kv_rank:]], axis=-1).astype(BF16)
    cos = c_ref[...]
    sin = s_ref[...]
    scale = QK_HEAD_DIM ** -0.5

    def finish(a_all, b_all, gm, gr, out_ref, mul):
        cm = gm * cos * mul
        sm = gr * sin * mul
        for h in range(MLA_HEADS):
            a = a_all[:, h * LANE:(h + 1) * LANE]
            b = b_all[:, h * LANE:(h + 1) * LANE]
            inv = lax.rsqrt(jnp.sum(a * a, axis=-1, keepdims=True) * (1.0 / QK_HEAD_DIM) + RMS_EPS)
            out_ref[:, h * LANE:(h + 1) * LANE] = (inv * (a * cm + b * sm)).astype(BF16)

    finish(_dot(cqn, wq_ref[...]), _dot(cqn, wqr_ref[...]), gqm_ref[...], gqr_ref[...], q_ref, scale)
    finish(_dot(ck, wk_ref[...]), _dot(ck, wkr_ref[...]), gkm_ref[...], gkr_ref[...], k_ref, 1.0)
    v_ref[...] = _dot(ck, wv_ref[...]).astype(BF16)


def _qkv(seg, cos, sin, gq, gkv, wq, wqr, wk, wkr, wv, gqm, gqr, gkm, gkr, tm):
    n, sw = seg.shape
    q_rank = gq.shape[1]
    kv_rank = gkv.shape[1]
    hq = wq.shape[1]
    hv = wv.shape[1]
    row = lambda wd: pl.BlockSpec((tm, wd), lambda i: (i, 0))
    full = lambda a: pl.BlockSpec(a.shape, lambda i: (0, 0))
    consts = (gq, gkv, wq, wqr, wk, wkr, wv, gqm, gqr, gkm, gkr)
    return pl.pallas_call(
        functools.partial(_qkv_body, q_rank=q_rank, kv_rank=kv_rank),
        grid=(n // tm,),
        in_specs=[row(sw), row(LANE), row(LANE)] + [full(a) for a in consts],
        out_specs=[row(hq), row(hq), row(hv)],
        out_shape=[jax.ShapeDtypeStruct((n, hq), BF16), jax.ShapeDtypeStruct((n, hq), BF16),
                   jax.ShapeDtypeStruct((n, hv), BF16)],
        compiler_params=_params("parallel"),
        name="qkv_prep",
    )(seg, cos, sin, *consts)


def _attn_body(q_ref, k_ref, v_ref, o_ref, *, tq, heads):
    i = pl.program_id(2)
    row = lax.broadcasted_iota(jnp.int32, (tq, tq), 0)
    col = lax.broadcasted_iota(jnp.int32, (tq, tq), 1)
    causal = col <= row
    lane = lax.broadcasted_iota(jnp.int32, (tq, heads * V_HEAD_DIM), 1)
    out = jnp.zeros((tq, heads * V_HEAD_DIM), F32)
    for hh in range(heads):
        q = q_ref[0, :, hh * LANE:(hh + 1) * LANE]

        def kv_step(j, carry, masked, hh=hh, q=q):
            m, l, acc = carry
            start = pl.multiple_of(j * tq, tq)
            ks = k_ref[0, pl.ds(start, tq), hh * LANE:(hh + 1) * LANE]
            vs = v_ref[0, pl.ds(start, tq), :]
            s = lax.dot_general(q, ks, (((1,), (1,)), ((), ())), preferred_element_type=F32)
            if masked:
                s = jnp.where(causal, s, -jnp.inf)
            m_new = jnp.maximum(m, jnp.max(s, axis=-1, keepdims=True))
            p = jnp.exp(s - m_new)
            alpha = jnp.exp(m - m_new)
            l_new = alpha * l + jnp.sum(p, axis=-1, keepdims=True)
            acc_new = alpha * acc + _dot(p.astype(BF16), vs)
            return m_new, l_new, acc_new

        init = (jnp.full((tq, 1), -jnp.inf, F32), jnp.zeros((tq, 1), F32),
                jnp.zeros((tq, heads * V_HEAD_DIM), F32))
        carry = lax.fori_loop(0, i, functools.partial(kv_step, masked=False), init)
        m, l, acc = kv_step(i, carry, masked=True)
        mine = (lane >= hh * V_HEAD_DIM) & (lane < (hh + 1) * V_HEAD_DIM)
        out = jnp.where(mine, acc / l, out)
    o_ref[0] = out.astype(BF16)


def _attention(q, k, v, tq):
    b, s, hq = q.shape
    heads = LANE // V_HEAD_DIM
    n_hp = hq // (heads * LANE)
    return pl.pallas_call(
        functools.partial(_attn_body, tq=tq, heads=heads),
        grid=(b, n_hp, s // tq),
        in_specs=[pl.BlockSpec((1, tq, heads * LANE), lambda bi, hp, i: (bi, i, hp)),
                  pl.BlockSpec((1, s, heads * LANE), lambda bi, hp, i: (bi, 0, hp)),
                  pl.BlockSpec((1, s, LANE), lambda bi, hp, i: (bi, 0, hp))],
        out_specs=pl.BlockSpec((1, tq, LANE), lambda bi, hp, i: (bi, i, hp)),
        out_shape=jax.ShapeDtypeStruct((b, s, v.shape[2]), BF16),
        compiler_params=_params("parallel", "parallel", "arbitrary"),
        name="attention",
    )(q, k, v)


def _lru_body(x_ref, gl_ref, cw_ref, cb_ref, wg_ref, bg_ref, lam_ref, y_ref,
              ext_s, a_s, b_s, h_s, *, nb, t):
    c = x_ref.shape[2]
    pad = 8

    @pl.when(pl.program_id(0) == 0)
    def _():
        ext_s[:, 0:pad, :] = jnp.zeros((nb, pad, c), F32)
        h_s[...] = jnp.zeros_like(h_s)

    ext_s[:, pad:pad + t, :] = x_ref[...]
    xc = cb_ref[...][None]
    for kk in range(CONV_WIDTH):
        off = pad - (CONV_WIDTH - 1) + kk
        xc = xc + ext_s[:, off:off + t, :] * cw_ref[kk:kk + 1, :][None]
    ext_s[:, 0:pad, :] = ext_s[:, t:t + pad, :]

    xc2 = xc.reshape(nb * t, c)
    g = _dot(xc2.astype(BF16), wg_ref[...]) + bg_ref[...]
    r = jax.nn.sigmoid(g[:, :c])
    ig = jax.nn.sigmoid(g[:, c:])
    log_a = -LRU_C * r * jax.nn.softplus(-lam_ref[...])
    a = jnp.exp(log_a)
    mult = jnp.sqrt(-jnp.tanh(log_a) * (a * a + 1.0))
    bv = mult * ig * xc2
    n_lt = c // LANE
    for lt in range(n_lt):
        a_s[lt] = a[:, lt * LANE:(lt + 1) * LANE]
        b_s[lt] = bv[:, lt * LANE:(lt + 1) * LANE]

    def step(tt, hs):
        idx = pl.ds(tt, nb, stride=t)
        new = []
        for lt in range(n_lt):
            h = a_s[lt, idx, :] * hs[lt] + b_s[lt, idx, :]
            b_s[lt, idx, :] = h
            new.append(h)
        return tuple(new)

    hs = lax.fori_loop(0, t, step, tuple(h_s[lt] for lt in range(n_lt)), unroll=8)
    for lt in range(n_lt):
        h_s[lt] = hs[lt]
    hseq = jnp.concatenate([b_s[lt] for lt in range(n_lt)], axis=-1)
    y_ref[...] = (hseq.reshape(nb, t, c) * gl_ref[...].astype(F32)).astype(BF16)


def _lru(xl, gl, cw, cb, wg, bg, lam, t):
    nb, s, c = xl.shape
    blk = pl.BlockSpec((nb, t, c), lambda i: (0, i, 0))
    full = lambda a: pl.BlockSpec(a.shape, lambda i: (0, 0))
    return pl.pallas_call(
        functools.partial(_lru_body, nb=nb, t=t),
        grid=(s // t,),
        in_specs=[blk, blk, full(cw), full(cb), full(wg), full(bg), full(lam)],
        out_specs=blk,
        out_shape=jax.ShapeDtypeStruct((nb, s, c), BF16),
        scratch_shapes=[pltpu.VMEM((nb, t + 8, c), F32), pltpu.VMEM((c // LANE, nb * t, LANE), F32),
                        pltpu.VMEM((c // LANE, nb * t, LANE), F32), pltpu.VMEM((c // LANE, nb, LANE), F32)],
        compiler_params=_params("arbitrary"),
        name="rg_lru",
    )(xl, gl, cw, cb, wg, bg, lam)


def _merge_body(x_ref, o_ref, y_ref, ga_ref, gb_ref, wa_ref, wb_ref, wo_ref, g2_ref, *rest, router):
    ua = _dot(o_ref[...], wa_ref[...])
    ub = _dot(y_ref[...], wb_ref[...])
    merged = ga_ref[...].astype(F32) * ua + gb_ref[...].astype(F32) * ub
    xn = x_ref[...] + _dot(merged.astype(BF16), wo_ref[...])
    h2 = _rms(xn, g2_ref[...])
    if router:
        rhi_ref, rlo_ref, xn_ref, h2_ref, lg_ref = rest
        hi = h2.astype(BF16)
        lo = (h2 - hi.astype(F32)).astype(BF16)
        lg_ref[...] = _dot(hi, rhi_ref[...]) + (_dot(lo, rhi_ref[...]) + _dot(hi, rlo_ref[...]))
        h2_ref[...] = h2
    else:
        xn_ref, h2_ref = rest
        h2_ref[...] = h2.astype(BF16)
    xn_ref[...] = xn


def _merge(x, o, y, ga, gb, wa, wb, wo, g2, router, tm):
    n, d = x.shape
    row = lambda a: pl.BlockSpec((tm, a.shape[1]), lambda i: (i, 0))
    full = lambda a: pl.BlockSpec(a.shape, lambda i: (0, 0))
    ins = [x, o, y, ga, gb]
    consts = [wa, wb, wo, g2] + list(router or ())
    out_shape = [jax.ShapeDtypeStruct((n, d), F32), jax.ShapeDtypeStruct((n, d), F32 if router else BF16)]
    if router:
        out_shape.append(jax.ShapeDtypeStruct((n, LANE), F32))
    return pl.pallas_call(
        functools.partial(_merge_body, router=bool(router)),
        grid=(n // tm,),
        in_specs=[row(a) for a in ins] + [full(a) for a in consts],
        out_specs=[pl.BlockSpec((tm, sh.shape[1]), lambda i: (i, 0)) for sh in out_shape],
        out_shape=out_shape,
        compiler_params=_params("parallel"),
        name="merge_router" if router else "merge",
    )(*ins, *consts)


def _ffn_body(h_ref, x_ref, wg_ref, wu_ref, wd_ref, o_ref, acc_ref):
    j = pl.program_id(1)
    h = h_ref[...]
    act = (jax.nn.silu(_dot(h, wg_ref[...])) * _dot(h, wu_ref[...])).astype(BF16)
    part = _dot(act, wd_ref[...])

    @pl.when(j == 0)
    def _():
        acc_ref[...] = x_ref[...] + part

    @pl.when(j > 0)
    def _():
        acc_ref[...] += part

    @pl.when(j == pl.num_programs(1) - 1)
    def _():
        o_ref[...] = acc_ref[...]


def _ffn(h, x, wg, wu, wd, tm, tf):
    n, d = x.shape
    f = wg.shape[1]
    return pl.pallas_call(
        _ffn_body,
        grid=(n // tm, f // tf),
        in_specs=[pl.BlockSpec((tm, d), lambda i, j: (i, 0)), pl.BlockSpec((tm, d), lambda i, j: (i, 0)),
                  pl.BlockSpec((d, tf), lambda i, j: (0, j)), pl.BlockSpec((d, tf), lambda i, j: (0, j)),
                  pl.BlockSpec((tf, d), lambda i, j: (j, 0))],
        out_specs=pl.BlockSpec((tm, d), lambda i, j: (i, 0)),
        out_shape=jax.ShapeDtypeStruct((n, d), F32),
        scratch_shapes=[pltpu.VMEM((tm, d), F32)],
        compiler_params=_params("parallel", "arbitrary"),
        name="dense_ffn",
    )(h, x, wg, wu, wd)


def _row_copy(src_hbm, dst, src_row, dst_row, sem):
    return pltpu.make_async_copy(src_hbm.at[pl.ds(src_row, 1), :], dst.at[pl.ds(dst_row, 1), :], sem)


def _gather_body(tok_ref, src_hbm, o_ref, sem, *, tm):
    def issue(r, carry):
        _row_copy(src_hbm, o_ref, tok_ref[r], r, sem).start()
        return carry

    lax.fori_loop(0, tm, issue, 0, unroll=8)
    pltpu.make_async_copy(src_hbm.at[pl.ds(0, tm), :], o_ref, sem).wait()


def _gather_rows(tok, src, tm):
    r = tok.shape[0]
    d = src.shape[1]
    return pl.pallas_call(
        functools.partial(_gather_body, tm=tm),
        grid=(r // tm,),
        in_specs=[pl.BlockSpec((tm,), lambda i: (i,), memory_space=pltpu.SMEM),
                  pl.BlockSpec(memory_space=pl.ANY)],
        out_specs=pl.BlockSpec((tm, d), lambda i: (i, 0)),
        out_shape=jax.ShapeDtypeStruct((r, d), src.dtype),
        scratch_shapes=[pltpu.SemaphoreType.DMA(())],
        compiler_params=_params("arbitrary"),
        name="moe_gather",
    )(tok, src)


def _moe_ffn_body(te_ref, nv_ref, x_ref, w_ref, wg_ref, wu_ref, wd_ref, o_ref, acc_ref):
    i = pl.program_id(0)
    j = pl.program_id(1)
    last = pl.num_programs(1) - 1

    @pl.when(i < nv_ref[0])
    def _():
        h = x_ref[...].astype(BF16)
        act = (jax.nn.silu(_dot(h, wg_ref[0])) * _dot(h, wu_ref[0])).astype(BF16)
        part = _dot(act, wd_ref[0])

        @pl.when(j == 0)
        def _():
            acc_ref[...] = part

        @pl.when(j > 0)
        def _():
            acc_ref[...] += part

        @pl.when(j == last)
        def _():
            o_ref[...] = acc_ref[...] * w_ref[...]

    @pl.when((i >= nv_ref[0]) & (j == last))
    def _():
        o_ref[...] = jnp.zeros_like(o_ref)


def _moe_ffn(tile_e, n_valid, xg, w_rows, wg, wu, wd, tm, tf):
    r, d = xg.shape
    f = wg.shape[2]
    grid_spec = pltpu.PrefetchScalarGridSpec(
        num_scalar_prefetch=2,
        grid=(r // tm, f // tf),
        in_specs=[pl.BlockSpec((tm, d), lambda i, j, te, nv: (i, 0)),
                  pl.BlockSpec((tm, 1), lambda i, j, te, nv: (i, 0)),
                  pl.BlockSpec((1, d, tf), lambda i, j, te, nv: (te[i], 0, j)),
                  pl.BlockSpec((1, d, tf), lambda i, j, te, nv: (te[i], 0, j)),
                  pl.BlockSpec((1, tf, d), lambda i, j, te, nv: (te[i], j, 0))],
        out_specs=pl.BlockSpec((tm, d), lambda i, j, te, nv: (i, 0)),
        scratch_shapes=[pltpu.VMEM((tm, d), F32)],
    )
    return pl.pallas_call(
        _moe_ffn_body,
        grid_spec=grid_spec,
        out_shape=jax.ShapeDtypeStruct((r, d), F32),
        compiler_params=_params("arbitrary", "arbitrary"),
        name="moe_ffn",
    )(tile_e, n_valid, xg, w_rows, wg, wu, wd)


def _combine_body(d_ref, x_ref, yb_hbm, o_ref, buf, sem, *, tm):
    def issue(r, carry):
        for kk in range(TOP_K):
            _row_copy(yb_hbm, buf.at[kk], d_ref[TOP_K * r + kk], r, sem).start()
        return carry

    lax.fori_loop(0, tm, issue, 0, unroll=4)
    for kk in range(TOP_K):
        pltpu.make_async_copy(yb_hbm.at[pl.ds(0, tm), :], buf.at[kk], sem).wait()
    acc = x_ref[...]
    for kk in range(TOP_K):
        acc = acc + buf[kk]
    o_ref[...] = acc


def _combine(dest, x, yb, tm):
    n, d = x.shape
    return pl.pallas_call(
        functools.partial(_combine_body, tm=tm),
        grid=(n // tm,),
        in_specs=[pl.BlockSpec((TOP_K * tm,), lambda i: (i,), memory_space=pltpu.SMEM),
                  pl.BlockSpec((tm, d), lambda i: (i, 0)),
                  pl.BlockSpec(memory_space=pl.ANY)],
        out_specs=pl.BlockSpec((tm, d), lambda i: (i, 0)),
        out_shape=jax.ShapeDtypeStruct((n, d), F32),
        scratch_shapes=[pltpu.VMEM((TOP_K, tm, d), F32), pltpu.SemaphoreType.DMA(())],
        compiler_params=_params("arbitrary"),
        name="moe_combine",
    )(dest, x, yb)


def _route(logits, tm):
    n = logits.shape[0]
    top_logit, top_e = lax.top_k(logits[:, :N_EXPERTS], TOP_K)
    top_w = jax.nn.softmax(top_logit, axis=-1)
    flat_e = top_e.reshape(-1).astype(jnp.int32)
    onehot = (flat_e[:, None] == jnp.arange(N_EXPERTS, dtype=jnp.int32)[None, :]).astype(jnp.int32)
    csum = jnp.cumsum(onehot, axis=0)
    rank = jnp.sum((csum - onehot) * onehot, axis=1)
    counts = csum[-1]
    padded = (counts + tm - 1) // tm * tm
    pend = jnp.cumsum(padded)
    pstart = pend - padded
    dest = (pstart[flat_e] + rank).astype(jnp.int32)
    n_rows = -(-(n * TOP_K + N_EXPERTS * (tm - 1)) // tm) * tm
    flat_tok = jnp.repeat(jnp.arange(n, dtype=jnp.int32), TOP_K)
    tok_buf = jnp.zeros((n_rows,), jnp.int32).at[dest].set(flat_tok)
    w_buf = jnp.zeros((n_rows,), F32).at[dest].set(top_w.reshape(-1))
    tile_start = jnp.arange(n_rows // tm, dtype=jnp.int32) * tm
    tile_e = jnp.minimum(jnp.sum(tile_start[:, None] >= pend[None, :], axis=1), N_EXPERTS - 1)
    n_valid = (pend[-1] // tm).astype(jnp.int32).reshape(1)
    return tok_buf, w_buf.reshape(-1, 1), dest, tile_e.astype(jnp.int32), n_valid


def _head_slots(w, heads, width):
    k = w.shape[0]
    return jnp.pad(w.reshape(k, heads, width), ((0, 0), (0, 0), (0, LANE - width)))


def _rotate_half_slots(w3):
    half = QK_ROPE_DIM // 2
    lo, mid, hi = QK_NOPE_DIM, QK_NOPE_DIM + half, QK_HEAD_DIM
    out = jnp.zeros_like(w3)
    out = out.at[:, :, lo:mid].set(-w3[:, :, mid:hi])
    out = out.at[:, :, mid:hi].set(w3[:, :, lo:mid])
    return out


def _rope_gains(g):
    half = QK_ROPE_DIM // 2
    lo, mid, hi = QK_NOPE_DIM, QK_NOPE_DIM + half, QK_HEAD_DIM
    main = jnp.pad(g, (0, LANE - QK_HEAD_DIM))
    rot = jnp.zeros((LANE,), F32).at[lo:mid].set(g[mid:hi]).at[mid:hi].set(g[lo:mid])
    return main.reshape(1, LANE), rot.reshape(1, LANE)


def _rope_tables(positions):
    half = QK_ROPE_DIM // 2
    inv_freq = ROPE_BASE ** (-jnp.arange(0, QK_ROPE_DIM, 2, dtype=F32) / QK_ROPE_DIM)
    ang = positions.astype(F32).reshape(-1, 1) * inv_freq[None, :]
    n = ang.shape[0]
    pad = jnp.zeros((n, LANE - QK_HEAD_DIM), F32)
    cos = jnp.concatenate([jnp.ones((n, QK_NOPE_DIM), F32), jnp.cos(ang), jnp.cos(ang), pad], axis=-1)
    sin = jnp.concatenate([jnp.zeros((n, QK_NOPE_DIM), F32), jnp.sin(ang), jnp.sin(ang), pad], axis=-1)
    del half
    return cos, sin


def _block_diag(w):
    h, a, b = w.shape
    eye = jnp.eye(h, dtype=w.dtype)
    return (eye[:, None, :, None] * w[:, :, None, :]).reshape(h * a, h * b)


def kernel(x, positions, norm1_g, w_in, q_norm_g, w_uq, kv_norm_g, w_ukv, qk_q_g, qk_k_g, w_up_attn, conv_w, conv_b, w_rg, b_rg, w_ig, b_ig, lru_lambda, w_up_lru, w_o, norm2_g, ffn_w_gate, ffn_w_up, ffn_w_down, moe_router, moe_w_gate, moe_w_up, moe_w_down):
    batch, seq, d = x.shape
    n = batch * seq
    depth = norm1_g.shape[0]
    q_rank = q_norm_g.shape[1]
    kv_rank = kv_norm_g.shape[1]
    c_lru = conv_w.shape[2]
    head_w = d - 0
    lat = q_rank + kv_rank + QK_ROPE_DIM
    seg_w = -(-lat // LANE) * LANE
    widths = (seg_w, c_lru, c_lru, head_w, head_w)

    tm = 512
    tq = 512
    t_lru = 128
    tm_moe = 512
    tf_dense = ffn_w_gate.shape[2] // 2
    tf_moe = moe_w_gate.shape[3] // 4

    cos, sin = _rope_tables(positions)
    xf = x.reshape(n, d)
    for layer in range(depth):
        w_in_l = w_in[layer]
        w_in_p = jnp.concatenate(
            [w_in_l[:, :lat], jnp.zeros((d, seg_w - lat), F32), w_in_l[:, lat:]], axis=1).astype(BF16)
        seg, xl, gl, ga, gb = _in_proj(xf, norm1_g[layer].reshape(1, d), w_in_p, widths, tm)

        wq3 = _head_slots(w_uq[layer], MLA_HEADS, QK_HEAD_DIM)
        wkv3 = w_ukv[layer].reshape(kv_rank, MLA_HEADS, QK_NOPE_DIM + V_HEAD_DIM)
        ck_w = seg_w - q_rank
        wk3 = jnp.zeros((ck_w, MLA_HEADS, LANE), F32)
        wk3 = wk3.at[:kv_rank, :, :QK_NOPE_DIM].set(wkv3[:, :, :QK_NOPE_DIM])
        eye = jnp.eye(QK_ROPE_DIM, dtype=F32)
        wk3 = wk3.at[kv_rank:kv_rank + QK_ROPE_DIM, :, QK_NOPE_DIM:QK_HEAD_DIM].set(
            jnp.broadcast_to(eye[:, None, :], (QK_ROPE_DIM, MLA_HEADS, QK_ROPE_DIM)))
        wv = jnp.zeros((ck_w, MLA_HEADS * V_HEAD_DIM), F32).at[:kv_rank].set(
            wkv3[:, :, QK_NOPE_DIM:].reshape(kv_rank, MLA_HEADS * V_HEAD_DIM))
        flat = lambda w3: w3.reshape(w3.shape[0], MLA_HEADS * LANE).astype(BF16)
        gqm, gqr = _rope_gains(qk_q_g[layer])
        gkm, gkr = _rope_gains(qk_k_g[layer])
        q, k, v = _qkv(seg, cos, sin, q_norm_g[layer].reshape(1, -1), kv_norm_g[layer].reshape(1, -1),
                       flat(wq3), flat(_rotate_half_slots(wq3)), flat(wk3), flat(_rotate_half_slots(wk3)),
                       wv.astype(BF16), gqm, gqr, gkm, gkr, tm)
        o = _attention(q.reshape(batch, seq, -1), k.reshape(batch, seq, -1), v.reshape(batch, seq, -1), tq)

        wgate = jnp.concatenate([_block_diag(w_rg[layer]), _block_diag(w_ig[layer])], axis=1).astype(BF16)
        bgate = jnp.concatenate([b_rg[layer], b_ig[layer]]).reshape(1, -1)
        y = _lru(xl.reshape(batch, seq, c_lru), gl.reshape(batch, seq, c_lru), conv_w[layer],
                 conv_b[layer].reshape(1, -1), wgate, bgate, lru_lambda[layer].reshape(1, -1), t_lru)

        j = layer // 2
        is_moe = layer % 2 == 1
        router = None
        if is_moe:
            r_pad = jnp.pad(moe_router[j], ((0, 0), (0, LANE - N_EXPERTS)))
            r_hi = r_pad.astype(BF16)
            router = (r_hi, (r_pad - r_hi.astype(F32)).astype(BF16))
        outs = _merge(xf, o.reshape(n, -1), y.reshape(n, -1), ga, gb, w_up_attn[layer].astype(BF16),
                      w_up_lru[layer].astype(BF16), w_o[layer].astype(BF16), norm2_g[layer].reshape(1, d),
                      router, tm)
        if not is_moe:
            xn, h2 = outs
            xf = _ffn(h2, xn, ffn_w_gate[j].astype(BF16), ffn_w_up[j].astype(BF16),
                      ffn_w_down[j].astype(BF16), tm, tf_dense)
        else:
            xn, h2, logits = outs
            tok_buf, w_rows, dest, tile_e, n_valid = _route(logits, tm_moe)
            xg = _gather_rows(tok_buf, h2, tm_moe)
            yb = _moe_ffn(tile_e, n_valid, xg, w_rows, moe_w_gate[j].astype(BF16), moe_w_up[j].astype(BF16),
                          moe_w_down[j].astype(BF16), tm_moe, tf_moe)
            xf = _combine(dest, xn, yb, 256)
    return xf.reshape(batch, seq, d)
```

```python
import functools

import numpy as np
import jax
import jax.numpy as jnp
from jax import lax
from jax.experimental import pallas as pl
from jax.experimental.pallas import tpu as pltpu

BF16 = jnp.bfloat16
F32 = jnp.float32

MLA_HEADS = 8
QK_NOPE_DIM = 64
QK_ROPE_DIM = 32
V_HEAD_DIM = 64
QK_HEAD_DIM = QK_NOPE_DIM + QK_ROPE_DIM
ROPE_BASE = 10000.0
LRU_HEADS = 8
CONV_WIDTH = 4
LRU_C = 8.0
N_EXPERTS = 8
TOP_K = 2
RMS_EPS = 1e-6

LANE = 128
VMEM_LIMIT = 48 * 1024 * 1024


def _params(*sem):
    return pltpu.CompilerParams(dimension_semantics=sem, vmem_limit_bytes=VMEM_LIMIT)


def _rms(x, g):
    ms = jnp.mean(x * x, axis=-1, keepdims=True)
    return x * lax.rsqrt(ms + RMS_EPS) * g


def _dot(a, b):
    return jnp.dot(a, b, preferred_element_type=F32)


def _in_proj_body(x_ref, g_ref, w_ref, seg_ref, xl_ref, gl_ref, ga_ref, gb_ref, *, widths):
    h = _rms(x_ref[...], g_ref[...]).astype(BF16)
    o = np.cumsum((0,) + widths)
    seg_ref[...] = _dot(h, w_ref[:, o[0]:o[1]])
    xl_ref[...] = _dot(h, w_ref[:, o[1]:o[2]])
    gl_ref[...] = jax.nn.gelu(_dot(h, w_ref[:, o[2]:o[3]]), approximate=True).astype(BF16)
    ga_ref[...] = jax.nn.sigmoid(_dot(h, w_ref[:, o[3]:o[4]])).astype(BF16)
    gb_ref[...] = jax.nn.sigmoid(_dot(h, w_ref[:, o[4]:o[5]])).astype(BF16)


def _in_proj(x, g, w, widths, tm):
    n, d = x.shape
    dts = (F32, F32, BF16, BF16, BF16)
    row = lambda wd: pl.BlockSpec((tm, wd), lambda i: (i, 0))
    return pl.pallas_call(
        functools.partial(_in_proj_body, widths=widths),
        grid=(n // tm,),
        in_specs=[row(d), pl.BlockSpec((1, d), lambda i: (0, 0)),
                  pl.BlockSpec(w.shape, lambda i: (0, 0))],
        out_specs=[row(wd) for wd in widths],
        out_shape=[jax.ShapeDtypeStruct((n, wd), dt) for wd, dt in zip(widths, dts)],
        compiler_params=_params("parallel"),
        name="in_proj",
    )(x, g, w)


def _qkv_body(seg_ref, c_ref, s_ref, gq_ref, gkv_ref, wq_ref, wqr_ref, wk_ref, wkr_ref, wv_ref,
              gqm_ref, gqr_ref, gkm_ref, gkr_ref, q_ref, k_ref, v_ref, *, q_rank, kv_rank):
    cqn = _rms(seg_ref[:, 0:q_rank], gq_ref[...]).astype(BF16)
    ckvn = _rms(seg_ref[:, q_rank:q_rank + kv_rank], gkv_ref[...])
    ck = jnp.concatenate([ckvn, seg_ref[:, q_rank + kv_rank:]], axis=-1).astype(BF16)
    cos = c_ref[...]
    sin = s_ref[...]
    scale = QK_HEAD_DIM ** -0.5 * np.log2(np.e)

    def finish(a_all, b_all, gm, gr, out_ref, mul):
        cm = gm * cos * mul
        sm = gr * sin * mul
        for h in range(MLA_HEADS):
            a = a_all[:, h * LANE:(h + 1) * LANE]
            b = b_all[:, h * LANE:(h + 1) * LANE]
            inv = lax.rsqrt(jnp.sum(a * a, axis=-1, keepdims=True) * (1.0 / QK_HEAD_DIM) + RMS_EPS)
            out_ref[:, h * LANE:(h + 1) * LANE] = (inv * (a * cm + b * sm)).astype(BF16)

    finish(_dot(cqn, wq_ref[...]), _dot(cqn, wqr_ref[...]), gqm_ref[...], gqr_ref[...], q_ref, scale)
    finish(_dot(ck, wk_ref[...]), _dot(ck, wkr_ref[...]), gkm_ref[...], gkr_ref[...], k_ref, 1.0)
    v_ref[...] = _dot(ck, wv_ref[...]).astype(BF16)


def _qkv(seg, cos, sin, gq, gkv, wq, wqr, wk, wkr, wv, gqm, gqr, gkm, gkr, tm):
    n, sw = seg.shape
    q_rank = gq.shape[1]
    kv_rank = gkv.shape[1]
    hq = wq.shape[1]
    hv = wv.shape[1]
    row = lambda wd: pl.BlockSpec((tm, wd), lambda i: (i, 0))
    full = lambda a: pl.BlockSpec(a.shape, lambda i: (0, 0))
    consts = (gq, gkv, wq, wqr, wk, wkr, wv, gqm, gqr, gkm, gkr)
    return pl.pallas_call(
        functools.partial(_qkv_body, q_rank=q_rank, kv_rank=kv_rank),
        grid=(n // tm,),
        in_specs=[row(sw), row(LANE), row(LANE)] + [full(a) for a in consts],
        out_specs=[row(hq), row(hq), row(hv)],
        out_shape=[jax.ShapeDtypeStruct((n, hq), BF16), jax.ShapeDtypeStruct((n, hq), BF16),
                   jax.ShapeDtypeStruct((n, hv), BF16)],
        compiler_params=_params("parallel"),
        name="qkv_prep",
    )(seg, cos, sin, *consts)


def _attn_body(q_ref, k_ref, v_ref, o_ref, *, tq, heads):
    i = pl.program_id(2)
    row = lax.broadcasted_iota(jnp.int32, (tq, tq), 0)
    col = lax.broadcasted_iota(jnp.int32, (tq, tq), 1)
    causal = col <= row
    lane = lax.broadcasted_iota(jnp.int32, (tq, LANE), 1)
    qs = [q_ref[0, :, hh * LANE:(hh + 1) * LANE] for hh in range(heads)]

    def per_head(vals):
        out = vals[heads - 1]
        for hh in range(heads - 2, -1, -1):
            out = jnp.where(lane < (hh + 1) * V_HEAD_DIM, vals[hh], out)
        return out

    def kv_step(j, carry, masked):
        ms, ls, acc = carry
        start = pl.multiple_of(j * tq, tq)
        vs = v_ref[0, pl.ds(start, tq), :]
        vlane = lax.broadcasted_iota(jnp.int32, vs.shape, 1)
        m_out, l_out, alphas, ps, vparts = [], [], [], [], []
        for hh in range(heads):
            ks = k_ref[0, pl.ds(start, tq), hh * LANE:(hh + 1) * LANE]
            s = lax.dot_general(qs[hh], ks, (((1,), (1,)), ((), ())), preferred_element_type=F32)
            if masked:
                s = jnp.where(causal, s, -jnp.inf)
            m_new = jnp.maximum(ms[hh], jnp.max(s, axis=-1, keepdims=True))
            p = jnp.exp2(s - m_new)
            alpha = jnp.exp2(ms[hh] - m_new)
            m_out.append(m_new)
            l_out.append(alpha * ls[hh] + jnp.sum(p, axis=-1, keepdims=True))
            alphas.append(alpha)
            ps.append(p.astype(BF16))
            mine = (vlane >= hh * V_HEAD_DIM) & (vlane < (hh + 1) * V_HEAD_DIM)
            vparts.append(jnp.where(mine, vs, jnp.zeros_like(vs)))
        pv = _dot(jnp.concatenate(ps, axis=1), jnp.concatenate(vparts, axis=0))
        return tuple(m_out), tuple(l_out), per_head(alphas) * acc + pv

    init = (tuple(jnp.full((tq, 1), -jnp.inf, F32) for _ in range(heads)),
            tuple(jnp.zeros((tq, 1), F32) for _ in range(heads)),
            jnp.zeros((tq, LANE), F32))
    carry = lax.fori_loop(0, i, functools.partial(kv_step, masked=False), init)
    _, ls, acc = kv_step(i, carry, masked=True)
    o_ref[0] = (acc / per_head(ls)).astype(BF16)


def _attention(q, k, v, tq):
    b, s, hq = q.shape
    heads = LANE // V_HEAD_DIM
    n_hp = hq // (heads * LANE)
    return pl.pallas_call(
        functools.partial(_attn_body, tq=tq, heads=heads),
        grid=(b, n_hp, s // tq),
        in_specs=[pl.BlockSpec((1, tq, heads * LANE), lambda bi, hp, i: (bi, i, hp)),
                  pl.BlockSpec((1, s, heads * LANE), lambda bi, hp, i: (bi, 0, hp)),
                  pl.BlockSpec((1, s, LANE), lambda bi, hp, i: (bi, 0, hp))],
        out_specs=pl.BlockSpec((1, tq, LANE), lambda bi, hp, i: (bi, i, hp)),
        out_shape=jax.ShapeDtypeStruct((b, s, v.shape[2]), BF16),
        compiler_params=_params("parallel", "parallel", "arbitrary"),
        name="attention",
    )(q, k, v)


def _lru_body(x_ref, gl_ref, cw_ref, cb_ref, wg_ref, bg_ref, lam_ref, y_ref,
              ext_s, a_s, b_s, h_s, *, nb, t):
    c = x_ref.shape[2]
    pad = 8

    @pl.when(pl.program_id(0) == 0)
    def _():
        ext_s[:, 0:pad, :] = jnp.zeros((nb, pad, c), F32)
        h_s[...] = jnp.zeros_like(h_s)

    ext_s[:, pad:pad + t, :] = x_ref[...]
    xc = cb_ref[...][None]
    for kk in range(CONV_WIDTH):
        off = pad - (CONV_WIDTH - 1) + kk
        xc = xc + ext_s[:, off:off + t, :] * cw_ref[kk:kk + 1, :][None]
    ext_s[:, 0:pad, :] = ext_s[:, t:t + pad, :]

    xc2 = xc.reshape(nb * t, c)
    g = _dot(xc2.astype(BF16), wg_ref[...]) + bg_ref[...]
    r = jax.nn.sigmoid(g[:, :c])
    ig = jax.nn.sigmoid(g[:, c:])
    log_a = -LRU_C * r * jax.nn.softplus(-lam_ref[...])
    a = jnp.exp(log_a)
    mult = jnp.sqrt(-jnp.tanh(log_a) * (a * a + 1.0))
    bv = mult * ig * xc2
    n_lt = c // LANE
    for lt in range(n_lt):
        a_s[lt] = a[:, lt * LANE:(lt + 1) * LANE]
        b_s[lt] = bv[:, lt * LANE:(lt + 1) * LANE]

    def step(tt, hs):
        idx = pl.ds(tt, nb, stride=t)
        new = []
        for lt in range(n_lt):
            h = a_s[lt, idx, :] * hs[lt] + b_s[lt, idx, :]
            b_s[lt, idx, :] = h
            new.append(h)
        return tuple(new)

    hs = lax.fori_loop(0, t, step, tuple(h_s[lt] for lt in range(n_lt)), unroll=8)
    for lt in range(n_lt):
        h_s[lt] = hs[lt]
    hseq = jnp.concatenate([b_s[lt] for lt in range(n_lt)], axis=-1)
    y_ref[...] = (hseq.reshape(nb, t, c) * gl_ref[...].astype(F32)).astype(BF16)


def _lru(xl, gl, cw, cb, wg, bg, lam, t):
    nb, s, c = xl.shape
    blk = pl.BlockSpec((nb, t, c), lambda i: (0, i, 0))
    full = lambda a: pl.BlockSpec(a.shape, lambda i: (0, 0))
    return pl.pallas_call(
        functools.partial(_lru_body, nb=nb, t=t),
        grid=(s // t,),
        in_specs=[blk, blk, full(cw), full(cb), full(wg), full(bg), full(lam)],
        out_specs=blk,
        out_shape=jax.ShapeDtypeStruct((nb, s, c), BF16),
        scratch_shapes=[pltpu.VMEM((nb, t + 8, c), F32), pltpu.VMEM((c // LANE, nb * t, LANE), F32),
                        pltpu.VMEM((c // LANE, nb * t, LANE), F32), pltpu.VMEM((c // LANE, nb, LANE), F32)],
        compiler_params=_params("arbitrary"),
        name="rg_lru",
    )(xl, gl, cw, cb, wg, bg, lam)


def _merge_body(x_ref, o_ref, y_ref, ga_ref, gb_ref, wa_ref, wb_ref, wo_ref, g2_ref, *rest, router):
    ua = _dot(o_ref[...], wa_ref[...])
    ub = _dot(y_ref[...], wb_ref[...])
    merged = ga_ref[...].astype(F32) * ua + gb_ref[...].astype(F32) * ub
    xn = x_ref[...] + _dot(merged.astype(BF16), wo_ref[...])
    h2 = _rms(xn, g2_ref[...])
    if router:
        rhi_ref, rlo_ref, xn_ref, h2_ref, lg_ref = rest
        hi = h2.astype(BF16)
        lo = (h2 - hi.astype(F32)).astype(BF16)
        lg_ref[...] = _dot(hi, rhi_ref[...]) + (_dot(lo, rhi_ref[...]) + _dot(hi, rlo_ref[...]))
        h2_ref[...] = h2
    else:
        xn_ref, h2_ref = rest
        h2_ref[...] = h2.astype(BF16)
    xn_ref[...] = xn


def _merge(x, o, y, ga, gb, wa, wb, wo, g2, router, tm):
    n, d = x.shape
    row = lambda a: pl.BlockSpec((tm, a.shape[1]), lambda i: (i, 0))
    full = lambda a: pl.BlockSpec(a.shape, lambda i: (0, 0))
    ins = [x, o, y, ga, gb]
    consts = [wa, wb, wo, g2] + list(router or ())
    out_shape = [jax.ShapeDtypeStruct((n, d), F32), jax.ShapeDtypeStruct((n, d), F32 if router else BF16)]
    if router:
        out_shape.append(jax.ShapeDtypeStruct((n, LANE), F32))
    return pl.pallas_call(
        functools.partial(_merge_body, router=bool(router)),
        grid=(n // tm,),
        in_specs=[row(a) for a in ins] + [full(a) for a in consts],
        out_specs=[pl.BlockSpec((tm, sh.shape[1]), lambda i: (i, 0)) for sh in out_shape],
        out_shape=out_shape,
        compiler_params=_params("parallel"),
        name="merge_router" if router else "merge",
    )(*ins, *consts)


def _ffn_body(h_ref, x_ref, wg_ref, wu_ref, wd_ref, o_ref, acc_ref):
    j = pl.program_id(1)
    h = h_ref[...]
    act = (jax.nn.silu(_dot(h, wg_ref[...])) * _dot(h, wu_ref[...])).astype(BF16)
    part = _dot(act, wd_ref[...])

    @pl.when(j == 0)
    def _():
        acc_ref[...] = x_ref[...] + part

    @pl.when(j > 0)
    def _():
        acc_ref[...] += part

    @pl.when(j == pl.num_programs(1) - 1)
    def _():
        o_ref[...] = acc_ref[...]


def _ffn(h, x, wg, wu, wd, tm, tf):
    n, d = x.shape
    f = wg.shape[1]
    return pl.pallas_call(
        _ffn_body,
        grid=(n // tm, f // tf),
        in_specs=[pl.BlockSpec((tm, d), lambda i, j: (i, 0)), pl.BlockSpec((tm, d), lambda i, j: (i, 0)),
                  pl.BlockSpec((d, tf), lambda i, j: (0, j)), pl.BlockSpec((d, tf), lambda i, j: (0, j)),
                  pl.BlockSpec((tf, d), lambda i, j: (j, 0))],
        out_specs=pl.BlockSpec((tm, d), lambda i, j: (i, 0)),
        out_shape=jax.ShapeDtypeStruct((n, d), F32),
        scratch_shapes=[pltpu.VMEM((tm, d), F32)],
        compiler_params=_params("parallel", "arbitrary"),
        name="dense_ffn",
    )(h, x, wg, wu, wd)


def _row_copy(src_hbm, dst, src_row, dst_row, sem):
    return pltpu.make_async_copy(src_hbm.at[pl.ds(src_row, 1), :], dst.at[pl.ds(dst_row, 1), :], sem)


MOE_PREFETCH_STEPS = 2


def _moe_ffn_body(te_ref, nv_ref, tok_ref, tokn_ref, w_ref, h_hbm, wg_ref, wu_ref, wd_ref, o_ref,
                  xbuf, xb16, acc_ref, sem, *, tm):
    i = pl.program_id(0)
    j = pl.program_id(1)
    last = pl.num_programs(1) - 1
    slot = i % 2
    part_rows = tm // MOE_PREFETCH_STEPS

    def wait_tile(s):
        pltpu.make_async_copy(h_hbm.at[pl.ds(0, tm), :], xbuf.at[s], sem.at[s]).wait()

    @pl.when((i == 0) & (j == 0))
    def _():
        def issue(r, carry):
            _row_copy(h_hbm, xbuf.at[0], tok_ref[r], r, sem.at[0]).start()
            return carry

        lax.fori_loop(0, tm, issue, 0, unroll=8)

    @pl.when(j == 0)
    def _():
        wait_tile(slot)
        xb16[...] = xbuf[slot].astype(BF16)

    def prefetch(part):
        for r in range(part * part_rows, (part + 1) * part_rows):
            _row_copy(h_hbm, xbuf.at[1 - slot], tokn_ref[r], r, sem.at[1 - slot]).start()

    def compute():
        h = xb16[...]
        act = (jax.nn.silu(_dot(h, wg_ref[0])) * _dot(h, wu_ref[0])).astype(BF16)
        part = _dot(act, wd_ref[0])

        @pl.when(j == 0)
        def _():
            acc_ref[...] = part

        @pl.when(j > 0)
        def _():
            acc_ref[...] += part

        @pl.when(j == last)
        def _():
            o_ref[...] = acc_ref[...] * w_ref[...]

    valid = i < nv_ref[0]
    for part in range(MOE_PREFETCH_STEPS):
        @pl.when(valid & (j == part))
        def _(part=part):
            prefetch(part)
            compute()

        @pl.when(jnp.logical_not(valid) & (j == part))
        def _(part=part):
            prefetch(part)

    @pl.when(valid & (j >= MOE_PREFETCH_STEPS))
    def _():
        compute()

    @pl.when(jnp.logical_not(valid) & (j == last))
    def _():
        o_ref[...] = jnp.zeros_like(o_ref)

    @pl.when((i == pl.num_programs(0) - 1) & (j == last))
    def _():
        wait_tile(1 - slot)


def _moe_ffn(tile_e, n_valid, tok, h, w_rows, wg, wu, wd, tm, tf):
    r = tok.shape[0]
    d = h.shape[1]
    f = wg.shape[2]
    n_tiles = r // tm
    assert f // tf > MOE_PREFETCH_STEPS and tm % MOE_PREFETCH_STEPS == 0
    grid_spec = pltpu.PrefetchScalarGridSpec(
        num_scalar_prefetch=2,
        grid=(n_tiles, f // tf),
        in_specs=[pl.BlockSpec((tm,), lambda i, j, te, nv: (i,), memory_space=pltpu.SMEM),
                  pl.BlockSpec((tm,), lambda i, j, te, nv: (jnp.minimum(i + 1, n_tiles - 1),),
                               memory_space=pltpu.SMEM),
                  pl.BlockSpec((tm, 1), lambda i, j, te, nv: (i, 0)),
                  pl.BlockSpec(memory_space=pl.ANY),
                  pl.BlockSpec((1, d, tf), lambda i, j, te, nv: (te[i], 0, j)),
                  pl.BlockSpec((1, d, tf), lambda i, j, te, nv: (te[i], 0, j)),
                  pl.BlockSpec((1, tf, d), lambda i, j, te, nv: (te[i], j, 0))],
        out_specs=pl.BlockSpec((tm, d), lambda i, j, te, nv: (i, 0)),
        scratch_shapes=[pltpu.VMEM((2, tm, d), F32), pltpu.VMEM((tm, d), BF16), pltpu.VMEM((tm, d), F32),
                        pltpu.SemaphoreType.DMA((2,))],
    )
    return pl.pallas_call(
        functools.partial(_moe_ffn_body, tm=tm),
        grid_spec=grid_spec,
        out_shape=jax.ShapeDtypeStruct((r, d), F32),
        compiler_params=_params("arbitrary", "arbitrary"),
        name="moe_ffn",
    )(tile_e, n_valid, tok, tok, w_rows, h, wg, wu, wd)


def _combine_body(d_ref, x_ref, yb_hbm, o_ref, buf, sem, *, tm):
    def issue(r, carry):
        for kk in range(TOP_K):
            _row_copy(yb_hbm, buf.at[kk], d_ref[TOP_K * r + kk], r, sem).start()
        return carry

    lax.fori_loop(0, tm, issue, 0, unroll=4)
    for kk in range(TOP_K):
        pltpu.make_async_copy(yb_hbm.at[pl.ds(0, tm), :], buf.at[kk], sem).wait()
    acc = x_ref[...]
    for kk in range(TOP_K):
        acc = acc + buf[kk]
    o_ref[...] = acc


def _combine(dest, x, yb, tm):
    n, d = x.shape
    return pl.pallas_call(
        functools.partial(_combine_body, tm=tm),
        grid=(n // tm,),
        in_specs=[pl.BlockSpec((TOP_K * tm,), lambda i: (i,), memory_space=pltpu.SMEM),
                  pl.BlockSpec((tm, d), lambda i: (i, 0)),
                  pl.BlockSpec(memory_space=pl.ANY)],
        out_specs=pl.BlockSpec((tm, d), lambda i: (i, 0)),
        out_shape=jax.ShapeDtypeStruct((n, d), F32),
        scratch_shapes=[pltpu.VMEM((TOP_K, tm, d), F32), pltpu.SemaphoreType.DMA(())],
        compiler_params=_params("arbitrary"),
        name="moe_combine",
    )(dest, x, yb)


def _route(logits, tm):
    n = logits.shape[0]
    top_logit, top_e = lax.top_k(logits[:, :N_EXPERTS], TOP_K)
    top_w = jax.nn.softmax(top_logit, axis=-1)
    flat_e = top_e.reshape(-1).astype(jnp.int32)
    onehot = (flat_e[:, None] == jnp.arange(N_EXPERTS, dtype=jnp.int32)[None, :]).astype(jnp.int32)
    csum = jnp.cumsum(onehot, axis=0)
    rank = jnp.sum((csum - onehot) * onehot, axis=1)
    counts = csum[-1]
    padded = (counts + tm - 1) // tm * tm
    pend = jnp.cumsum(padded)
    pstart = pend - padded
    dest = (pstart[flat_e] + rank).astype(jnp.int32)
    n_rows = -(-(n * TOP_K + N_EXPERTS * (tm - 1)) // tm) * tm
    flat_tok = jnp.repeat(jnp.arange(n, dtype=jnp.int32), TOP_K)
    packed = jnp.stack([flat_tok, lax.bitcast_convert_type(top_w.reshape(-1), jnp.int32)], axis=1)
    buf = jnp.zeros((n_rows, 2), jnp.int32).at[dest].set(packed, unique_indices=True, mode="promise_in_bounds")
    tok_buf = buf[:, 0]
    w_buf = lax.bitcast_convert_type(buf[:, 1], F32)
    tile_start = jnp.arange(n_rows // tm, dtype=jnp.int32) * tm
    tile_e = jnp.minimum(jnp.sum(tile_start[:, None] >= pend[None, :], axis=1), N_EXPERTS - 1)
    n_valid = (pend[-1] // tm).astype(jnp.int32).reshape(1)
    return tok_buf, w_buf.reshape(-1, 1), dest, tile_e.astype(jnp.int32), n_valid


def _head_slots(w, heads, width):
    k = w.shape[0]
    return jnp.pad(w.reshape(k, heads, width), ((0, 0), (0, 0), (0, LANE - width)))


def _rotate_half_slots(w3):
    half = QK_ROPE_DIM // 2
    lo, mid, hi = QK_NOPE_DIM, QK_NOPE_DIM + half, QK_HEAD_DIM
    out = jnp.zeros_like(w3)
    out = out.at[:, :, lo:mid].set(-w3[:, :, mid:hi])
    out = out.at[:, :, mid:hi].set(w3[:, :, lo:mid])
    return out


def _rope_gains(g):
    half = QK_ROPE_DIM // 2
    lo, mid, hi = QK_NOPE_DIM, QK_NOPE_DIM + half, QK_HEAD_DIM
    main = jnp.pad(g, (0, LANE - QK_HEAD_DIM))
    rot = jnp.zeros((LANE,), F32).at[lo:mid].set(g[mid:hi]).at[mid:hi].set(g[lo:mid])
    return main.reshape(1, LANE), rot.reshape(1, LANE)


def _rope_tables(positions):
    half = QK_ROPE_DIM // 2
    inv_freq = ROPE_BASE ** (-jnp.arange(0, QK_ROPE_DIM, 2, dtype=F32) / QK_ROPE_DIM)
    ang = positions.astype(F32).reshape(-1, 1) * inv_freq[None, :]
    n = ang.shape[0]
    pad = jnp.zeros((n, LANE - QK_HEAD_DIM), F32)
    cos = jnp.concatenate([jnp.ones((n, QK_NOPE_DIM), F32), jnp.cos(ang), jnp.cos(ang), pad], axis=-1)
    sin = jnp.concatenate([jnp.zeros((n, QK_NOPE_DIM), F32), jnp.sin(ang), jnp.sin(ang), pad], axis=-1)
    del half
    return cos, sin


def _block_diag(w):
    h, a, b = w.shape
    eye = jnp.eye(h, dtype=w.dtype)
    return (eye[:, None, :, None] * w[:, :, None, :]).reshape(h * a, h * b)


def kernel(x, positions, norm1_g, w_in, q_norm_g, w_uq, kv_norm_g, w_ukv, qk_q_g, qk_k_g, w_up_attn, conv_w, conv_b, w_rg, b_rg, w_ig, b_ig, lru_lambda, w_up_lru, w_o, norm2_g, ffn_w_gate, ffn_w_up, ffn_w_down, moe_router, moe_w_gate, moe_w_up, moe_w_down):
    batch, seq, d = x.shape
    n = batch * seq
    depth = norm1_g.shape[0]
    q_rank = q_norm_g.shape[1]
    kv_rank = kv_norm_g.shape[1]
    c_lru = conv_w.shape[2]
    head_w = d - 0
    lat = q_rank + kv_rank + QK_ROPE_DIM
    seg_w = -(-lat // LANE) * LANE
    widths = (seg_w, c_lru, c_lru, head_w, head_w)

    tm = 512
    tq = 512
    t_lru = 128
    tm_moe = 512
    tf_dense = ffn_w_gate.shape[2] // 2
    tf_moe = moe_w_gate.shape[3] // 4

    cos, sin = _rope_tables(positions)
    xf = x.reshape(n, d)
    for layer in range(depth):
        w_in_l = w_in[layer]
        w_in_p = jnp.concatenate(
            [w_in_l[:, :lat], jnp.zeros((d, seg_w - lat), F32), w_in_l[:, lat:]], axis=1).astype(BF16)
        seg, xl, gl, ga, gb = _in_proj(xf, norm1_g[layer].reshape(1, d), w_in_p, widths, tm)

        wq3 = _head_slots(w_uq[layer], MLA_HEADS, QK_HEAD_DIM)
        wkv3 = w_ukv[layer].reshape(kv_rank, MLA_HEADS, QK_NOPE_DIM + V_HEAD_DIM)
        ck_w = seg_w - q_rank
        wk3 = jnp.zeros((ck_w, MLA_HEADS, LANE), F32)
        wk3 = wk3.at[:kv_rank, :, :QK_NOPE_DIM].set(wkv3[:, :, :QK_NOPE_DIM])
        eye = jnp.eye(QK_ROPE_DIM, dtype=F32)
        wk3 = wk3.at[kv_rank:kv_rank + QK_ROPE_DIM, :, QK_NOPE_DIM:QK_HEAD_DIM].set(
            jnp.broadcast_to(eye[:, None, :], (QK_ROPE_DIM, MLA_HEADS, QK_ROPE_DIM)))
        wv = jnp.zeros((ck_w, MLA_HEADS * V_HEAD_DIM), F32).at[:kv_rank].set(
            wkv3[:, :, QK_NOPE_DIM:].reshape(kv_rank, MLA_HEADS * V_HEAD_DIM))
        flat = lambda w3: w3.reshape(w3.shape[0], MLA_HEADS * LANE).astype(BF16)
        gqm, gqr = _rope_gains(qk_q_g[layer])
        gkm, gkr = _rope_gains(qk_k_g[layer])
        q, k, v = _qkv(seg, cos, sin, q_norm_g[layer].reshape(1, -1), kv_norm_g[layer].reshape(1, -1),
                       flat(wq3), flat(_rotate_half_slots(wq3)), flat(wk3), flat(_rotate_half_slots(wk3)),
                       wv.astype(BF16), gqm, gqr, gkm, gkr, tm)
        o = _attention(q.reshape(batch, seq, -1), k.reshape(batch, seq, -1), v.reshape(batch, seq, -1), tq)

        wgate = jnp.concatenate([_block_diag(w_rg[layer]), _block_diag(w_ig[layer])], axis=1).astype(BF16)
        bgate = jnp.concatenate([b_rg[layer], b_ig[layer]]).reshape(1, -1)
        y = _lru(xl.reshape(batch, seq, c_lru), gl.reshape(batch, seq, c_lru), conv_w[layer],
                 conv_b[layer].reshape(1, -1), wgate, bgate, lru_lambda[layer].reshape(1, -1), t_lru)

        j = layer // 2
        is_moe = layer % 2 == 1
        router = None
        if is_moe:
            r_pad = jnp.pad(moe_router[j], ((0, 0), (0, LANE - N_EXPERTS)))
            r_hi = r_pad.astype(BF16)
            router = (r_hi, (r_pad - r_hi.astype(F32)).astype(BF16))
        outs = _merge(xf, o.reshape(n, -1), y.reshape(n, -1), ga, gb, w_up_attn[layer].astype(BF16),
                      w_up_lru[layer].astype(BF16), w_o[layer].astype(BF16), norm2_g[layer].reshape(1, d),
                      router, tm)
        if not is_moe:
            xn, h2 = outs
            xf = _ffn(h2, xn, ffn_w_gate[j].astype(BF16), ffn_w_up[j].astype(BF16),
                      ffn_w_down[j].astype(BF16), tm, tf_dense)
        else:
            xn, h2, logits = outs
            tok_buf, w_rows, dest, tile_e, n_valid = _route(logits, tm_moe)
            yb = _moe_ffn(tile_e, n_valid, tok_buf, h2, w_rows, moe_w_gate[j].astype(BF16), moe_w_up[j].astype(BF16),
                          moe_w_down[j].astype(BF16), tm_moe, tf_moe)
            xf = _combine(dest, xn, yb, 256)
    return xf.reshape(batch, seq, d)
```

```python
import functools

import numpy as np
import jax
import jax.numpy as jnp
from jax import lax
from jax.experimental import pallas as pl
from jax.experimental.pallas import tpu as pltpu

BF16 = jnp.bfloat16
F32 = jnp.float32

MLA_HEADS = 8
QK_NOPE_DIM = 64
QK_ROPE_DIM = 32
V_HEAD_DIM = 64
QK_HEAD_DIM = QK_NOPE_DIM + QK_ROPE_DIM
ROPE_BASE = 10000.0
LRU_HEADS = 8
CONV_WIDTH = 4
LRU_C = 8.0
N_EXPERTS = 8
TOP_K = 2
RMS_EPS = 1e-6

LANE = 128
VMEM_LIMIT = 56 * 1024 * 1024


def _params(*sem):
    return pltpu.CompilerParams(dimension_semantics=sem, vmem_limit_bytes=VMEM_LIMIT)


def _rms(x, g):
    ms = jnp.mean(x * x, axis=-1, keepdims=True)
    return x * lax.rsqrt(ms + RMS_EPS) * g


def _dot(a, b):
    return jnp.dot(a, b, preferred_element_type=F32)


def _in_proj_body(x_ref, g_ref, w_ref, seg_ref, xl_ref, gl_ref, ga_ref, gb_ref, *, widths):
    h = _rms(x_ref[...], g_ref[...]).astype(BF16)
    o = np.cumsum((0,) + widths)
    seg_ref[...] = _dot(h, w_ref[:, o[0]:o[1]])
    xl_ref[...] = _dot(h, w_ref[:, o[1]:o[2]])
    gl_ref[...] = jax.nn.gelu(_dot(h, w_ref[:, o[2]:o[3]]), approximate=True).astype(BF16)
    ga_ref[...] = jax.nn.sigmoid(_dot(h, w_ref[:, o[3]:o[4]])).astype(BF16)
    gb_ref[...] = jax.nn.sigmoid(_dot(h, w_ref[:, o[4]:o[5]])).astype(BF16)


def _in_proj(x, g, w, widths, tm):
    n, d = x.shape
    dts = (F32, F32, BF16, BF16, BF16)
    row = lambda wd: pl.BlockSpec((tm, wd), lambda i: (i, 0))
    return pl.pallas_call(
        functools.partial(_in_proj_body, widths=widths),
        grid=(n // tm,),
        in_specs=[row(d), pl.BlockSpec((1, d), lambda i: (0, 0)),
                  pl.BlockSpec(w.shape, lambda i: (0, 0))],
        out_specs=[row(wd) for wd in widths],
        out_shape=[jax.ShapeDtypeStruct((n, wd), dt) for wd, dt in zip(widths, dts)],
        compiler_params=_params("parallel"),
        name="in_proj",
    )(x, g, w)


def _qkv_body(seg_ref, c_ref, s_ref, gq_ref, gkv_ref, wq_ref, wqr_ref, wk_ref, wkr_ref, wv_ref,
              gqm_ref, gqr_ref, gkm_ref, gkr_ref, q_ref, kt_ref, v_ref, *, q_rank, kv_rank):
    cqn = _rms(seg_ref[:, 0:q_rank], gq_ref[...]).astype(BF16)
    ckvn = _rms(seg_ref[:, q_rank:q_rank + kv_rank], gkv_ref[...])
    ck = jnp.concatenate([ckvn, seg_ref[:, q_rank + kv_rank:]], axis=-1).astype(BF16)
    cos = c_ref[...]
    sin = s_ref[...]
    scale = QK_HEAD_DIM ** -0.5 * np.log2(np.e)

    def finish(a_all, b_all, gm, gr, mul, store):
        cm = gm * cos * mul
        sm = gr * sin * mul
        for h in range(MLA_HEADS):
            a = a_all[:, h * LANE:(h + 1) * LANE]
            b = b_all[:, h * LANE:(h + 1) * LANE]
            inv = lax.rsqrt(jnp.sum(a * a, axis=-1, keepdims=True) * (1.0 / QK_HEAD_DIM) + RMS_EPS)
            store(h, inv * (a * cm + b * sm))

    def store_q(h, val):
        q_ref[:, h * LANE:(h + 1) * LANE] = val.astype(BF16)

    def store_kt(h, val):
        kt_ref[0, h * LANE:(h + 1) * LANE, :] = val.T.astype(BF16)

    finish(_dot(cqn, wq_ref[...]), _dot(cqn, wqr_ref[...]), gqm_ref[...], gqr_ref[...], scale, store_q)
    finish(_dot(ck, wk_ref[...]), _dot(ck, wkr_ref[...]), gkm_ref[...], gkr_ref[...], 1.0, store_kt)
    v_ref[...] = _dot(ck, wv_ref[...]).astype(BF16)


def _qkv(seg, cos, sin, gq, gkv, wq, wqr, wk, wkr, wv, gqm, gqr, gkm, gkr, tm, seq):
    n, sw = seg.shape
    q_rank = gq.shape[1]
    kv_rank = gkv.shape[1]
    hq = wq.shape[1]
    hv = wv.shape[1]
    per_seq = seq // tm
    row = lambda wd: pl.BlockSpec((tm, wd), lambda i: (i, 0))
    full = lambda a: pl.BlockSpec(a.shape, lambda i: (0, 0))
    consts = (gq, gkv, wq, wqr, wk, wkr, wv, gqm, gqr, gkm, gkr)
    return pl.pallas_call(
        functools.partial(_qkv_body, q_rank=q_rank, kv_rank=kv_rank),
        grid=(n // tm,),
        in_specs=[row(sw), row(LANE), row(LANE)] + [full(a) for a in consts],
        out_specs=[row(hq), pl.BlockSpec((1, hq, tm), lambda i: (i // per_seq, 0, i % per_seq)), row(hv)],
        out_shape=[jax.ShapeDtypeStruct((n, hq), BF16), jax.ShapeDtypeStruct((n // seq, hq, seq), BF16),
                   jax.ShapeDtypeStruct((n, hv), BF16)],
        compiler_params=_params("parallel"),
        name="qkv_prep",
    )(seg, cos, sin, *consts)


ATTN_ROW_CHUNK = 256


def _attn_body(q_ref, kt_ref, v_ref, hm_ref, o_ref, m_s, l_s, acc_s, *, tq, heads):
    i = pl.program_id(2)
    rc = ATTN_ROW_CHUNK
    n_c = tq // rc
    m_s[...] = jnp.full(m_s.shape, -jnp.inf, F32)
    l_s[...] = jnp.zeros(l_s.shape, F32)
    acc_s[...] = jnp.zeros(acc_s.shape, F32)
    lane = lax.broadcasted_iota(jnp.int32, (rc, LANE), 1)

    def per_head(vals):
        out = vals[heads - 1]
        for hh in range(heads - 2, -1, -1):
            out = jnp.where(lane < (hh + 1) * V_HEAD_DIM, vals[hh], out)
        return out

    def load_kv(start, ncols):
        kts = [kt_ref[0, hh * LANE:(hh + 1) * LANE, pl.ds(start, ncols)] for hh in range(heads)]
        vs = v_ref[0, pl.ds(start, ncols), :]
        parts = []
        for hh in range(heads):
            ind = jnp.broadcast_to(hm_ref[hh], vs.shape)
            parts.append(jnp.concatenate([vs * ind, ind], axis=1))
        return kts, jnp.concatenate(parts, axis=0)

    def chunk_update(c, kts, vcat, ncols, masked):
        rows = pl.ds(c * rc, rc)
        ps, alphas = [], []
        for hh in range(heads):
            s = _dot(q_ref[0, rows, hh * LANE:(hh + 1) * LANE], kts[hh])
            if masked:
                r_idx = lax.broadcasted_iota(jnp.int32, (rc, ncols), 0) + c * rc
                c_idx = lax.broadcasted_iota(jnp.int32, (rc, ncols), 1)
                s = jnp.where(c_idx <= r_idx, s, -jnp.inf)
            m_old = m_s[hh, rows, :]
            m_new = jnp.maximum(m_old, jnp.max(s, axis=-1, keepdims=True))
            ps.append(jnp.exp2(s - jnp.concatenate([m_new] * (ncols // LANE), axis=1)).astype(BF16))
            alphas.append(jnp.exp2(m_old - m_new))
            m_s[hh, rows, :] = m_new
        pv = _dot(jnp.concatenate(ps, axis=1), vcat)
        alpha = per_head(alphas)
        acc_s[rows, :] = alpha * acc_s[rows, :] + pv[:, :LANE]
        l_s[rows, :] = alpha * l_s[rows, :] + pv[:, LANE:]

    def kv_step(j, carry):
        kts, vcat = load_kv(pl.multiple_of(j * tq, tq), tq)
        for c in range(n_c):
            chunk_update(c, kts, vcat, tq, masked=False)
        return carry

    lax.fori_loop(0, i, kv_step, 0)
    start = pl.multiple_of(i * tq, tq)
    for c in range(n_c):
        ncols = (c + 1) * rc
        kts, vcat = load_kv(start, ncols)
        chunk_update(c, kts, vcat, ncols, masked=True)
    o_ref[0] = (acc_s[...] / l_s[...]).astype(BF16)


def _attention(q, kt, v, tq):
    b, s, hq = q.shape
    heads = LANE // V_HEAD_DIM
    n_hp = hq // (heads * LANE)
    head_mask = jnp.asarray(np.arange(LANE)[None, None, :] // V_HEAD_DIM == np.arange(heads)[:, None, None], BF16)
    return pl.pallas_call(
        functools.partial(_attn_body, tq=tq, heads=heads),
        grid=(b, n_hp, s // tq),
        in_specs=[pl.BlockSpec((1, tq, heads * LANE), lambda bi, hp, i: (bi, i, hp)),
                  pl.BlockSpec((1, heads * LANE, s), lambda bi, hp, i: (bi, hp, 0)),
                  pl.BlockSpec((1, s, LANE), lambda bi, hp, i: (bi, 0, hp)),
                  pl.BlockSpec((heads, 1, LANE), lambda bi, hp, i: (0, 0, 0))],
        out_specs=pl.BlockSpec((1, tq, LANE), lambda bi, hp, i: (bi, i, hp)),
        out_shape=jax.ShapeDtypeStruct((b, s, v.shape[2]), BF16),
        scratch_shapes=[pltpu.VMEM((heads, tq, LANE), F32), pltpu.VMEM((tq, LANE), F32),
                        pltpu.VMEM((tq, LANE), F32)],
        compiler_params=_params("parallel", "parallel", "arbitrary"),
        name="attention",
    )(q, kt, v, head_mask)


def _lru_body(x_ref, gl_ref, cw_ref, cb_ref, wg_ref, bg_ref, lam_ref, y_ref,
              ext_s, a_s, b_s, h_s, *, nb, t):
    c = x_ref.shape[2]
    pad = 8

    @pl.when(pl.program_id(0) == 0)
    def _():
        ext_s[:, 0:pad, :] = jnp.zeros((nb, pad, c), F32)
        h_s[...] = jnp.zeros_like(h_s)

    ext_s[:, pad:pad + t, :] = x_ref[...]
    xc = cb_ref[...][None]
    for kk in range(CONV_WIDTH):
        off = pad - (CONV_WIDTH - 1) + kk
        xc = xc + ext_s[:, off:off + t, :] * cw_ref[kk:kk + 1, :][None]
    ext_s[:, 0:pad, :] = ext_s[:, t:t + pad, :]

    xc2 = xc.reshape(nb * t, c)
    g = _dot(xc2.astype(BF16), wg_ref[...]) + bg_ref[...]
    r = jax.nn.sigmoid(g[:, :c])
    ig = jax.nn.sigmoid(g[:, c:])
    log_a = -LRU_C * r * jax.nn.softplus(-lam_ref[...])
    a = jnp.exp(log_a)
    mult = jnp.sqrt(-jnp.tanh(log_a) * (a * a + 1.0))
    bv = mult * ig * xc2
    n_lt = c // LANE
    for lt in range(n_lt):
        a_s[lt] = a[:, lt * LANE:(lt + 1) * LANE]
        b_s[lt] = bv[:, lt * LANE:(lt + 1) * LANE]

    def step(tt, hs):
        idx = pl.ds(tt, nb, stride=t)
        new = []
        for lt in range(n_lt):
            h = a_s[lt, idx, :] * hs[lt] + b_s[lt, idx, :]
            b_s[lt, idx, :] = h
            new.append(h)
        return tuple(new)

    hs = lax.fori_loop(0, t, step, tuple(h_s[lt] for lt in range(n_lt)), unroll=8)
    for lt in range(n_lt):
        h_s[lt] = hs[lt]
    hseq = jnp.concatenate([b_s[lt] for lt in range(n_lt)], axis=-1)
    y_ref[...] = (hseq.reshape(nb, t, c) * gl_ref[...].astype(F32)).astype(BF16)


def _lru(xl, gl, cw, cb, wg, bg, lam, t):
    nb, s, c = xl.shape
    blk = pl.BlockSpec((nb, t, c), lambda i: (0, i, 0))
    full = lambda a: pl.BlockSpec(a.shape, lambda i: (0, 0))
    return pl.pallas_call(
        functools.partial(_lru_body, nb=nb, t=t),
        grid=(s // t,),
        in_specs=[blk, blk, full(cw), full(cb), full(wg), full(bg), full(lam)],
        out_specs=blk,
        out_shape=jax.ShapeDtypeStruct((nb, s, c), BF16),
        scratch_shapes=[pltpu.VMEM((nb, t + 8, c), F32), pltpu.VMEM((c // LANE, nb * t, LANE), F32),
                        pltpu.VMEM((c // LANE, nb * t, LANE), F32), pltpu.VMEM((c // LANE, nb, LANE), F32)],
        compiler_params=_params("arbitrary"),
        name="rg_lru",
    )(xl, gl, cw, cb, wg, bg, lam)


def _merge_body(x_ref, o_ref, y_ref, ga_ref, gb_ref, wa_ref, wb_ref, wo_ref, g2_ref, *rest, router):
    ua = _dot(o_ref[...], wa_ref[...])
    ub = _dot(y_ref[...], wb_ref[...])
    merged = ga_ref[...].astype(F32) * ua + gb_ref[...].astype(F32) * ub
    xn = x_ref[...] + _dot(merged.astype(BF16), wo_ref[...])
    h2 = _rms(xn, g2_ref[...])
    if router:
        rhi_ref, rlo_ref, xn_ref, h2_ref, lg_ref = rest
        hi = h2.astype(BF16)
        lo = (h2 - hi.astype(F32)).astype(BF16)
        lg_ref[...] = _dot(hi, rhi_ref[...]) + (_dot(lo, rhi_ref[...]) + _dot(hi, rlo_ref[...]))
        h2_ref[...] = h2
    else:
        xn_ref, h2_ref = rest
        h2_ref[...] = h2.astype(BF16)
    xn_ref[...] = xn


def _merge(x, o, y, ga, gb, wa, wb, wo, g2, router, tm):
    n, d = x.shape
    row = lambda a: pl.BlockSpec((tm, a.shape[1]), lambda i: (i, 0))
    full = lambda a: pl.BlockSpec(a.shape, lambda i: (0, 0))
    ins = [x, o, y, ga, gb]
    consts = [wa, wb, wo, g2] + list(router or ())
    out_shape = [jax.ShapeDtypeStruct((n, d), F32), jax.ShapeDtypeStruct((n, d), F32 if router else BF16)]
    if router:
        out_shape.append(jax.ShapeDtypeStruct((n, LANE), F32))
    return pl.pallas_call(
        functools.partial(_merge_body, router=bool(router)),
        grid=(n // tm,),
        in_specs=[row(a) for a in ins] + [full(a) for a in consts],
        out_specs=[pl.BlockSpec((tm, sh.shape[1]), lambda i: (i, 0)) for sh in out_shape],
        out_shape=out_shape,
        compiler_params=_params("parallel"),
        name="merge_router" if router else "merge",
    )(*ins, *consts)


def _ffn_body(h_ref, x_ref, wg_ref, wu_ref, wd_ref, o_ref):
    h = h_ref[...]
    act = (jax.nn.silu(_dot(h, wg_ref[...])) * _dot(h, wu_ref[...])).astype(BF16)
    o_ref[...] = x_ref[...] + _dot(act, wd_ref[...])


def _resident(a):
    return pl.BlockSpec(a.shape, lambda *_: (0,) * a.ndim, pipeline_mode=pl.Buffered(1))


def _ffn(h, x, wg, wu, wd, tm):
    n, d = x.shape
    row = pl.BlockSpec((tm, d), lambda i: (i, 0))
    return pl.pallas_call(
        _ffn_body,
        grid=(n // tm,),
        in_specs=[row, row, _resident(wg), _resident(wu), _resident(wd)],
        out_specs=row,
        out_shape=jax.ShapeDtypeStruct((n, d), F32),
        compiler_params=_params("parallel"),
        name="dense_ffn",
    )(h, x, wg, wu, wd)


def _row_copy(src_hbm, dst, src_row, dst_row, sem):
    return pltpu.make_async_copy(src_hbm.at[pl.ds(src_row, 1), :], dst.at[pl.ds(dst_row, 1), :], sem)


MOE_PREFETCH_STEPS = 1


def _moe_ffn_body(te_ref, nv_ref, tok_ref, tokn_ref, w_ref, h_hbm, wg_ref, wu_ref, wd_ref, o_ref,
                  xbuf, xb16, acc_ref, sem, *, tm):
    i = pl.program_id(0)
    j = pl.program_id(1)
    last = pl.num_programs(1) - 1
    slot = i % 2
    part_rows = tm // MOE_PREFETCH_STEPS

    def wait_tile(s):
        pltpu.make_async_copy(h_hbm.at[pl.ds(0, tm), :], xbuf.at[s], sem.at[s]).wait()

    @pl.when((i == 0) & (j == 0))
    def _():
        def issue(r, carry):
            _row_copy(h_hbm, xbuf.at[0], tok_ref[r], r, sem.at[0]).start()
            return carry

        lax.fori_loop(0, tm, issue, 0, unroll=8)

    @pl.when(j == 0)
    def _():
        wait_tile(slot)
        xb16[...] = xbuf[slot].astype(BF16)

    def prefetch(part):
        for r in range(part * part_rows, (part + 1) * part_rows):
            _row_copy(h_hbm, xbuf.at[1 - slot], tokn_ref[r], r, sem.at[1 - slot]).start()

    def compute():
        h = xb16[...]
        act = (jax.nn.silu(_dot(h, wg_ref[0])) * _dot(h, wu_ref[0])).astype(BF16)
        part = _dot(act, wd_ref[0])

        @pl.when(j == 0)
        def _():
            acc_ref[...] = part

        @pl.when(j > 0)
        def _():
            acc_ref[...] += part

        @pl.when(j == last)
        def _():
            o_ref[...] = acc_ref[...] * w_ref[...]

    valid = i < nv_ref[0]
    for part in range(MOE_PREFETCH_STEPS):
        @pl.when(valid & (j == part))
        def _(part=part):
            prefetch(part)
            compute()

        @pl.when(jnp.logical_not(valid) & (j == part))
        def _(part=part):
            prefetch(part)

    @pl.when(valid & (j >= MOE_PREFETCH_STEPS))
    def _():
        compute()

    @pl.when(jnp.logical_not(valid) & (j == last))
    def _():
        o_ref[...] = jnp.zeros_like(o_ref)

    @pl.when((i == pl.num_programs(0) - 1) & (j == last))
    def _():
        wait_tile(1 - slot)


def _moe_ffn(tile_e, n_valid, tok, h, w_rows, wg, wu, wd, tm, tf):
    r = tok.shape[0]
    d = h.shape[1]
    f = wg.shape[2]
    n_tiles = r // tm
    assert f // tf > MOE_PREFETCH_STEPS and tm % MOE_PREFETCH_STEPS == 0
    grid_spec = pltpu.PrefetchScalarGridSpec(
        num_scalar_prefetch=2,
        grid=(n_tiles, f // tf),
        in_specs=[pl.BlockSpec((tm,), lambda i, j, te, nv: (i,), memory_space=pltpu.SMEM),
                  pl.BlockSpec((tm,), lambda i, j, te, nv: (jnp.minimum(i + 1, n_tiles - 1),),
                               memory_space=pltpu.SMEM),
                  pl.BlockSpec((tm, 1), lambda i, j, te, nv: (i, 0)),
                  pl.BlockSpec(memory_space=pl.ANY),
                  pl.BlockSpec((1, d, tf), lambda i, j, te, nv: (te[i], 0, j)),
                  pl.BlockSpec((1, d, tf), lambda i, j, te, nv: (te[i], 0, j)),
                  pl.BlockSpec((1, tf, d), lambda i, j, te, nv: (te[i], j, 0))],
        out_specs=pl.BlockSpec((tm, d), lambda i, j, te, nv: (i, 0)),
        scratch_shapes=[pltpu.VMEM((2, tm, d), F32), pltpu.VMEM((tm, d), BF16), pltpu.VMEM((tm, d), F32),
                        pltpu.SemaphoreType.DMA((2,))],
    )
    return pl.pallas_call(
        functools.partial(_moe_ffn_body, tm=tm),
        grid_spec=grid_spec,
        out_shape=jax.ShapeDtypeStruct((r, d), F32),
        compiler_params=_params("arbitrary", "arbitrary"),
        name="moe_ffn",
    )(tile_e, n_valid, tok, tok, w_rows, h, wg, wu, wd)


def _combine_body(d_ref, x_ref, yb_hbm, o_ref, buf, sem, *, tm):
    def issue(r, carry):
        for kk in range(TOP_K):
            _row_copy(yb_hbm, buf.at[kk], d_ref[TOP_K * r + kk], r, sem).start()
        return carry

    lax.fori_loop(0, tm, issue, 0, unroll=4)
    for kk in range(TOP_K):
        pltpu.make_async_copy(yb_hbm.at[pl.ds(0, tm), :], buf.at[kk], sem).wait()
    acc = x_ref[...]
    for kk in range(TOP_K):
        acc = acc + buf[kk]
    o_ref[...] = acc


def _combine(dest, x, yb, tm):
    n, d = x.shape
    return pl.pallas_call(
        functools.partial(_combine_body, tm=tm),
        grid=(n // tm,),
        in_specs=[pl.BlockSpec((TOP_K * tm,), lambda i: (i,), memory_space=pltpu.SMEM),
                  pl.BlockSpec((tm, d), lambda i: (i, 0)),
                  pl.BlockSpec(memory_space=pl.ANY)],
        out_specs=pl.BlockSpec((tm, d), lambda i: (i, 0)),
        out_shape=jax.ShapeDtypeStruct((n, d), F32),
        scratch_shapes=[pltpu.VMEM((TOP_K, tm, d), F32), pltpu.SemaphoreType.DMA(())],
        compiler_params=_params("arbitrary"),
        name="moe_combine",
    )(dest, x, yb)


def _route(logits, tm):
    n = logits.shape[0]
    top_logit, top_e = lax.top_k(logits[:, :N_EXPERTS], TOP_K)
    top_w = jax.nn.softmax(top_logit, axis=-1)
    flat_e = top_e.reshape(-1).astype(jnp.int32)
    onehot = (flat_e[:, None] == jnp.arange(N_EXPERTS, dtype=jnp.int32)[None, :]).astype(jnp.int32)
    csum = jnp.cumsum(onehot, axis=0)
    rank = jnp.sum((csum - onehot) * onehot, axis=1)
    counts = csum[-1]
    padded = (counts + tm - 1) // tm * tm
    pend = jnp.cumsum(padded)
    pstart = pend - padded
    dest = (pstart[flat_e] + rank).astype(jnp.int32)
    n_rows = -(-(n * TOP_K + N_EXPERTS * (tm - 1)) // tm) * tm
    flat_tok = jnp.repeat(jnp.arange(n, dtype=jnp.int32), TOP_K)
    packed = jnp.stack([flat_tok, lax.bitcast_convert_type(top_w.reshape(-1), jnp.int32)], axis=1)
    buf = jnp.zeros((n_rows, 2), jnp.int32).at[dest].set(packed, unique_indices=True, mode="promise_in_bounds")
    tok_buf = buf[:, 0]
    w_buf = lax.bitcast_convert_type(buf[:, 1], F32)
    tile_start = jnp.arange(n_rows // tm, dtype=jnp.int32) * tm
    tile_e = jnp.minimum(jnp.sum(tile_start[:, None] >= pend[None, :], axis=1), N_EXPERTS - 1)
    n_valid = (pend[-1] // tm).astype(jnp.int32).reshape(1)
    return tok_buf, w_buf.reshape(-1, 1), dest, tile_e.astype(jnp.int32), n_valid


def _head_slots(w, heads, width):
    k = w.shape[0]
    return jnp.pad(w.reshape(k, heads, width), ((0, 0), (0, 0), (0, LANE - width)))


def _rotate_half_slots(w3):
    half = QK_ROPE_DIM // 2
    lo, mid, hi = QK_NOPE_DIM, QK_NOPE_DIM + half, QK_HEAD_DIM
    out = jnp.zeros_like(w3)
    out = out.at[:, :, lo:mid].set(-w3[:, :, mid:hi])
    out = out.at[:, :, mid:hi].set(w3[:, :, lo:mid])
    return out


def _rope_gains(g):
    half = QK_ROPE_DIM // 2
    lo, mid, hi = QK_NOPE_DIM, QK_NOPE_DIM + half, QK_HEAD_DIM
    main = jnp.pad(g, (0, LANE - QK_HEAD_DIM))
    rot = jnp.zeros((LANE,), F32).at[lo:mid].set(g[mid:hi]).at[mid:hi].set(g[lo:mid])
    return main.reshape(1, LANE), rot.reshape(1, LANE)


def _rope_tables(positions):
    half = QK_ROPE_DIM // 2
    inv_freq = ROPE_BASE ** (-jnp.arange(0, QK_ROPE_DIM, 2, dtype=F32) / QK_ROPE_DIM)
    ang = positions.astype(F32).reshape(-1, 1) * inv_freq[None, :]
    n = ang.shape[0]
    pad = jnp.zeros((n, LANE - QK_HEAD_DIM), F32)
    cos = jnp.concatenate([jnp.ones((n, QK_NOPE_DIM), F32), jnp.cos(ang), jnp.cos(ang), pad], axis=-1)
    sin = jnp.concatenate([jnp.zeros((n, QK_NOPE_DIM), F32), jnp.sin(ang), jnp.sin(ang), pad], axis=-1)
    del half
    return cos, sin


def _block_diag(w):
    h, a, b = w.shape
    eye = jnp.eye(h, dtype=w.dtype)
    return (eye[:, None, :, None] * w[:, :, None, :]).reshape(h * a, h * b)


def kernel(x, positions, norm1_g, w_in, q_norm_g, w_uq, kv_norm_g, w_ukv, qk_q_g, qk_k_g, w_up_attn, conv_w, conv_b, w_rg, b_rg, w_ig, b_ig, lru_lambda, w_up_lru, w_o, norm2_g, ffn_w_gate, ffn_w_up, ffn_w_down, moe_router, moe_w_gate, moe_w_up, moe_w_down):
    batch, seq, d = x.shape
    n = batch * seq
    depth = norm1_g.shape[0]
    q_rank = q_norm_g.shape[1]
    kv_rank = kv_norm_g.shape[1]
    c_lru = conv_w.shape[2]
    head_w = d - 0
    lat = q_rank + kv_rank + QK_ROPE_DIM
    seg_w = -(-lat // LANE) * LANE
    widths = (seg_w, c_lru, c_lru, head_w, head_w)

    tm = 512
    tq = 1024
    t_lru = 128
    tm_moe = 512
    tf_moe = moe_w_gate.shape[3] // 2

    cos, sin = _rope_tables(positions)
    xf = x.reshape(n, d)
    for layer in range(depth):
        w_in_l = w_in[layer]
        w_in_p = jnp.concatenate(
            [w_in_l[:, :lat], jnp.zeros((d, seg_w - lat), F32), w_in_l[:, lat:]], axis=1).astype(BF16)
        seg, xl, gl, ga, gb = _in_proj(xf, norm1_g[layer].reshape(1, d), w_in_p, widths, tm)

        wq3 = _head_slots(w_uq[layer], MLA_HEADS, QK_HEAD_DIM)
        wkv3 = w_ukv[layer].reshape(kv_rank, MLA_HEADS, QK_NOPE_DIM + V_HEAD_DIM)
        ck_w = seg_w - q_rank
        wk3 = jnp.zeros((ck_w, MLA_HEADS, LANE), F32)
        wk3 = wk3.at[:kv_rank, :, :QK_NOPE_DIM].set(wkv3[:, :, :QK_NOPE_DIM])
        eye = jnp.eye(QK_ROPE_DIM, dtype=F32)
        wk3 = wk3.at[kv_rank:kv_rank + QK_ROPE_DIM, :, QK_NOPE_DIM:QK_HEAD_DIM].set(
            jnp.broadcast_to(eye[:, None, :], (QK_ROPE_DIM, MLA_HEADS, QK_ROPE_DIM)))
        wv = jnp.zeros((ck_w, MLA_HEADS * V_HEAD_DIM), F32).at[:kv_rank].set(
            wkv3[:, :, QK_NOPE_DIM:].reshape(kv_rank, MLA_HEADS * V_HEAD_DIM))
        flat = lambda w3: w3.reshape(w3.shape[0], MLA_HEADS * LANE).astype(BF16)
        gqm, gqr = _rope_gains(qk_q_g[layer])
        gkm, gkr = _rope_gains(qk_k_g[layer])
        q, kt, v = _qkv(seg, cos, sin, q_norm_g[layer].reshape(1, -1), kv_norm_g[layer].reshape(1, -1),
                        flat(wq3), flat(_rotate_half_slots(wq3)), flat(wk3), flat(_rotate_half_slots(wk3)),
                        wv.astype(BF16), gqm, gqr, gkm, gkr, tm, seq)
        o = _attention(q.reshape(batch, seq, -1), kt, v.reshape(batch, seq, -1), tq)

        wgate = jnp.concatenate([_block_diag(w_rg[layer]), _block_diag(w_ig[layer])], axis=1).astype(BF16)
        bgate = jnp.concatenate([b_rg[layer], b_ig[layer]]).reshape(1, -1)
        y = _lru(xl.reshape(batch, seq, c_lru), gl.reshape(batch, seq, c_lru), conv_w[layer],
                 conv_b[layer].reshape(1, -1), wgate, bgate, lru_lambda[layer].reshape(1, -1), t_lru)

        j = layer // 2
        is_moe = layer % 2 == 1
        router = None
        if is_moe:
            r_pad = jnp.pad(moe_router[j], ((0, 0), (0, LANE - N_EXPERTS)))
            r_hi = r_pad.astype(BF16)
            router = (r_hi, (r_pad - r_hi.astype(F32)).astype(BF16))
        outs = _merge(xf, o.reshape(n, -1), y.reshape(n, -1), ga, gb, w_up_attn[layer].astype(BF16),
                      w_up_lru[layer].astype(BF16), w_o[layer].astype(BF16), norm2_g[layer].reshape(1, d),
                      router, tm)
        if not is_moe:
            xn, h2 = outs
            xf = _ffn(h2, xn, ffn_w_gate[j].astype(BF16), ffn_w_up[j].astype(BF16),
                      ffn_w_down[j].astype(BF16), tm)
        else:
            xn, h2, logits = outs
            tok_buf, w_rows, dest, tile_e, n_valid = _route(logits, tm_moe)
            yb = _moe_ffn(tile_e, n_valid, tok_buf, h2, w_rows, moe_w_gate[j].astype(BF16), moe_w_up[j].astype(BF16),
                          moe_w_down[j].astype(BF16), tm_moe, tf_moe)
            xf = _combine(dest, xn, yb, 256)
    return xf.reshape(batch, seq, d)
```

```python
import functools

import numpy as np
import jax
import jax.numpy as jnp
from jax import lax
from jax.experimental import pallas as pl
from jax.experimental.pallas import tpu as pltpu

BF16 = jnp.bfloat16
F32 = jnp.float32

MLA_HEADS = 8
QK_NOPE_DIM = 64
QK_ROPE_DIM = 32
V_HEAD_DIM = 64
QK_HEAD_DIM = QK_NOPE_DIM + QK_ROPE_DIM
ROPE_BASE = 10000.0
LRU_HEADS = 8
CONV_WIDTH = 4
LRU_C = 8.0
N_EXPERTS = 8
TOP_K = 2
RMS_EPS = 1e-6

LANE = 128
VMEM_LIMIT = 56 * 1024 * 1024


def _params(*sem):
    return pltpu.CompilerParams(dimension_semantics=sem, vmem_limit_bytes=VMEM_LIMIT)


def _rms(x, g):
    ms = jnp.mean(x * x, axis=-1, keepdims=True)
    return x * lax.rsqrt(ms + RMS_EPS) * g


def _dot(a, b):
    return jnp.dot(a, b, preferred_element_type=F32)


def _in_proj_body(x_ref, g_ref, w_ref, seg_ref, xl_ref, gl_ref, ga_ref, gb_ref, *, widths):
    h = _rms(x_ref[...], g_ref[...]).astype(BF16)
    o = np.cumsum((0,) + widths)
    seg_ref[...] = _dot(h, w_ref[:, o[0]:o[1]])
    xl_ref[...] = _dot(h, w_ref[:, o[1]:o[2]])
    gl_ref[...] = jax.nn.gelu(_dot(h, w_ref[:, o[2]:o[3]]), approximate=True).astype(BF16)
    ga_ref[...] = jax.nn.sigmoid(_dot(h, w_ref[:, o[3]:o[4]])).astype(BF16)
    gb_ref[...] = jax.nn.sigmoid(_dot(h, w_ref[:, o[4]:o[5]])).astype(BF16)


def _in_proj(x, g, w, widths, tm):
    n, d = x.shape
    dts = (F32, F32, BF16, BF16, BF16)
    row = lambda wd: pl.BlockSpec((tm, wd), lambda i: (i, 0))
    return pl.pallas_call(
        functools.partial(_in_proj_body, widths=widths),
        grid=(n // tm,),
        in_specs=[row(d), pl.BlockSpec((1, d), lambda i: (0, 0)),
                  pl.BlockSpec(w.shape, lambda i: (0, 0))],
        out_specs=[row(wd) for wd in widths],
        out_shape=[jax.ShapeDtypeStruct((n, wd), dt) for wd, dt in zip(widths, dts)],
        compiler_params=_params("parallel"),
        name="in_proj",
    )(x, g, w)


def _qkv_body(seg_ref, c_ref, s_ref, gq_ref, gkv_ref, wq_ref, wqr_ref, wk_ref, wkr_ref, wv_ref,
              gqm_ref, gqr_ref, gkm_ref, gkr_ref, q_ref, kt_ref, v_ref, *, q_rank, kv_rank):
    cqn = _rms(seg_ref[:, 0:q_rank], gq_ref[...]).astype(BF16)
    ckvn = _rms(seg_ref[:, q_rank:q_rank + kv_rank], gkv_ref[...])
    ck = jnp.concatenate([ckvn, seg_ref[:, q_rank + kv_rank:]], axis=-1).astype(BF16)
    cos = c_ref[...]
    sin = s_ref[...]
    scale = QK_HEAD_DIM ** -0.5 * np.log2(np.e)

    def finish(a_all, b_all, gm, gr, mul, store):
        cm = gm * cos * mul
        sm = gr * sin * mul
        for h in range(MLA_HEADS):
            a = a_all[:, h * LANE:(h + 1) * LANE]
            b = b_all[:, h * LANE:(h + 1) * LANE]
            inv = lax.rsqrt(jnp.sum(a * a, axis=-1, keepdims=True) * (1.0 / QK_HEAD_DIM) + RMS_EPS)
            store(h, inv * (a * cm + b * sm))

    def store_q(h, val):
        q_ref[:, h * LANE:(h + 1) * LANE] = val.astype(BF16)

    def store_kt(h, val):
        kt_ref[0, h * LANE:(h + 1) * LANE, :] = val.T.astype(BF16)

    finish(_dot(cqn, wq_ref[...]), _dot(cqn, wqr_ref[...]), gqm_ref[...], gqr_ref[...], scale, store_q)
    finish(_dot(ck, wk_ref[...]), _dot(ck, wkr_ref[...]), gkm_ref[...], gkr_ref[...], 1.0, store_kt)
    v_ref[...] = _dot(ck, wv_ref[...]).astype(BF16)


def _qkv(seg, cos, sin, gq, gkv, wq, wqr, wk, wkr, wv, gqm, gqr, gkm, gkr, tm, seq):
    n, sw = seg.shape
    q_rank = gq.shape[1]
    kv_rank = gkv.shape[1]
    hq = wq.shape[1]
    hv = wv.shape[1]
    per_seq = seq // tm
    row = lambda wd: pl.BlockSpec((tm, wd), lambda i: (i, 0))
    full = lambda a: pl.BlockSpec(a.shape, lambda i: (0, 0))
    consts = (gq, gkv, wq, wqr, wk, wkr, wv, gqm, gqr, gkm, gkr)
    return pl.pallas_call(
        functools.partial(_qkv_body, q_rank=q_rank, kv_rank=kv_rank),
        grid=(n // tm,),
        in_specs=[row(sw), row(LANE), row(LANE)] + [full(a) for a in consts],
        out_specs=[row(hq), pl.BlockSpec((1, hq, tm), lambda i: (i // per_seq, 0, i % per_seq)), row(hv)],
        out_shape=[jax.ShapeDtypeStruct((n, hq), BF16), jax.ShapeDtypeStruct((n // seq, hq, seq), BF16),
                   jax.ShapeDtypeStruct((n, hv), BF16)],
        compiler_params=_params("parallel"),
        name="qkv_prep",
    )(seg, cos, sin, *consts)


ATTN_ROW_CHUNK = 256


def _attn_body(q_ref, kt_ref, v_ref, hm_ref, o_ref, m_s, l_s, acc_s, *, tq, heads):
    i = pl.program_id(2)
    rc = ATTN_ROW_CHUNK
    n_c = tq // rc
    m_s[...] = jnp.full(m_s.shape, -jnp.inf, F32)
    l_s[...] = jnp.zeros(l_s.shape, F32)
    acc_s[...] = jnp.zeros(acc_s.shape, F32)
    lane = lax.broadcasted_iota(jnp.int32, (rc, LANE), 1)

    def per_head(vals):
        out = vals[heads - 1]
        for hh in range(heads - 2, -1, -1):
            out = jnp.where(lane < (hh + 1) * V_HEAD_DIM, vals[hh], out)
        return out

    def load_kv(start, ncols):
        kts = [kt_ref[0, hh * LANE:(hh + 1) * LANE, pl.ds(start, ncols)] for hh in range(heads)]
        vs = v_ref[0, pl.ds(start, ncols), :]
        parts = []
        for hh in range(heads):
            ind = jnp.broadcast_to(hm_ref[hh], vs.shape)
            parts.append(jnp.concatenate([vs * ind, ind], axis=1))
        return kts, jnp.concatenate(parts, axis=0)

    def chunk_update(c, kts, vcat, ncols, masked):
        rows = pl.ds(c * rc, rc)
        ps, alphas = [], []
        for hh in range(heads):
            s = _dot(q_ref[0, rows, hh * LANE:(hh + 1) * LANE], kts[hh])
            if masked:
                r_idx = lax.broadcasted_iota(jnp.int32, (rc, ncols), 0) + c * rc
                c_idx = lax.broadcasted_iota(jnp.int32, (rc, ncols), 1)
                s = jnp.where(c_idx <= r_idx, s, -jnp.inf)
            m_old = m_s[hh, rows, :]
            m_new = jnp.maximum(m_old, jnp.max(s, axis=-1, keepdims=True))
            ps.append(jnp.exp2(s - jnp.concatenate([m_new] * (ncols // LANE), axis=1)).astype(BF16))
            alphas.append(jnp.exp2(m_old - m_new))
            m_s[hh, rows, :] = m_new
        pv = _dot(jnp.concatenate(ps, axis=1), vcat)
        alpha = per_head(alphas)
        acc_s[rows, :] = alpha * acc_s[rows, :] + pv[:, :LANE]
        l_s[rows, :] = alpha * l_s[rows, :] + pv[:, LANE:]

    def kv_step(j, carry):
        kts, vcat = load_kv(pl.multiple_of(j * tq, tq), tq)
        for c in range(n_c):
            chunk_update(c, kts, vcat, tq, masked=False)
        return carry

    lax.fori_loop(0, i, kv_step, 0)
    start = pl.multiple_of(i * tq, tq)
    for c in range(n_c):
        ncols = (c + 1) * rc
        kts, vcat = load_kv(start, ncols)
        chunk_update(c, kts, vcat, ncols, masked=True)
    o_ref[0] = (acc_s[...] / l_s[...]).astype(BF16)


def _attention(q, kt, v, tq):
    b, s, hq = q.shape
    heads = LANE // V_HEAD_DIM
    n_hp = hq // (heads * LANE)
    head_mask = jnp.asarray(np.arange(LANE)[None, None, :] // V_HEAD_DIM == np.arange(heads)[:, None, None], BF16)
    return pl.pallas_call(
        functools.partial(_attn_body, tq=tq, heads=heads),
        grid=(b, n_hp, s // tq),
        in_specs=[pl.BlockSpec((1, tq, heads * LANE), lambda bi, hp, i: (bi, i, hp)),
                  pl.BlockSpec((1, heads * LANE, s), lambda bi, hp, i: (bi, hp, 0)),
                  pl.BlockSpec((1, s, LANE), lambda bi, hp, i: (bi, 0, hp)),
                  pl.BlockSpec((heads, 1, LANE), lambda bi, hp, i: (0, 0, 0))],
        out_specs=pl.BlockSpec((1, tq, LANE), lambda bi, hp, i: (bi, i, hp)),
        out_shape=jax.ShapeDtypeStruct((b, s, v.shape[2]), BF16),
        scratch_shapes=[pltpu.VMEM((heads, tq, LANE), F32), pltpu.VMEM((tq, LANE), F32),
                        pltpu.VMEM((tq, LANE), F32)],
        compiler_params=_params("parallel", "parallel", "arbitrary"),
        name="attention",
    )(q, kt, v, head_mask)


def _lru_body(x_ref, gl_ref, cw_ref, cb_ref, wg_ref, bg_ref, lam_ref, y_ref,
              ext_s, a_s, b_s, h_s, *, nb, t):
    c = x_ref.shape[2]
    pad = 8

    @pl.when(pl.program_id(0) == 0)
    def _():
        ext_s[:, 0:pad, :] = jnp.zeros((nb, pad, c), F32)
        h_s[...] = jnp.zeros_like(h_s)

    ext_s[:, pad:pad + t, :] = x_ref[...]
    xc = cb_ref[...][None]
    for kk in range(CONV_WIDTH):
        off = pad - (CONV_WIDTH - 1) + kk
        xc = xc + ext_s[:, off:off + t, :] * cw_ref[kk:kk + 1, :][None]
    ext_s[:, 0:pad, :] = ext_s[:, t:t + pad, :]

    xc2 = xc.reshape(nb * t, c)
    g = _dot(xc2.astype(BF16), wg_ref[...]) + bg_ref[...]
    r = jax.nn.sigmoid(g[:, :c])
    ig = jax.nn.sigmoid(g[:, c:])
    log_a = -LRU_C * r * jax.nn.softplus(-lam_ref[...])
    a = jnp.exp(log_a)
    mult = jnp.sqrt(-jnp.tanh(log_a) * (a * a + 1.0))
    bv = mult * ig * xc2
    n_lt = c // LANE
    for lt in range(n_lt):
        a_s[lt] = a[:, lt * LANE:(lt + 1) * LANE]
        b_s[lt] = bv[:, lt * LANE:(lt + 1) * LANE]

    def step(tt, hs):
        idx = pl.ds(tt, nb, stride=t)
        new = []
        for lt in range(n_lt):
            h = a_s[lt, idx, :] * hs[lt] + b_s[lt, idx, :]
            b_s[lt, idx, :] = h
            new.append(h)
        return tuple(new)

    hs = lax.fori_loop(0, t, step, tuple(h_s[lt] for lt in range(n_lt)), unroll=8)
    for lt in range(n_lt):
        h_s[lt] = hs[lt]
    hseq = jnp.concatenate([b_s[lt] for lt in range(n_lt)], axis=-1)
    y_ref[...] = (hseq.reshape(nb, t, c) * gl_ref[...].astype(F32)).astype(BF16)


def _lru(xl, gl, cw, cb, wg, bg, lam, t):
    nb, s, c = xl.shape
    blk = pl.BlockSpec((nb, t, c), lambda i: (0, i, 0))
    full = lambda a: pl.BlockSpec(a.shape, lambda i: (0, 0))
    return pl.pallas_call(
        functools.partial(_lru_body, nb=nb, t=t),
        grid=(s // t,),
        in_specs=[blk, blk, full(cw), full(cb), full(wg), full(bg), full(lam)],
        out_specs=blk,
        out_shape=jax.ShapeDtypeStruct((nb, s, c), BF16),
        scratch_shapes=[pltpu.VMEM((nb, t + 8, c), F32), pltpu.VMEM((c // LANE, nb * t, LANE), F32),
                        pltpu.VMEM((c // LANE, nb * t, LANE), F32), pltpu.VMEM((c // LANE, nb, LANE), F32)],
        compiler_params=_params("arbitrary"),
        name="rg_lru",
    )(xl, gl, cw, cb, wg, bg, lam)


def _merge_body(x_ref, o_ref, y_ref, ga_ref, gb_ref, wa_ref, wb_ref, wo_ref, g2_ref, *rest, router):
    ua = _dot(o_ref[...], wa_ref[...])
    ub = _dot(y_ref[...], wb_ref[...])
    merged = ga_ref[...].astype(F32) * ua + gb_ref[...].astype(F32) * ub
    xn = x_ref[...] + _dot(merged.astype(BF16), wo_ref[...])
    h2 = _rms(xn, g2_ref[...])
    if router:
        rhi_ref, rlo_ref, xn_ref, h2_ref, lg_ref = rest
        hi = h2.astype(BF16)
        lo = (h2 - hi.astype(F32)).astype(BF16)
        lg_ref[...] = _dot(hi, rhi_ref[...]) + (_dot(lo, rhi_ref[...]) + _dot(hi, rlo_ref[...]))
        h2_ref[...] = h2
    else:
        xn_ref, h2_ref = rest
        h2_ref[...] = h2.astype(BF16)
    xn_ref[...] = xn


def _merge(x, o, y, ga, gb, wa, wb, wo, g2, router, tm):
    n, d = x.shape
    row = lambda a: pl.BlockSpec((tm, a.shape[1]), lambda i: (i, 0))
    full = lambda a: pl.BlockSpec(a.shape, lambda i: (0, 0))
    ins = [x, o, y, ga, gb]
    consts = [wa, wb, wo, g2] + list(router or ())
    out_shape = [jax.ShapeDtypeStruct((n, d), F32), jax.ShapeDtypeStruct((n, d), F32 if router else BF16)]
    if router:
        out_shape.append(jax.ShapeDtypeStruct((n, LANE), F32))
    return pl.pallas_call(
        functools.partial(_merge_body, router=bool(router)),
        grid=(n // tm,),
        in_specs=[row(a) for a in ins] + [full(a) for a in consts],
        out_specs=[pl.BlockSpec((tm, sh.shape[1]), lambda i: (i, 0)) for sh in out_shape],
        out_shape=out_shape,
        compiler_params=_params("parallel"),
        name="merge_router" if router else "merge",
    )(*ins, *consts)


def _ffn_body(h_ref, x_ref, wg_ref, wu_ref, wd_ref, o_ref):
    h = h_ref[...]
    act = (jax.nn.silu(_dot(h, wg_ref[...])) * _dot(h, wu_ref[...])).astype(BF16)
    o_ref[...] = x_ref[...] + _dot(act, wd_ref[...])


def _resident(a):
    return pl.BlockSpec(a.shape, lambda *_: (0,) * a.ndim, pipeline_mode=pl.Buffered(1))


def _ffn(h, x, wg, wu, wd, tm):
    n, d = x.shape
    row = pl.BlockSpec((tm, d), lambda i: (i, 0))
    return pl.pallas_call(
        _ffn_body,
        grid=(n // tm,),
        in_specs=[row, row, _resident(wg), _resident(wu), _resident(wd)],
        out_specs=row,
        out_shape=jax.ShapeDtypeStruct((n, d), F32),
        compiler_params=_params("parallel"),
        name="dense_ffn",
    )(h, x, wg, wu, wd)


def _row_copy(src_hbm, dst, src_row, dst_row, sem):
    return pltpu.make_async_copy(src_hbm.at[pl.ds(src_row, 1), :], dst.at[pl.ds(dst_row, 1), :], sem)


MOE_FF_STEPS = 2


def _moe_ffn_body(te_ref, nv_ref, tok_ref, tokn_ref, sidx_ref, w_ref, h_hbm, wg_ref, wu_ref, wd_ref, y_hbm,
                  xbuf, xb16, acc_ref, obuf, gsem, ssem, *, tm):
    i = pl.program_id(0)
    j = pl.program_id(1)
    n_i = pl.num_programs(0)
    slot = i % 2

    def wait_gather(s):
        pltpu.make_async_copy(h_hbm.at[pl.ds(0, tm), :], xbuf.at[s], gsem.at[s]).wait()

    def wait_scatter(s):
        pltpu.make_async_copy(obuf.at[s], y_hbm.at[pl.ds(0, tm), :], ssem.at[s]).wait()

    @pl.when((i == 0) & (j == 0))
    def _():
        obuf[...] = jnp.zeros_like(obuf)

        def issue(r, carry):
            _row_copy(h_hbm, xbuf.at[0], tok_ref[r], r, gsem.at[0]).start()
            return carry

        lax.fori_loop(0, tm, issue, 0, unroll=8)

    @pl.when(j == 0)
    def _():
        wait_gather(slot)
        xb16[...] = xbuf[slot].astype(BF16)

    def gather_next():
        for r in range(tm):
            _row_copy(h_hbm, xbuf.at[1 - slot], tokn_ref[r], r, gsem.at[1 - slot]).start()

    def scatter_prev():
        for r in range(tm):
            _row_copy(obuf.at[1 - slot], y_hbm, r, sidx_ref[r], ssem.at[1 - slot]).start()

    def compute(first):
        h = xb16[...]
        act = (jax.nn.silu(_dot(h, wg_ref[0])) * _dot(h, wu_ref[0])).astype(BF16)
        part = _dot(act, wd_ref[0])
        if first:
            acc_ref[...] = part
        else:
            obuf[slot] = (acc_ref[...] + part) * w_ref[...]

    valid = i < nv_ref[0]

    @pl.when(valid & (j == 0))
    def _():
        gather_next()
        compute(first=True)

    @pl.when(jnp.logical_not(valid) & (j == 0))
    def _():
        gather_next()

    @pl.when((j == 1) & (i >= 1))
    def _():
        wait_scatter(slot)

    @pl.when(valid & (j == 1))
    def _():
        scatter_prev()
        compute(first=False)

    @pl.when(jnp.logical_not(valid) & (j == 1) & (i < n_i - 1))
    def _():
        scatter_prev()

    @pl.when((i == n_i - 1) & (j == 1))
    def _():
        scatter_prev()
        wait_scatter(1 - slot)
        wait_gather(1 - slot)


def _moe_ffn(tile_e, n_valid, tok, sidx, h, w_rows, wg, wu, wd, y_rows, tm):
    d = h.shape[1]
    f = wg.shape[2]
    tf = f // MOE_FF_STEPS
    n_grid = tok.shape[0] // tm
    nxt = lambda i, j, te, nv: (jnp.minimum(i + 1, n_grid - 1),)
    grid_spec = pltpu.PrefetchScalarGridSpec(
        num_scalar_prefetch=2,
        grid=(n_grid, MOE_FF_STEPS),
        in_specs=[pl.BlockSpec((tm,), lambda i, j, te, nv: (i,), memory_space=pltpu.SMEM),
                  pl.BlockSpec((tm,), nxt, memory_space=pltpu.SMEM),
                  pl.BlockSpec((tm,), lambda i, j, te, nv: (i,), memory_space=pltpu.SMEM),
                  pl.BlockSpec((tm, 1), lambda i, j, te, nv: (i, 0)),
                  pl.BlockSpec(memory_space=pl.ANY),
                  pl.BlockSpec((1, d, tf), lambda i, j, te, nv: (te[i], 0, j)),
                  pl.BlockSpec((1, d, tf), lambda i, j, te, nv: (te[i], 0, j)),
                  pl.BlockSpec((1, tf, d), lambda i, j, te, nv: (te[i], j, 0))],
        out_specs=pl.BlockSpec(memory_space=pl.ANY),
        scratch_shapes=[pltpu.VMEM((2, tm, d), F32), pltpu.VMEM((tm, d), BF16), pltpu.VMEM((tm, d), F32),
                        pltpu.VMEM((2, tm, d), F32), pltpu.SemaphoreType.DMA((2,)),
                        pltpu.SemaphoreType.DMA((2,))],
    )
    return pl.pallas_call(
        functools.partial(_moe_ffn_body, tm=tm),
        grid_spec=grid_spec,
        out_shape=jax.ShapeDtypeStruct((y_rows, d), F32),
        compiler_params=_params("arbitrary", "arbitrary"),
        name="moe_ffn",
    )(tile_e, n_valid, tok, tok, sidx, w_rows, h, wg, wu, wd)


def _moe_add_body(x_ref, ya_ref, yb_ref, o_ref):
    o_ref[...] = x_ref[...] + ya_ref[...] + yb_ref[...]


def _moe_add(x, y2, tm):
    n, d = x.shape
    return pl.pallas_call(
        _moe_add_body,
        grid=(n // tm,),
        in_specs=[pl.BlockSpec((tm, d), lambda i: (i, 0)),
                  pl.BlockSpec((tm, d), lambda i: (i, 0)),
                  pl.BlockSpec((tm, d), lambda i: (i + n // tm, 0))],
        out_specs=pl.BlockSpec((tm, d), lambda i: (i, 0)),
        out_shape=jax.ShapeDtypeStruct((n, d), F32),
        compiler_params=_params("parallel"),
        name="moe_add",
    )(x, y2, y2)


def _route(logits, tm):
    n = logits.shape[0]
    top_logit, top_e = lax.top_k(logits[:, :N_EXPERTS], TOP_K)
    top_w = jax.nn.softmax(top_logit, axis=-1)
    flat_e = top_e.reshape(-1).astype(jnp.int32)
    onehot = (flat_e[:, None] == jnp.arange(N_EXPERTS, dtype=jnp.int32)[None, :]).astype(jnp.int32)
    csum = jnp.cumsum(onehot, axis=0)
    rank = jnp.sum((csum - onehot) * onehot, axis=1)
    counts = csum[-1]
    padded = (counts + tm - 1) // tm * tm
    pend = jnp.cumsum(padded)
    pstart = pend - padded
    dest = (pstart[flat_e] + rank).astype(jnp.int32)
    n_rows = (-(-(n * TOP_K + N_EXPERTS * (tm - 1)) // tm) + 1) * tm
    assign = jnp.arange(n * TOP_K, dtype=jnp.int32)
    packed = jnp.stack([assign, lax.bitcast_convert_type(top_w.reshape(-1), jnp.int32)], axis=1)
    fill = jnp.broadcast_to(jnp.array([-1, 0], jnp.int32), (n_rows, 2))
    buf = fill.at[dest].set(packed, unique_indices=True, mode="promise_in_bounds")
    a_buf = buf[:, 0]
    w_buf = lax.bitcast_convert_type(buf[:, 1], F32)
    real = a_buf >= 0
    tok_buf = jnp.where(real, a_buf // TOP_K, 0)
    spare = TOP_K * n + jnp.arange(n_rows, dtype=jnp.int32) % tm
    sidx = jnp.where(real, (a_buf % TOP_K) * n + tok_buf, spare)
    sidx = jnp.concatenate([spare[:tm], sidx[:-tm]])
    tile_start = jnp.arange(n_rows // tm, dtype=jnp.int32) * tm
    tile_e = jnp.minimum(jnp.sum(tile_start[:, None] >= pend[None, :], axis=1), N_EXPERTS - 1)
    n_valid = (pend[-1] // tm).astype(jnp.int32).reshape(1)
    return tok_buf, sidx, w_buf.reshape(-1, 1), tile_e.astype(jnp.int32), n_valid


def _head_slots(w, heads, width):
    k = w.shape[0]
    return jnp.pad(w.reshape(k, heads, width), ((0, 0), (0, 0), (0, LANE - width)))


def _rotate_half_slots(w3):
    half = QK_ROPE_DIM // 2
    lo, mid, hi = QK_NOPE_DIM, QK_NOPE_DIM + half, QK_HEAD_DIM
    out = jnp.zeros_like(w3)
    out = out.at[:, :, lo:mid].set(-w3[:, :, mid:hi])
    out = out.at[:, :, mid:hi].set(w3[:, :, lo:mid])
    return out


def _rope_gains(g):
    half = QK_ROPE_DIM // 2
    lo, mid, hi = QK_NOPE_DIM, QK_NOPE_DIM + half, QK_HEAD_DIM
    main = jnp.pad(g, (0, LANE - QK_HEAD_DIM))
    rot = jnp.zeros((LANE,), F32).at[lo:mid].set(g[mid:hi]).at[mid:hi].set(g[lo:mid])
    return main.reshape(1, LANE), rot.reshape(1, LANE)


def _rope_tables(positions):
    half = QK_ROPE_DIM // 2
    inv_freq = ROPE_BASE ** (-jnp.arange(0, QK_ROPE_DIM, 2, dtype=F32) / QK_ROPE_DIM)
    ang = positions.astype(F32).reshape(-1, 1) * inv_freq[None, :]
    n = ang.shape[0]
    dense = ang.reshape(n * half // LANE, LANE)
    cos_h = jnp.cos(dense).reshape(n, half)
    sin_h = jnp.sin(dense).reshape(n, half)
    pad = jnp.zeros((n, LANE - QK_HEAD_DIM), F32)
    cos = jnp.concatenate([jnp.ones((n, QK_NOPE_DIM), F32), cos_h, cos_h, pad], axis=-1)
    sin = jnp.concatenate([jnp.zeros((n, QK_NOPE_DIM), F32), sin_h, sin_h, pad], axis=-1)
    return cos, sin


def _block_diag(w):
    h, a, b = w.shape
    eye = jnp.eye(h, dtype=w.dtype)
    return (eye[:, None, :, None] * w[:, :, None, :]).reshape(h * a, h * b)


def kernel(x, positions, norm1_g, w_in, q_norm_g, w_uq, kv_norm_g, w_ukv, qk_q_g, qk_k_g, w_up_attn, conv_w, conv_b, w_rg, b_rg, w_ig, b_ig, lru_lambda, w_up_lru, w_o, norm2_g, ffn_w_gate, ffn_w_up, ffn_w_down, moe_router, moe_w_gate, moe_w_up, moe_w_down):
    batch, seq, d = x.shape
    n = batch * seq
    depth = norm1_g.shape[0]
    q_rank = q_norm_g.shape[1]
    kv_rank = kv_norm_g.shape[1]
    c_lru = conv_w.shape[2]
    head_w = d - 0
    lat = q_rank + kv_rank + QK_ROPE_DIM
    seg_w = -(-lat // LANE) * LANE
    widths = (seg_w, c_lru, c_lru, head_w, head_w)

    tm = 512
    tq = 1024
    t_lru = 128
    tm_moe = 512

    cos, sin = _rope_tables(positions)
    xf = x.reshape(n, d)
    for layer in range(depth):
        w_in_l = w_in[layer]
        w_in_p = jnp.concatenate(
            [w_in_l[:, :lat], jnp.zeros((d, seg_w - lat), F32), w_in_l[:, lat:]], axis=1).astype(BF16)
        seg, xl, gl, ga, gb = _in_proj(xf, norm1_g[layer].reshape(1, d), w_in_p, widths, tm)

        wq3 = _head_slots(w_uq[layer], MLA_HEADS, QK_HEAD_DIM)
        wkv3 = w_ukv[layer].reshape(kv_rank, MLA_HEADS, QK_NOPE_DIM + V_HEAD_DIM)
        ck_w = seg_w - q_rank
        wk3 = jnp.zeros((ck_w, MLA_HEADS, LANE), F32)
        wk3 = wk3.at[:kv_rank, :, :QK_NOPE_DIM].set(wkv3[:, :, :QK_NOPE_DIM])
        eye = jnp.eye(QK_ROPE_DIM, dtype=F32)
        wk3 = wk3.at[kv_rank:kv_rank + QK_ROPE_DIM, :, QK_NOPE_DIM:QK_HEAD_DIM].set(
            jnp.broadcast_to(eye[:, None, :], (QK_ROPE_DIM, MLA_HEADS, QK_ROPE_DIM)))
        wv = jnp.zeros((ck_w, MLA_HEADS * V_HEAD_DIM), F32).at[:kv_rank].set(
            wkv3[:, :, QK_NOPE_DIM:].reshape(kv_rank, MLA_HEADS * V_HEAD_DIM))
        flat = lambda w3: w3.reshape(w3.shape[0], MLA_HEADS * LANE).astype(BF16)
        gqm, gqr = _rope_gains(qk_q_g[layer])
        gkm, gkr = _rope_gains(qk_k_g[layer])
        q, kt, v = _qkv(seg, cos, sin, q_norm_g[layer].reshape(1, -1), kv_norm_g[layer].reshape(1, -1),
                        flat(wq3), flat(_rotate_half_slots(wq3)), flat(wk3), flat(_rotate_half_slots(wk3)),
                        wv.astype(BF16), gqm, gqr, gkm, gkr, tm, seq)
        o = _attention(q.reshape(batch, seq, -1), kt, v.reshape(batch, seq, -1), tq)

        wgate = jnp.concatenate([_block_diag(w_rg[layer]), _block_diag(w_ig[layer])], axis=1).astype(BF16)
        bgate = jnp.concatenate([b_rg[layer], b_ig[layer]]).reshape(1, -1)
        y = _lru(xl.reshape(batch, seq, c_lru), gl.reshape(batch, seq, c_lru), conv_w[layer],
                 conv_b[layer].reshape(1, -1), wgate, bgate, lru_lambda[layer].reshape(1, -1), t_lru)

        j = layer // 2
        is_moe = layer % 2 == 1
        router = None
        if is_moe:
            r_pad = jnp.pad(moe_router[j], ((0, 0), (0, LANE - N_EXPERTS)))
            r_hi = r_pad.astype(BF16)
            router = (r_hi, (r_pad - r_hi.astype(F32)).astype(BF16))
        outs = _merge(xf, o.reshape(n, -1), y.reshape(n, -1), ga, gb, w_up_attn[layer].astype(BF16),
                      w_up_lru[layer].astype(BF16), w_o[layer].astype(BF16), norm2_g[layer].reshape(1, d),
                      router, tm)
        if not is_moe:
            xn, h2 = outs
            xf = _ffn(h2, xn, ffn_w_gate[j].astype(BF16), ffn_w_up[j].astype(BF16),
                      ffn_w_down[j].astype(BF16), tm)
        else:
            xn, h2, logits = outs
            tok_buf, sidx, w_rows, tile_e, n_valid = _route(logits, tm_moe)
            y2 = _moe_ffn(tile_e, n_valid, tok_buf, sidx, h2, w_rows, moe_w_gate[j].astype(BF16),
                          moe_w_up[j].astype(BF16), moe_w_down[j].astype(BF16), TOP_K * n + tm_moe, tm_moe)
            xf = _moe_add(xn, y2, tm)
    return xf.reshape(batch, seq, d)
```

```python
import functools

import numpy as np
import jax
import jax.numpy as jnp
from jax import lax
from jax.experimental import pallas as pl
from jax.experimental.pallas import tpu as pltpu

BF16 = jnp.bfloat16
F32 = jnp.float32

MLA_HEADS = 8
QK_NOPE_DIM = 64
QK_ROPE_DIM = 32
V_HEAD_DIM = 64
QK_HEAD_DIM = QK_NOPE_DIM + QK_ROPE_DIM
ROPE_BASE = 10000.0
LRU_HEADS = 8
CONV_WIDTH = 4
LRU_C = 8.0
N_EXPERTS = 8
TOP_K = 2
RMS_EPS = 1e-6

LANE = 128
VMEM_LIMIT = 56 * 1024 * 1024


def _params(*sem):
    return pltpu.CompilerParams(dimension_semantics=sem, vmem_limit_bytes=VMEM_LIMIT)


def _rms(x, g):
    ms = jnp.mean(x * x, axis=-1, keepdims=True)
    return x * lax.rsqrt(ms + RMS_EPS) * g


def _dot(a, b):
    return jnp.dot(a, b, preferred_element_type=F32)


def _in_proj_body(x_ref, g_ref, w_ref, seg_ref, xl_ref, gl_ref, ga_ref, gb_ref, *, widths):
    h = _rms(x_ref[...], g_ref[...]).astype(BF16)
    o = np.cumsum((0,) + widths)
    seg_ref[...] = _dot(h, w_ref[:, o[0]:o[1]])
    xl_ref[...] = _dot(h, w_ref[:, o[1]:o[2]])
    gl_ref[...] = jax.nn.gelu(_dot(h, w_ref[:, o[2]:o[3]]), approximate=True).astype(BF16)
    ga_ref[...] = jax.nn.sigmoid(_dot(h, w_ref[:, o[3]:o[4]])).astype(BF16)
    gb_ref[...] = jax.nn.sigmoid(_dot(h, w_ref[:, o[4]:o[5]])).astype(BF16)


def _in_proj(x, g, w, widths, tm):
    n, d = x.shape
    dts = (F32, F32, BF16, BF16, BF16)
    row = lambda wd: pl.BlockSpec((tm, wd), lambda i: (i, 0))
    return pl.pallas_call(
        functools.partial(_in_proj_body, widths=widths),
        grid=(n // tm,),
        in_specs=[row(d), pl.BlockSpec((1, d), lambda i: (0, 0)),
                  pl.BlockSpec(w.shape, lambda i: (0, 0))],
        out_specs=[row(wd) for wd in widths],
        out_shape=[jax.ShapeDtypeStruct((n, wd), dt) for wd, dt in zip(widths, dts)],
        compiler_params=_params("parallel"),
        name="in_proj",
    )(x, g, w)


def _qkv_body(seg_ref, c_ref, s_ref, ct_ref, st_ref, gq_ref, gkv_ref, wq_ref, wqr_ref, wkt_ref, wkrt_ref, wv_ref,
              gqm_ref, gqr_ref, gkm_ref, gkr_ref, q_ref, kt_ref, v_ref, *, q_rank, kv_rank):
    cqn = _rms(seg_ref[:, 0:q_rank], gq_ref[...]).astype(BF16)
    ckvn = _rms(seg_ref[:, q_rank:q_rank + kv_rank], gkv_ref[...])
    ck = jnp.concatenate([ckvn, seg_ref[:, q_rank + kv_rank:]], axis=-1).astype(BF16)
    scale = QK_HEAD_DIM ** -0.5 * np.log2(np.e)
    inv_d = 1.0 / QK_HEAD_DIM

    a_all = _dot(cqn, wq_ref[...])
    b_all = _dot(cqn, wqr_ref[...])
    cm = gqm_ref[...] * c_ref[...] * scale
    sm = gqr_ref[...] * s_ref[...] * scale
    for h in range(MLA_HEADS):
        a = a_all[:, h * LANE:(h + 1) * LANE]
        b = b_all[:, h * LANE:(h + 1) * LANE]
        inv = lax.rsqrt(jnp.sum(a * a, axis=-1, keepdims=True) * inv_d + RMS_EPS)
        q_ref[:, h * LANE:(h + 1) * LANE] = (inv * (a * cm + b * sm)).astype(BF16)

    nt = (((1,), (1,)), ((), ()))
    at_all = lax.dot_general(wkt_ref[...], ck, nt, preferred_element_type=F32)
    bt_all = lax.dot_general(wkrt_ref[...], ck, nt, preferred_element_type=F32)
    cmt = gkm_ref[...] * ct_ref[...]
    smt = gkr_ref[...] * st_ref[...]
    for h in range(MLA_HEADS):
        a = at_all[h * LANE:(h + 1) * LANE, :]
        b = bt_all[h * LANE:(h + 1) * LANE, :]
        inv = lax.rsqrt(jnp.sum(a * a, axis=0, keepdims=True) * inv_d + RMS_EPS)
        kt_ref[0, h * LANE:(h + 1) * LANE, :] = (inv * (a * cmt + b * smt)).astype(BF16)
    v_ref[...] = _dot(ck, wv_ref[...]).astype(BF16)


def _qkv(seg, cos, sin, cos_t, sin_t, gq, gkv, wq, wqr, wkt, wkrt, wv, gqm, gqr, gkm, gkr, tm, seq):
    n, sw = seg.shape
    q_rank = gq.shape[1]
    kv_rank = gkv.shape[1]
    hq = wq.shape[1]
    hv = wv.shape[1]
    per_seq = seq // tm
    row = lambda wd: pl.BlockSpec((tm, wd), lambda i: (i, 0))
    col = pl.BlockSpec((LANE, tm), lambda i: (0, i))
    full = lambda a: pl.BlockSpec(a.shape, lambda i: (0, 0))
    consts = (gq, gkv, wq, wqr, wkt, wkrt, wv, gqm, gqr, gkm, gkr)
    return pl.pallas_call(
        functools.partial(_qkv_body, q_rank=q_rank, kv_rank=kv_rank),
        grid=(n // tm,),
        in_specs=[row(sw), row(LANE), row(LANE), col, col] + [full(a) for a in consts],
        out_specs=[row(hq), pl.BlockSpec((1, hq, tm), lambda i: (i // per_seq, 0, i % per_seq)), row(hv)],
        out_shape=[jax.ShapeDtypeStruct((n, hq), BF16), jax.ShapeDtypeStruct((n // seq, hq, seq), BF16),
                   jax.ShapeDtypeStruct((n, hv), BF16)],
        compiler_params=_params("parallel"),
        name="qkv_prep",
    )(seg, cos, sin, cos_t, sin_t, *consts)


ATTN_ROW_CHUNK = 256


def _attn_body(q_ref, kt_ref, v_ref, hm_ref, o_ref, m_s, l_s, acc_s, *, tq, heads):
    i = pl.program_id(2)
    rc = ATTN_ROW_CHUNK
    n_c = tq // rc
    m_s[...] = jnp.full(m_s.shape, -jnp.inf, F32)
    l_s[...] = jnp.zeros(l_s.shape, F32)
    acc_s[...] = jnp.zeros(acc_s.shape, F32)
    lane = lax.broadcasted_iota(jnp.int32, (rc, LANE), 1)
    tri = lax.broadcasted_iota(jnp.int32, (rc, rc), 1) <= lax.broadcasted_iota(jnp.int32, (rc, rc), 0)

    def per_head(vals):
        out = vals[heads - 1]
        for hh in range(heads - 2, -1, -1):
            out = jnp.where(lane < (hh + 1) * V_HEAD_DIM, vals[hh], out)
        return out

    def load_kv(start, ncols):
        kts = [kt_ref[0, hh * LANE:(hh + 1) * LANE, pl.ds(start, ncols)] for hh in range(heads)]
        vs = v_ref[0, pl.ds(start, ncols), :]
        parts = []
        for hh in range(heads):
            ind = jnp.broadcast_to(hm_ref[hh], vs.shape)
            parts.append(jnp.concatenate([vs * ind, ind], axis=1))
        return kts, jnp.concatenate(parts, axis=0)

    def chunk_update(c, kts, vcat, ncols, masked):
        rows = pl.ds(c * rc, rc)
        ps, alphas = [], []
        for hh in range(heads):
            s = _dot(q_ref[0, rows, hh * LANE:(hh + 1) * LANE], kts[hh])
            if masked:
                tail = jnp.where(tri, s[:, ncols - rc:], -jnp.inf)
                s = tail if ncols == rc else jnp.concatenate([s[:, :ncols - rc], tail], axis=1)
            m_old = m_s[hh, rows, :]
            m_new = jnp.maximum(m_old, jnp.max(s, axis=-1, keepdims=True))
            ps.append(jnp.exp2(s - jnp.concatenate([m_new] * (ncols // LANE), axis=1)).astype(BF16))
            alphas.append(jnp.exp2(m_old - m_new))
            m_s[hh, rows, :] = m_new
        pv = _dot(jnp.concatenate(ps, axis=1), vcat)
        alpha = per_head(alphas)
        acc_s[rows, :] = alpha * acc_s[rows, :] + pv[:, :LANE]
        l_s[rows, :] = alpha * l_s[rows, :] + pv[:, LANE:]

    def kv_step(j, carry):
        kts, vcat = load_kv(pl.multiple_of(j * tq, tq), tq)
        for c in range(n_c):
            chunk_update(c, kts, vcat, tq, masked=False)
        return carry

    lax.fori_loop(0, i, kv_step, 0)
    start = pl.multiple_of(i * tq, tq)
    for c in range(n_c):
        ncols = (c + 1) * rc
        kts, vcat = load_kv(start, ncols)
        chunk_update(c, kts, vcat, ncols, masked=True)
    o_ref[0] = (acc_s[...] / l_s[...]).astype(BF16)


def _attention(q, kt, v, tq):
    b, s, hq = q.shape
    heads = LANE // V_HEAD_DIM
    n_hp = hq // (heads * LANE)
    head_mask = jnp.asarray(np.arange(LANE)[None, None, :] // V_HEAD_DIM == np.arange(heads)[:, None, None], BF16)
    return pl.pallas_call(
        functools.partial(_attn_body, tq=tq, heads=heads),
        grid=(b, n_hp, s // tq),
        in_specs=[pl.BlockSpec((1, tq, heads * LANE), lambda bi, hp, i: (bi, i, hp)),
                  pl.BlockSpec((1, heads * LANE, s), lambda bi, hp, i: (bi, hp, 0)),
                  pl.BlockSpec((1, s, LANE), lambda bi, hp, i: (bi, 0, hp)),
                  pl.BlockSpec((heads, 1, LANE), lambda bi, hp, i: (0, 0, 0))],
        out_specs=pl.BlockSpec((1, tq, LANE), lambda bi, hp, i: (bi, i, hp)),
        out_shape=jax.ShapeDtypeStruct((b, s, v.shape[2]), BF16),
        scratch_shapes=[pltpu.VMEM((heads, tq, LANE), F32), pltpu.VMEM((tq, LANE), F32),
                        pltpu.VMEM((tq, LANE), F32)],
        compiler_params=_params("parallel", "parallel", "arbitrary"),
        name="attention",
    )(q, kt, v, head_mask)


def _lru_body(x_ref, gl_ref, cw_ref, cb_ref, wg_ref, bg_ref, lam_ref, y_ref,
              ext_s, a_s, b_s, h_s, *, nb, t):
    c = x_ref.shape[2]
    pad = 8

    @pl.when(pl.program_id(0) == 0)
    def _():
        ext_s[:, 0:pad, :] = jnp.zeros((nb, pad, c), F32)
        h_s[...] = jnp.zeros_like(h_s)

    ext_s[:, pad:pad + t, :] = x_ref[...]
    xc = cb_ref[...][None]
    for kk in range(CONV_WIDTH):
        off = pad - (CONV_WIDTH - 1) + kk
        xc = xc + ext_s[:, off:off + t, :] * cw_ref[kk:kk + 1, :][None]
    ext_s[:, 0:pad, :] = ext_s[:, t:t + pad, :]

    xc2 = xc.reshape(nb * t, c)
    g = _dot(xc2.astype(BF16), wg_ref[...]) + bg_ref[...]
    r = jax.nn.sigmoid(g[:, :c])
    ig = jax.nn.sigmoid(g[:, c:])
    log_a = -LRU_C * r * jax.nn.softplus(-lam_ref[...])
    a = jnp.exp(log_a)
    mult = jnp.sqrt(-jnp.tanh(log_a) * (a * a + 1.0))
    bv = mult * ig * xc2
    n_lt = c // LANE
    for lt in range(n_lt):
        a_s[lt] = a[:, lt * LANE:(lt + 1) * LANE]
        b_s[lt] = bv[:, lt * LANE:(lt + 1) * LANE]

    def step(tt, hs):
        idx = pl.ds(tt, nb, stride=t)
        new = []
        for lt in range(n_lt):
            h = a_s[lt, idx, :] * hs[lt] + b_s[lt, idx, :]
            b_s[lt, idx, :] = h
            new.append(h)
        return tuple(new)

    hs = lax.fori_loop(0, t, step, tuple(h_s[lt] for lt in range(n_lt)), unroll=8)
    for lt in range(n_lt):
        h_s[lt] = hs[lt]
    hseq = jnp.concatenate([b_s[lt] for lt in range(n_lt)], axis=-1)
    y_ref[...] = (hseq.reshape(nb, t, c) * gl_ref[...].astype(F32)).astype(BF16)


def _lru(xl, gl, cw, cb, wg, bg, lam, t):
    nb, s, c = xl.shape
    blk = pl.BlockSpec((nb, t, c), lambda i: (0, i, 0))
    full = lambda a: pl.BlockSpec(a.shape, lambda i: (0, 0))
    return pl.pallas_call(
        functools.partial(_lru_body, nb=nb, t=t),
        grid=(s // t,),
        in_specs=[blk, blk, full(cw), full(cb), full(wg), full(bg), full(lam)],
        out_specs=blk,
        out_shape=jax.ShapeDtypeStruct((nb, s, c), BF16),
        scratch_shapes=[pltpu.VMEM((nb, t + 8, c), F32), pltpu.VMEM((c // LANE, nb * t, LANE), F32),
                        pltpu.VMEM((c // LANE, nb * t, LANE), F32), pltpu.VMEM((c // LANE, nb, LANE), F32)],
        compiler_params=_params("arbitrary"),
        name="rg_lru",
    )(xl, gl, cw, cb, wg, bg, lam)


def _merge_body(x_ref, o_ref, y_ref, ga_ref, gb_ref, wa_ref, wb_ref, wo_ref, g2_ref, *rest, router):
    ua = _dot(o_ref[...], wa_ref[...])
    ub = _dot(y_ref[...], wb_ref[...])
    merged = ga_ref[...].astype(F32) * ua + gb_ref[...].astype(F32) * ub
    xn = x_ref[...] + _dot(merged.astype(BF16), wo_ref[...])
    h2 = _rms(xn, g2_ref[...])
    if router:
        rcat_ref, xn_ref, h2_ref, lg_ref = rest
        hi = h2.astype(BF16)
        lo = (h2 - hi.astype(F32)).astype(BF16)
        prod = _dot(jnp.concatenate([hi, lo], axis=1), rcat_ref[...])
        lg_ref[...] = prod[:, :LANE] + prod[:, LANE:]
        h2_ref[...] = h2
    else:
        xn_ref, h2_ref = rest
        h2_ref[...] = h2.astype(BF16)
    xn_ref[...] = xn


def _merge(x, o, y, ga, gb, wa, wb, wo, g2, router, tm):
    n, d = x.shape
    row = lambda a: pl.BlockSpec((tm, a.shape[1]), lambda i: (i, 0))
    full = lambda a: pl.BlockSpec(a.shape, lambda i: (0, 0))
    ins = [x, o, y, ga, gb]
    consts = [wa, wb, wo, g2] + list(router or ())
    out_shape = [jax.ShapeDtypeStruct((n, d), F32), jax.ShapeDtypeStruct((n, d), F32 if router else BF16)]
    if router:
        out_shape.append(jax.ShapeDtypeStruct((n, LANE), F32))
    return pl.pallas_call(
        functools.partial(_merge_body, router=bool(router)),
        grid=(n // tm,),
        in_specs=[row(a) for a in ins] + [full(a) for a in consts],
        out_specs=[pl.BlockSpec((tm, sh.shape[1]), lambda i: (i, 0)) for sh in out_shape],
        out_shape=out_shape,
        compiler_params=_params("parallel"),
        name="merge_router" if router else "merge",
    )(*ins, *consts)


def _ffn_body(h_ref, x_ref, wg_ref, wu_ref, wd_ref, o_ref):
    h = h_ref[...]
    act = (jax.nn.silu(_dot(h, wg_ref[...])) * _dot(h, wu_ref[...])).astype(BF16)
    o_ref[...] = x_ref[...] + _dot(act, wd_ref[...])


def _resident(a):
    return pl.BlockSpec(a.shape, lambda *_: (0,) * a.ndim, pipeline_mode=pl.Buffered(1))


def _ffn(h, x, wg, wu, wd, tm):
    n, d = x.shape
    row = pl.BlockSpec((tm, d), lambda i: (i, 0))
    return pl.pallas_call(
        _ffn_body,
        grid=(n // tm,),
        in_specs=[row, row, _resident(wg), _resident(wu), _resident(wd)],
        out_specs=row,
        out_shape=jax.ShapeDtypeStruct((n, d), F32),
        compiler_params=_params("parallel"),
        name="dense_ffn",
    )(h, x, wg, wu, wd)


def _row_copy(src_hbm, dst, src_row, dst_row, sem):
    return pltpu.make_async_copy(src_hbm.at[pl.ds(src_row, 1), :], dst.at[pl.ds(dst_row, 1), :], sem)


MOE_FF_STEPS = 2


def _moe_ffn_body(te_ref, nv_ref, tok_ref, tokn_ref, sidx_ref, w_ref, h_hbm, wg_ref, wu_ref, wd_ref, y_hbm,
                  xbuf, xb16, acc_ref, obuf, gsem, ssem, *, tm):
    i = pl.program_id(0)
    j = pl.program_id(1)
    n_i = pl.num_programs(0)
    slot = i % 2

    def wait_gather(s):
        pltpu.make_async_copy(h_hbm.at[pl.ds(0, tm), :], xbuf.at[s], gsem.at[s]).wait()

    def wait_scatter(s):
        pltpu.make_async_copy(obuf.at[s], y_hbm.at[pl.ds(0, tm), :], ssem.at[s]).wait()

    @pl.when((i == 0) & (j == 0))
    def _():
        obuf[...] = jnp.zeros_like(obuf)

        def issue(r, carry):
            _row_copy(h_hbm, xbuf.at[0], tok_ref[r], r, gsem.at[0]).start()
            return carry

        lax.fori_loop(0, tm, issue, 0, unroll=8)

    @pl.when(j == 0)
    def _():
        wait_gather(slot)
        xb16[...] = xbuf[slot].astype(BF16)

    def gather_next():
        for r in range(tm):
            _row_copy(h_hbm, xbuf.at[1 - slot], tokn_ref[r], r, gsem.at[1 - slot]).start()

    def scatter_prev():
        for r in range(tm):
            _row_copy(obuf.at[1 - slot], y_hbm, r, sidx_ref[r], ssem.at[1 - slot]).start()

    def compute(first):
        h = xb16[...]
        act = (jax.nn.silu(_dot(h, wg_ref[0])) * _dot(h, wu_ref[0])).astype(BF16)
        part = _dot(act, wd_ref[0])
        if first:
            acc_ref[...] = part
        else:
            obuf[slot] = (acc_ref[...] + part) * w_ref[...]

    valid = i < nv_ref[0]

    @pl.when(valid & (j == 0))
    def _():
        gather_next()
        compute(first=True)

    @pl.when(jnp.logical_not(valid) & (j == 0))
    def _():
        gather_next()

    @pl.when((j == 1) & (i >= 1))
    def _():
        wait_scatter(slot)

    @pl.when(valid & (j == 1))
    def _():
        scatter_prev()
        compute(first=False)

    @pl.when(jnp.logical_not(valid) & (j == 1) & (i < n_i - 1))
    def _():
        scatter_prev()

    @pl.when((i == n_i - 1) & (j == 1))
    def _():
        scatter_prev()
        wait_scatter(1 - slot)
        wait_gather(1 - slot)


def _moe_ffn(tile_e, n_valid, tok, sidx, h, w_rows, wg, wu, wd, y_rows, tm):
    d = h.shape[1]
    f = wg.shape[2]
    tf = f // MOE_FF_STEPS
    n_grid = tok.shape[0] // tm
    nxt = lambda i, j, te, nv: (jnp.minimum(i + 1, n_grid - 1),)
    grid_spec = pltpu.PrefetchScalarGridSpec(
        num_scalar_prefetch=2,
        grid=(n_grid, MOE_FF_STEPS),
        in_specs=[pl.BlockSpec((tm,), lambda i, j, te, nv: (i,), memory_space=pltpu.SMEM),
                  pl.BlockSpec((tm,), nxt, memory_space=pltpu.SMEM),
                  pl.BlockSpec((tm,), lambda i, j, te, nv: (i,), memory_space=pltpu.SMEM),
                  pl.BlockSpec((tm, 1), lambda i, j, te, nv: (i, 0)),
                  pl.BlockSpec(memory_space=pl.ANY),
                  pl.BlockSpec((1, d, tf), lambda i, j, te, nv: (te[i], 0, j)),
                  pl.BlockSpec((1, d, tf), lambda i, j, te, nv: (te[i], 0, j)),
                  pl.BlockSpec((1, tf, d), lambda i, j, te, nv: (te[i], j, 0))],
        out_specs=pl.BlockSpec(memory_space=pl.ANY),
        scratch_shapes=[pltpu.VMEM((2, tm, d), F32), pltpu.VMEM((tm, d), BF16), pltpu.VMEM((tm, d), F32),
                        pltpu.VMEM((2, tm, d), F32), pltpu.SemaphoreType.DMA((2,)),
                        pltpu.SemaphoreType.DMA((2,))],
    )
    return pl.pallas_call(
        functools.partial(_moe_ffn_body, tm=tm),
        grid_spec=grid_spec,
        out_shape=jax.ShapeDtypeStruct((y_rows, d), F32),
        compiler_params=_params("arbitrary", "arbitrary"),
        name="moe_ffn",
    )(tile_e, n_valid, tok, tok, sidx, w_rows, h, wg, wu, wd)


def _moe_add_body(x_ref, ya_ref, yb_ref, o_ref):
    o_ref[...] = x_ref[...] + ya_ref[...] + yb_ref[...]


def _moe_add(x, y2, tm):
    n, d = x.shape
    return pl.pallas_call(
        _moe_add_body,
        grid=(n // tm,),
        in_specs=[pl.BlockSpec((tm, d), lambda i: (i, 0)),
                  pl.BlockSpec((tm, d), lambda i: (i, 0)),
                  pl.BlockSpec((tm, d), lambda i: (i + n // tm, 0))],
        out_specs=pl.BlockSpec((tm, d), lambda i: (i, 0)),
        out_shape=jax.ShapeDtypeStruct((n, d), F32),
        compiler_params=_params("parallel"),
        name="moe_add",
    )(x, y2, y2)


def _route(logits, tm):
    n = logits.shape[0]
    top_logit, top_e = lax.top_k(logits[:, :N_EXPERTS], TOP_K)
    top_w = jax.nn.softmax(top_logit, axis=-1)
    flat_e = top_e.reshape(-1).astype(jnp.int32)
    onehot = (flat_e[:, None] == jnp.arange(N_EXPERTS, dtype=jnp.int32)[None, :]).astype(jnp.int32)
    csum = jnp.cumsum(onehot, axis=0)
    rank = jnp.sum((csum - onehot) * onehot, axis=1)
    counts = csum[-1]
    padded = (counts + tm - 1) // tm * tm
    pend = jnp.cumsum(padded)
    pstart = pend - padded
    dest = (pstart[flat_e] + rank).astype(jnp.int32)
    n_rows = (-(-(n * TOP_K + N_EXPERTS * (tm - 1)) // tm) + 1) * tm
    assign = jnp.arange(n * TOP_K, dtype=jnp.int32)
    packed = jnp.stack([assign, lax.bitcast_convert_type(top_w.reshape(-1), jnp.int32)], axis=1)
    fill = jnp.broadcast_to(jnp.array([-1, 0], jnp.int32), (n_rows, 2))
    buf = fill.at[dest].set(packed, unique_indices=True, mode="promise_in_bounds")
    a_buf = buf[:, 0]
    w_buf = lax.bitcast_convert_type(buf[:, 1], F32)
    real = a_buf >= 0
    tok_buf = jnp.where(real, a_buf // TOP_K, 0)
    spare = TOP_K * n + jnp.arange(n_rows, dtype=jnp.int32) % tm
    sidx = jnp.where(real, (a_buf % TOP_K) * n + tok_buf, spare)
    sidx = jnp.concatenate([spare[:tm], sidx[:-tm]])
    tile_start = jnp.arange(n_rows // tm, dtype=jnp.int32) * tm
    tile_e = jnp.minimum(jnp.sum(tile_start[:, None] >= pend[None, :], axis=1), N_EXPERTS - 1)
    n_valid = (pend[-1] // tm).astype(jnp.int32).reshape(1)
    return tok_buf, sidx, w_buf.reshape(-1, 1), tile_e.astype(jnp.int32), n_valid


def _head_slots(w, heads, width):
    k = w.shape[0]
    return jnp.pad(w.reshape(k, heads, width), ((0, 0), (0, 0), (0, LANE - width)))


def _rotate_half_slots(w3):
    half = QK_ROPE_DIM // 2
    lo, mid, hi = QK_NOPE_DIM, QK_NOPE_DIM + half, QK_HEAD_DIM
    out = jnp.zeros_like(w3)
    out = out.at[:, :, lo:mid].set(-w3[:, :, mid:hi])
    out = out.at[:, :, mid:hi].set(w3[:, :, lo:mid])
    return out


def _rope_gains(g):
    half = QK_ROPE_DIM // 2
    lo, mid, hi = QK_NOPE_DIM, QK_NOPE_DIM + half, QK_HEAD_DIM
    main = jnp.pad(g, (0, LANE - QK_HEAD_DIM))
    rot = jnp.zeros((LANE,), F32).at[lo:mid].set(g[mid:hi]).at[mid:hi].set(g[lo:mid])
    return main.reshape(1, LANE), rot.reshape(1, LANE)


def _rope_tables(positions):
    half = QK_ROPE_DIM // 2
    inv_freq = ROPE_BASE ** (-jnp.arange(0, QK_ROPE_DIM, 2, dtype=F32) / QK_ROPE_DIM)
    ang = positions.astype(F32).reshape(-1, 1) * inv_freq[None, :]
    n = ang.shape[0]
    dense = ang.reshape(n * half // LANE, LANE)
    cos_h = jnp.cos(dense).reshape(n, half)
    sin_h = jnp.sin(dense).reshape(n, half)
    pad = jnp.zeros((n, LANE - QK_HEAD_DIM), F32)
    cos = jnp.concatenate([jnp.ones((n, QK_NOPE_DIM), F32), cos_h, cos_h, pad], axis=-1)
    sin = jnp.concatenate([jnp.zeros((n, QK_NOPE_DIM), F32), sin_h, sin_h, pad], axis=-1)
    return cos, sin


def _block_diag(w):
    h, a, b = w.shape
    eye = jnp.eye(h, dtype=w.dtype)
    return (eye[:, None, :, None] * w[:, :, None, :]).reshape(h * a, h * b)


def kernel(x, positions, norm1_g, w_in, q_norm_g, w_uq, kv_norm_g, w_ukv, qk_q_g, qk_k_g, w_up_attn, conv_w, conv_b, w_rg, b_rg, w_ig, b_ig, lru_lambda, w_up_lru, w_o, norm2_g, ffn_w_gate, ffn_w_up, ffn_w_down, moe_router, moe_w_gate, moe_w_up, moe_w_down):
    batch, seq, d = x.shape
    n = batch * seq
    depth = norm1_g.shape[0]
    q_rank = q_norm_g.shape[1]
    kv_rank = kv_norm_g.shape[1]
    c_lru = conv_w.shape[2]
    head_w = d - 0
    lat = q_rank + kv_rank + QK_ROPE_DIM
    seg_w = -(-lat // LANE) * LANE
    widths = (seg_w, c_lru, c_lru, head_w, head_w)

    tm = 512
    tq = 2048
    t_lru = 128
    tm_moe = 512

    cos, sin = _rope_tables(positions)
    cos_t, sin_t = cos.T, sin.T
    xf = x.reshape(n, d)
    for layer in range(depth):
        w_in_l = w_in[layer]
        w_in_p = jnp.concatenate(
            [w_in_l[:, :lat], jnp.zeros((d, seg_w - lat), F32), w_in_l[:, lat:]], axis=1).astype(BF16)
        seg, xl, gl, ga, gb = _in_proj(xf, norm1_g[layer].reshape(1, d), w_in_p, widths, tm)

        wq3 = _head_slots(w_uq[layer], MLA_HEADS, QK_HEAD_DIM)
        wkv3 = w_ukv[layer].reshape(kv_rank, MLA_HEADS, QK_NOPE_DIM + V_HEAD_DIM)
        ck_w = seg_w - q_rank
        wk3 = jnp.zeros((ck_w, MLA_HEADS, LANE), F32)
        wk3 = wk3.at[:kv_rank, :, :QK_NOPE_DIM].set(wkv3[:, :, :QK_NOPE_DIM])
        eye = jnp.eye(QK_ROPE_DIM, dtype=F32)
        wk3 = wk3.at[kv_rank:kv_rank + QK_ROPE_DIM, :, QK_NOPE_DIM:QK_HEAD_DIM].set(
            jnp.broadcast_to(eye[:, None, :], (QK_ROPE_DIM, MLA_HEADS, QK_ROPE_DIM)))
        wv = jnp.zeros((ck_w, MLA_HEADS * V_HEAD_DIM), F32).at[:kv_rank].set(
            wkv3[:, :, QK_NOPE_DIM:].reshape(kv_rank, MLA_HEADS * V_HEAD_DIM))
        flat = lambda w3: w3.reshape(w3.shape[0], MLA_HEADS * LANE).astype(BF16)
        gqm, gqr = _rope_gains(qk_q_g[layer])
        gkm, gkr = (g.reshape(LANE, 1) for g in _rope_gains(qk_k_g[layer]))
        q, kt, v = _qkv(seg, cos, sin, cos_t, sin_t, q_norm_g[layer].reshape(1, -1),
                        kv_norm_g[layer].reshape(1, -1), flat(wq3), flat(_rotate_half_slots(wq3)),
                        flat(wk3).T, flat(_rotate_half_slots(wk3)).T, wv.astype(BF16), gqm, gqr, gkm, gkr, tm, seq)
        o = _attention(q.reshape(batch, seq, -1), kt, v.reshape(batch, seq, -1), tq)

        wgate = jnp.concatenate([_block_diag(w_rg[layer]), _block_diag(w_ig[layer])], axis=1).astype(BF16)
        bgate = jnp.concatenate([b_rg[layer], b_ig[layer]]).reshape(1, -1)
        y = _lru(xl.reshape(batch, seq, c_lru), gl.reshape(batch, seq, c_lru), conv_w[layer],
                 conv_b[layer].reshape(1, -1), wgate, bgate, lru_lambda[layer].reshape(1, -1), t_lru)

        j = layer // 2
        is_moe = layer % 2 == 1
        router = None
        if is_moe:
            r_pad = jnp.pad(moe_router[j], ((0, 0), (0, LANE - N_EXPERTS)))
            r_hi = r_pad.astype(BF16)
            r_lo = (r_pad - r_hi.astype(F32)).astype(BF16)
            router = (jnp.concatenate([jnp.concatenate([r_hi, r_lo], axis=1),
                                       jnp.concatenate([r_hi, jnp.zeros_like(r_lo)], axis=1)], axis=0),)
        outs = _merge(xf, o.reshape(n, -1), y.reshape(n, -1), ga, gb, w_up_attn[layer].astype(BF16),
                      w_up_lru[layer].astype(BF16), w_o[layer].astype(BF16), norm2_g[layer].reshape(1, d),
                      router, tm)
        if not is_moe:
            xn, h2 = outs
            xf = _ffn(h2, xn, ffn_w_gate[j].astype(BF16), ffn_w_up[j].astype(BF16),
                      ffn_w_down[j].astype(BF16), tm)
        else:
            xn, h2, logits = outs
            tok_buf, sidx, w_rows, tile_e, n_valid = _route(logits, tm_moe)
            y2 = _moe_ffn(tile_e, n_valid, tok_buf, sidx, h2, w_rows, moe_w_gate[j].astype(BF16),
                          moe_w_up[j].astype(BF16), moe_w_down[j].astype(BF16), TOP_K * n + tm_moe, tm_moe)
            xf = _moe_add(xn, y2, tm)
    return xf.reshape(batch, seq, d)
```

```python
import functools

import numpy as np
import jax
import jax.numpy as jnp
from jax import lax
from jax.experimental import pallas as pl
from jax.experimental.pallas import tpu as pltpu

BF16 = jnp.bfloat16
F32 = jnp.float32

MLA_HEADS = 8
QK_NOPE_DIM = 64
QK_ROPE_DIM = 32
V_HEAD_DIM = 64
QK_HEAD_DIM = QK_NOPE_DIM + QK_ROPE_DIM
ROPE_BASE = 10000.0
LRU_HEADS = 8
CONV_WIDTH = 4
LRU_C = 8.0
N_EXPERTS = 8
TOP_K = 2
RMS_EPS = 1e-6

LANE = 128
VMEM_LIMIT = 56 * 1024 * 1024


def _params(*sem):
    return pltpu.CompilerParams(dimension_semantics=sem, vmem_limit_bytes=VMEM_LIMIT)


def _rms(x, g):
    ms = jnp.mean(x * x, axis=-1, keepdims=True)
    return x * lax.rsqrt(ms + RMS_EPS) * g


def _dot(a, b):
    return jnp.dot(a, b, preferred_element_type=F32)


def _in_proj_body(x_ref, g_ref, w_ref, seg_ref, xl_ref, gl_ref, ga_ref, gb_ref, *, widths):
    h = _rms(x_ref[...], g_ref[...]).astype(BF16)
    o = np.cumsum((0,) + widths)
    seg_ref[...] = _dot(h, w_ref[:, o[0]:o[1]])
    xl_ref[...] = _dot(h, w_ref[:, o[1]:o[2]])
    gl_ref[...] = jax.nn.gelu(_dot(h, w_ref[:, o[2]:o[3]]), approximate=True).astype(BF16)
    ga_ref[...] = jax.nn.sigmoid(_dot(h, w_ref[:, o[3]:o[4]])).astype(BF16)
    gb_ref[...] = jax.nn.sigmoid(_dot(h, w_ref[:, o[4]:o[5]])).astype(BF16)


def _in_proj(x, g, w, widths, tm):
    n, d = x.shape
    dts = (F32, F32, BF16, BF16, BF16)
    row = lambda wd: pl.BlockSpec((tm, wd), lambda i: (i, 0))
    return pl.pallas_call(
        functools.partial(_in_proj_body, widths=widths),
        grid=(n // tm,),
        in_specs=[row(d), pl.BlockSpec((1, d), lambda i: (0, 0)),
                  pl.BlockSpec(w.shape, lambda i: (0, 0))],
        out_specs=[row(wd) for wd in widths],
        out_shape=[jax.ShapeDtypeStruct((n, wd), dt) for wd, dt in zip(widths, dts)],
        compiler_params=_params("parallel"),
        name="in_proj",
    )(x, g, w)


def _qkv_body(seg_ref, c_ref, s_ref, ct_ref, st_ref, gq_ref, gkv_ref, wq_ref, wqr_ref, wkt_ref, wkrt_ref, wv_ref,
              gqm_ref, gqr_ref, gkm_ref, gkr_ref, q_ref, kt_ref, v_ref, *, q_rank, kv_rank):
    cqn = _rms(seg_ref[:, 0:q_rank], gq_ref[...]).astype(BF16)
    ckvn = _rms(seg_ref[:, q_rank:q_rank + kv_rank], gkv_ref[...])
    ck = jnp.concatenate([ckvn, seg_ref[:, q_rank + kv_rank:]], axis=-1).astype(BF16)
    scale = QK_HEAD_DIM ** -0.5 * np.log2(np.e)
    inv_d = 1.0 / QK_HEAD_DIM

    a_all = _dot(cqn, wq_ref[...])
    b_all = _dot(cqn, wqr_ref[...])
    cm = gqm_ref[...] * c_ref[...] * scale
    sm = gqr_ref[...] * s_ref[...] * scale
    for h in range(MLA_HEADS):
        a = a_all[:, h * LANE:(h + 1) * LANE]
        b = b_all[:, h * LANE:(h + 1) * LANE]
        inv = lax.rsqrt(jnp.sum(a * a, axis=-1, keepdims=True) * inv_d + RMS_EPS)
        q_ref[:, h * LANE:(h + 1) * LANE] = (inv * (a * cm + b * sm)).astype(BF16)

    nt = (((1,), (1,)), ((), ()))
    at_all = lax.dot_general(wkt_ref[...], ck, nt, preferred_element_type=F32)
    bt_all = lax.dot_general(wkrt_ref[...], ck, nt, preferred_element_type=F32)
    cmt = gkm_ref[...] * ct_ref[...]
    smt = gkr_ref[...] * st_ref[...]
    for h in range(MLA_HEADS):
        a = at_all[h * LANE:(h + 1) * LANE, :]
        b = bt_all[h * LANE:(h + 1) * LANE, :]
        inv = lax.rsqrt(jnp.sum(a * a, axis=0, keepdims=True) * inv_d + RMS_EPS)
        kt_ref[0, h * LANE:(h + 1) * LANE, :] = (inv * (a * cmt + b * smt)).astype(BF16)
    v_ref[...] = _dot(ck, wv_ref[...]).astype(BF16)


def _qkv(seg, cos, sin, cos_t, sin_t, gq, gkv, wq, wqr, wkt, wkrt, wv, gqm, gqr, gkm, gkr, tm, seq):
    n, sw = seg.shape
    q_rank = gq.shape[1]
    kv_rank = gkv.shape[1]
    hq = wq.shape[1]
    hv = wv.shape[1]
    per_seq = seq // tm
    row = lambda wd: pl.BlockSpec((tm, wd), lambda i: (i, 0))
    col = pl.BlockSpec((LANE, tm), lambda i: (0, i))
    full = lambda a: pl.BlockSpec(a.shape, lambda i: (0, 0))
    consts = (gq, gkv, wq, wqr, wkt, wkrt, wv, gqm, gqr, gkm, gkr)
    return pl.pallas_call(
        functools.partial(_qkv_body, q_rank=q_rank, kv_rank=kv_rank),
        grid=(n // tm,),
        in_specs=[row(sw), row(LANE), row(LANE), col, col] + [full(a) for a in consts],
        out_specs=[row(hq), pl.BlockSpec((1, hq, tm), lambda i: (i // per_seq, 0, i % per_seq)), row(hv)],
        out_shape=[jax.ShapeDtypeStruct((n, hq), BF16), jax.ShapeDtypeStruct((n // seq, hq, seq), BF16),
                   jax.ShapeDtypeStruct((n, hv), BF16)],
        compiler_params=_params("parallel"),
        name="qkv_prep",
    )(seg, cos, sin, cos_t, sin_t, *consts)


ATTN_ROW_CHUNK = 256


def _attn_body(q_ref, kt_ref, v_ref, hm_ref, o_ref, m_s, l_s, acc_s, *, tq, heads):
    i = pl.program_id(2)
    rc = ATTN_ROW_CHUNK
    n_c = tq // rc
    m_s[...] = jnp.full(m_s.shape, -jnp.inf, F32)
    l_s[...] = jnp.zeros(l_s.shape, F32)
    acc_s[...] = jnp.zeros(acc_s.shape, F32)
    lane = lax.broadcasted_iota(jnp.int32, (rc, LANE), 1)
    tri = lax.broadcasted_iota(jnp.int32, (rc, rc), 1) <= lax.broadcasted_iota(jnp.int32, (rc, rc), 0)

    def per_head(vals):
        out = vals[heads - 1]
        for hh in range(heads - 2, -1, -1):
            out = jnp.where(lane < (hh + 1) * V_HEAD_DIM, vals[hh], out)
        return out

    def load_kv(start, ncols):
        kts = [kt_ref[0, hh * LANE:(hh + 1) * LANE, pl.ds(start, ncols)] for hh in range(heads)]
        vs = v_ref[0, pl.ds(start, ncols), :]
        parts = []
        for hh in range(heads):
            ind = jnp.broadcast_to(hm_ref[hh], vs.shape)
            parts.append(jnp.concatenate([vs * ind, ind], axis=1))
        return kts, jnp.concatenate(parts, axis=0)

    def chunk_update(c, kts, vcat, ncols, masked):
        rows = pl.ds(c * rc, rc)
        ps, alphas = [], []
        for hh in range(heads):
            s = _dot(q_ref[0, rows, hh * LANE:(hh + 1) * LANE], kts[hh])
            if masked:
                tail = jnp.where(tri, s[:, ncols - rc:], -jnp.inf)
                s = tail if ncols == rc else jnp.concatenate([s[:, :ncols - rc], tail], axis=1)
            m_old = m_s[hh, rows, :]
            m_new = jnp.maximum(m_old, jnp.max(s, axis=-1, keepdims=True))
            ps.append(jnp.exp2(s - jnp.concatenate([m_new] * (ncols // LANE), axis=1)).astype(BF16))
            alphas.append(jnp.exp2(m_old - m_new))
            m_s[hh, rows, :] = m_new
        pv = _dot(jnp.concatenate(ps, axis=1), vcat)
        alpha = per_head(alphas)
        acc_s[rows, :] = alpha * acc_s[rows, :] + pv[:, :LANE]
        l_s[rows, :] = alpha * l_s[rows, :] + pv[:, LANE:]

    def kv_step(j, carry):
        kts, vcat = load_kv(pl.multiple_of(j * tq, tq), tq)
        for c in range(n_c):
            chunk_update(c, kts, vcat, tq, masked=False)
        return carry

    lax.fori_loop(0, i, kv_step, 0)
    start = pl.multiple_of(i * tq, tq)
    for c in range(n_c):
        ncols = (c + 1) * rc
        kts, vcat = load_kv(start, ncols)
        chunk_update(c, kts, vcat, ncols, masked=True)
    o_ref[0] = (acc_s[...] / l_s[...]).astype(BF16)


def _attention(q, kt, v, tq):
    b, s, hq = q.shape
    heads = LANE // V_HEAD_DIM
    n_hp = hq // (heads * LANE)
    head_mask = jnp.asarray(np.arange(LANE)[None, None, :] // V_HEAD_DIM == np.arange(heads)[:, None, None], BF16)
    return pl.pallas_call(
        functools.partial(_attn_body, tq=tq, heads=heads),
        grid=(b, n_hp, s // tq),
        in_specs=[pl.BlockSpec((1, tq, heads * LANE), lambda bi, hp, i: (bi, i, hp)),
                  pl.BlockSpec((1, heads * LANE, s), lambda bi, hp, i: (bi, hp, 0)),
                  pl.BlockSpec((1, s, LANE), lambda bi, hp, i: (bi, 0, hp)),
                  pl.BlockSpec((heads, 1, LANE), lambda bi, hp, i: (0, 0, 0))],
        out_specs=pl.BlockSpec((1, tq, LANE), lambda bi, hp, i: (bi, i, hp)),
        out_shape=jax.ShapeDtypeStruct((b, s, v.shape[2]), BF16),
        scratch_shapes=[pltpu.VMEM((heads, tq, LANE), F32), pltpu.VMEM((tq, LANE), F32),
                        pltpu.VMEM((tq, LANE), F32)],
        compiler_params=_params("parallel", "parallel", "arbitrary"),
        name="attention",
    )(q, kt, v, head_mask)


F32_SUBLANES = 8


def _lru_body(x_ref, gl_ref, cw_ref, cb_ref, wg_ref, bg_ref, lam_ref, y_ref, ext_s, h_s, *, nb, t):
    c = x_ref.shape[2]
    sub = F32_SUBLANES

    @pl.when(pl.program_id(0) == 0)
    def _():
        ext_s[:, 0:sub, :] = jnp.zeros((nb, sub, c), F32)
        h_s[...] = jnp.zeros_like(h_s)

    ext_s[:, sub:sub + t, :] = x_ref[...]
    xc = cb_ref[...][None]
    for kk in range(CONV_WIDTH):
        off = sub - (CONV_WIDTH - 1) + kk
        xc = xc + ext_s[:, off:off + t, :] * cw_ref[kk:kk + 1, :][None]
    ext_s[:, 0:sub, :] = ext_s[:, t:t + sub, :]

    xc2 = xc.reshape(nb * t, c)
    g = _dot(xc2.astype(BF16), wg_ref[...]) + bg_ref[...]
    r = jax.nn.sigmoid(g[:, :c])
    ig = jax.nn.sigmoid(g[:, c:])
    log_a = -LRU_C * r * jax.nn.softplus(-lam_ref[...])
    a = jnp.exp(log_a)
    mult = jnp.sqrt(-jnp.tanh(log_a) * (a * a + 1.0))
    bv = mult * ig * xc2

    n_g = t // sub
    a3 = a.reshape(nb * n_g, sub, c)
    b3 = bv.reshape(nb * n_g, sub, c)
    row = lax.broadcasted_iota(jnp.int32, (1, sub, c), 1)
    d = 1
    while d < sub:
        keep = row >= d
        b3 = jnp.where(keep, a3 * pltpu.roll(b3, d, axis=1) + b3, b3)
        a3 = jnp.where(keep, a3 * pltpu.roll(a3, d, axis=1), a3)
        d *= 2
    a4 = a3.reshape(nb, n_g, sub, c)
    b4 = b3.reshape(nb, n_g, sub, c)
    h = h_s[...]
    groups = []
    for g in range(n_g):
        hg = a4[:, g] * h + b4[:, g]
        groups.append(hg)
        h = hg[:, sub - 1:sub, :]
    h_s[...] = h
    hseq = jnp.stack(groups, axis=1).reshape(nb, t, c)
    y_ref[...] = (hseq * gl_ref[...].astype(F32)).astype(BF16)


def _lru(xl, gl, cw, cb, wg, bg, lam, t):
    nb, s, c = xl.shape
    blk = pl.BlockSpec((nb, t, c), lambda i: (0, i, 0))
    full = lambda a: pl.BlockSpec(a.shape, lambda i: (0, 0))
    return pl.pallas_call(
        functools.partial(_lru_body, nb=nb, t=t),
        grid=(s // t,),
        in_specs=[blk, blk, full(cw), full(cb), full(wg), full(bg), full(lam)],
        out_specs=blk,
        out_shape=jax.ShapeDtypeStruct((nb, s, c), BF16),
        scratch_shapes=[pltpu.VMEM((nb, t + F32_SUBLANES, c), F32), pltpu.VMEM((nb, 1, c), F32)],
        compiler_params=_params("arbitrary"),
        name="rg_lru",
    )(xl, gl, cw, cb, wg, bg, lam)


def _merge_body(x_ref, o_ref, y_ref, ga_ref, gb_ref, wa_ref, wb_ref, wo_ref, g2_ref, *rest, router):
    ua = _dot(o_ref[...], wa_ref[...])
    ub = _dot(y_ref[...], wb_ref[...])
    merged = ga_ref[...].astype(F32) * ua + gb_ref[...].astype(F32) * ub
    xn = x_ref[...] + _dot(merged.astype(BF16), wo_ref[...])
    h2 = _rms(xn, g2_ref[...])
    if router:
        rcat_ref, xn_ref, h2_ref, lg_ref = rest
        hi = h2.astype(BF16)
        lo = (h2 - hi.astype(F32)).astype(BF16)
        prod = _dot(jnp.concatenate([hi, lo], axis=1), rcat_ref[...])
        lg_ref[...] = prod[:, :LANE] + prod[:, LANE:]
        h2_ref[...] = h2
    else:
        xn_ref, h2_ref = rest
        h2_ref[...] = h2.astype(BF16)
    xn_ref[...] = xn


def _merge(x, o, y, ga, gb, wa, wb, wo, g2, router, tm):
    n, d = x.shape
    row = lambda a: pl.BlockSpec((tm, a.shape[1]), lambda i: (i, 0))
    full = lambda a: pl.BlockSpec(a.shape, lambda i: (0, 0))
    ins = [x, o, y, ga, gb]
    consts = [wa, wb, wo, g2] + list(router or ())
    out_shape = [jax.ShapeDtypeStruct((n, d), F32), jax.ShapeDtypeStruct((n, d), F32 if router else BF16)]
    if router:
        out_shape.append(jax.ShapeDtypeStruct((n, LANE), F32))
    return pl.pallas_call(
        functools.partial(_merge_body, router=bool(router)),
        grid=(n // tm,),
        in_specs=[row(a) for a in ins] + [full(a) for a in consts],
        out_specs=[pl.BlockSpec((tm, sh.shape[1]), lambda i: (i, 0)) for sh in out_shape],
        out_shape=out_shape,
        compiler_params=_params("parallel"),
        name="merge_router" if router else "merge",
    )(*ins, *consts)


def _ffn_body(h_ref, x_ref, wg_ref, wu_ref, wd_ref, o_ref):
    h = h_ref[...]
    act = (jax.nn.silu(_dot(h, wg_ref[...])) * _dot(h, wu_ref[...])).astype(BF16)
    o_ref[...] = x_ref[...] + _dot(act, wd_ref[...])


def _resident(a):
    return pl.BlockSpec(a.shape, lambda *_: (0,) * a.ndim, pipeline_mode=pl.Buffered(1))


def _ffn(h, x, wg, wu, wd, tm):
    n, d = x.shape
    row = pl.BlockSpec((tm, d), lambda i: (i, 0))
    return pl.pallas_call(
        _ffn_body,
        grid=(n // tm,),
        in_specs=[row, row, _resident(wg), _resident(wu), _resident(wd)],
        out_specs=row,
        out_shape=jax.ShapeDtypeStruct((n, d), F32),
        compiler_params=_params("parallel"),
        name="dense_ffn",
    )(h, x, wg, wu, wd)


def _row_copy(src_hbm, dst, src_row, dst_row, sem):
    return pltpu.make_async_copy(src_hbm.at[pl.ds(src_row, 1), :], dst.at[pl.ds(dst_row, 1), :], sem)


MOE_FF_STEPS = 2


def _moe_ffn_body(te_ref, nv_ref, tok_ref, tokn_ref, sidx_ref, w_ref, h_hbm, wg_ref, wu_ref, wd_ref, y_hbm,
                  xbuf, xb16, acc_ref, obuf, gsem, ssem, *, tm):
    i = pl.program_id(0)
    j = pl.program_id(1)
    n_i = pl.num_programs(0)
    slot = i % 2

    def wait_gather(s):
        pltpu.make_async_copy(h_hbm.at[pl.ds(0, tm), :], xbuf.at[s], gsem.at[s]).wait()

    def wait_scatter(s):
        pltpu.make_async_copy(obuf.at[s], y_hbm.at[pl.ds(0, tm), :], ssem.at[s]).wait()

    @pl.when((i == 0) & (j == 0))
    def _():
        obuf[...] = jnp.zeros_like(obuf)

        def issue(r, carry):
            _row_copy(h_hbm, xbuf.at[0], tok_ref[r], r, gsem.at[0]).start()
            return carry

        lax.fori_loop(0, tm, issue, 0, unroll=8)

    @pl.when(j == 0)
    def _():
        wait_gather(slot)
        xb16[...] = xbuf[slot].astype(BF16)

    def gather_next():
        for r in range(tm):
            _row_copy(h_hbm, xbuf.at[1 - slot], tokn_ref[r], r, gsem.at[1 - slot]).start()

    def scatter_prev():
        for r in range(tm):
            _row_copy(obuf.at[1 - slot], y_hbm, r, sidx_ref[r], ssem.at[1 - slot]).start()

    def compute(first):
        h = xb16[...]
        act = (jax.nn.silu(_dot(h, wg_ref[0])) * _dot(h, wu_ref[0])).astype(BF16)
        part = _dot(act, wd_ref[0])
        if first:
            acc_ref[...] = part
        else:
            obuf[slot] = (acc_ref[...] + part) * w_ref[...]

    valid = i < nv_ref[0]

    @pl.when(valid & (j == 0))
    def _():
        gather_next()
        compute(first=True)

    @pl.when(jnp.logical_not(valid) & (j == 0))
    def _():
        gather_next()

    @pl.when((j == 1) & (i >= 1))
    def _():
        wait_scatter(slot)

    @pl.when(valid & (j == 1))
    def _():
        scatter_prev()
        compute(first=False)

    @pl.when(jnp.logical_not(valid) & (j == 1) & (i < n_i - 1))
    def _():
        scatter_prev()

    @pl.when((i == n_i - 1) & (j == 1))
    def _():
        scatter_prev()
        wait_scatter(1 - slot)
        wait_gather(1 - slot)


def _moe_ffn(tile_e, n_valid, tok, sidx, h, w_rows, wg, wu, wd, y_rows, tm):
    d = h.shape[1]
    f = wg.shape[2]
    tf = f // MOE_FF_STEPS
    n_grid = tok.shape[0] // tm
    nxt = lambda i, j, te, nv: (jnp.minimum(i + 1, n_grid - 1),)
    grid_spec = pltpu.PrefetchScalarGridSpec(
        num_scalar_prefetch=2,
        grid=(n_grid, MOE_FF_STEPS),
        in_specs=[pl.BlockSpec((tm,), lambda i, j, te, nv: (i,), memory_space=pltpu.SMEM),
                  pl.BlockSpec((tm,), nxt, memory_space=pltpu.SMEM),
                  pl.BlockSpec((tm,), lambda i, j, te, nv: (i,), memory_space=pltpu.SMEM),
                  pl.BlockSpec((tm, 1), lambda i, j, te, nv: (i, 0)),
                  pl.BlockSpec(memory_space=pl.ANY),
                  pl.BlockSpec((1, d, tf), lambda i, j, te, nv: (te[i], 0, j)),
                  pl.BlockSpec((1, d, tf), lambda i, j, te, nv: (te[i], 0, j)),
                  pl.BlockSpec((1, tf, d), lambda i, j, te, nv: (te[i], j, 0))],
        out_specs=pl.BlockSpec(memory_space=pl.ANY),
        scratch_shapes=[pltpu.VMEM((2, tm, d), F32), pltpu.VMEM((tm, d), BF16), pltpu.VMEM((tm, d), F32),
                        pltpu.VMEM((2, tm, d), F32), pltpu.SemaphoreType.DMA((2,)),
                        pltpu.SemaphoreType.DMA((2,))],
    )
    return pl.pallas_call(
        functools.partial(_moe_ffn_body, tm=tm),
        grid_spec=grid_spec,
        out_shape=jax.ShapeDtypeStruct((y_rows, d), F32),
        compiler_params=_params("arbitrary", "arbitrary"),
        name="moe_ffn",
    )(tile_e, n_valid, tok, tok, sidx, w_rows, h, wg, wu, wd)


def _moe_add_body(x_ref, ya_ref, yb_ref, o_ref):
    o_ref[...] = x_ref[...] + ya_ref[...] + yb_ref[...]


def _moe_add(x, y2, tm):
    n, d = x.shape
    return pl.pallas_call(
        _moe_add_body,
        grid=(n // tm,),
        in_specs=[pl.BlockSpec((tm, d), lambda i: (i, 0)),
                  pl.BlockSpec((tm, d), lambda i: (i, 0)),
                  pl.BlockSpec((tm, d), lambda i: (i + n // tm, 0))],
        out_specs=pl.BlockSpec((tm, d), lambda i: (i, 0)),
        out_shape=jax.ShapeDtypeStruct((n, d), F32),
        compiler_params=_params("parallel"),
        name="moe_add",
    )(x, y2, y2)


def _route(logits, tm):
    n = logits.shape[0]
    top_logit, top_e = lax.top_k(logits[:, :N_EXPERTS], TOP_K)
    top_w = jax.nn.softmax(top_logit, axis=-1)
    flat_e = top_e.reshape(-1).astype(jnp.int32)
    onehot = (flat_e[:, None] == jnp.arange(N_EXPERTS, dtype=jnp.int32)[None, :]).astype(jnp.int32)
    csum = jnp.cumsum(onehot, axis=0)
    rank = jnp.sum((csum - onehot) * onehot, axis=1)
    counts = csum[-1]
    padded = (counts + tm - 1) // tm * tm
    pend = jnp.cumsum(padded)
    pstart = pend - padded
    dest = (pstart[flat_e] + rank).astype(jnp.int32)
    n_rows = (-(-(n * TOP_K + N_EXPERTS * (tm - 1)) // tm) + 1) * tm
    assign = jnp.arange(n * TOP_K, dtype=jnp.int32)
    packed = jnp.stack([assign, lax.bitcast_convert_type(top_w.reshape(-1), jnp.int32)], axis=1)
    fill = jnp.broadcast_to(jnp.array([-1, 0], jnp.int32), (n_rows, 2))
    buf = fill.at[dest].set(packed, unique_indices=True, mode="promise_in_bounds")
    a_buf = buf[:, 0]
    w_buf = lax.bitcast_convert_type(buf[:, 1], F32)
    real = a_buf >= 0
    tok_buf = jnp.where(real, a_buf // TOP_K, 0)
    spare = TOP_K * n + jnp.arange(n_rows, dtype=jnp.int32) % tm
    sidx = jnp.where(real, (a_buf % TOP_K) * n + tok_buf, spare)
    sidx = jnp.concatenate([spare[:tm], sidx[:-tm]])
    tile_start = jnp.arange(n_rows // tm, dtype=jnp.int32) * tm
    tile_e = jnp.minimum(jnp.sum(tile_start[:, None] >= pend[None, :], axis=1), N_EXPERTS - 1)
    n_valid = (pend[-1] // tm).astype(jnp.int32).reshape(1)
    return tok_buf, sidx, w_buf.reshape(-1, 1), tile_e.astype(jnp.int32), n_valid


def _head_slots(w, heads, width):
    k = w.shape[0]
    return jnp.pad(w.reshape(k, heads, width), ((0, 0), (0, 0), (0, LANE - width)))


def _rotate_half_slots(w3):
    half = QK_ROPE_DIM // 2
    lo, mid, hi = QK_NOPE_DIM, QK_NOPE_DIM + half, QK_HEAD_DIM
    out = jnp.zeros_like(w3)
    out = out.at[:, :, lo:mid].set(-w3[:, :, mid:hi])
    out = out.at[:, :, mid:hi].set(w3[:, :, lo:mid])
    return out


def _rope_gains(g):
    half = QK_ROPE_DIM // 2
    lo, mid, hi = QK_NOPE_DIM, QK_NOPE_DIM + half, QK_HEAD_DIM
    main = jnp.pad(g, (0, LANE - QK_HEAD_DIM))
    rot = jnp.zeros((LANE,), F32).at[lo:mid].set(g[mid:hi]).at[mid:hi].set(g[lo:mid])
    return main.reshape(1, LANE), rot.reshape(1, LANE)


def _rope_tables(positions):
    half = QK_ROPE_DIM // 2
    inv_freq = ROPE_BASE ** (-jnp.arange(0, QK_ROPE_DIM, 2, dtype=F32) / QK_ROPE_DIM)
    ang = positions.astype(F32).reshape(-1, 1) * inv_freq[None, :]
    n = ang.shape[0]
    dense = ang.reshape(n * half // LANE, LANE)
    cos_h = jnp.cos(dense).reshape(n, half)
    sin_h = jnp.sin(dense).reshape(n, half)
    pad = jnp.zeros((n, LANE - QK_HEAD_DIM), F32)
    cos = jnp.concatenate([jnp.ones((n, QK_NOPE_DIM), F32), cos_h, cos_h, pad], axis=-1)
    sin = jnp.concatenate([jnp.zeros((n, QK_NOPE_DIM), F32), sin_h, sin_h, pad], axis=-1)
    return cos, sin


def _block_diag(w):
    h, a, b = w.shape
    eye = jnp.eye(h, dtype=w.dtype)
    return (eye[:, None, :, None] * w[:, :, None, :]).reshape(h * a, h * b)


def kernel(x, positions, norm1_g, w_in, q_norm_g, w_uq, kv_norm_g, w_ukv, qk_q_g, qk_k_g, w_up_attn, conv_w, conv_b, w_rg, b_rg, w_ig, b_ig, lru_lambda, w_up_lru, w_o, norm2_g, ffn_w_gate, ffn_w_up, ffn_w_down, moe_router, moe_w_gate, moe_w_up, moe_w_down):
    batch, seq, d = x.shape
    n = batch * seq
    depth = norm1_g.shape[0]
    q_rank = q_norm_g.shape[1]
    kv_rank = kv_norm_g.shape[1]
    c_lru = conv_w.shape[2]
    head_w = d - 0
    lat = q_rank + kv_rank + QK_ROPE_DIM
    seg_w = -(-lat // LANE) * LANE
    widths = (seg_w, c_lru, c_lru, head_w, head_w)

    tm = 512
    tq = 2048
    t_lru = 128
    tm_moe = 512

    cos, sin = _rope_tables(positions)
    cos_t, sin_t = cos.T, sin.T
    xf = x.reshape(n, d)
    for layer in range(depth):
        w_in_l = w_in[layer]
        w_in_p = jnp.concatenate(
            [w_in_l[:, :lat], jnp.zeros((d, seg_w - lat), F32), w_in_l[:, lat:]], axis=1).astype(BF16)
        seg, xl, gl, ga, gb = _in_proj(xf, norm1_g[layer].reshape(1, d), w_in_p, widths, tm)

        wq3 = _head_slots(w_uq[layer], MLA_HEADS, QK_HEAD_DIM)
        wkv3 = w_ukv[layer].reshape(kv_rank, MLA_HEADS, QK_NOPE_DIM + V_HEAD_DIM)
        ck_w = seg_w - q_rank
        wk3 = jnp.zeros((ck_w, MLA_HEADS, LANE), F32)
        wk3 = wk3.at[:kv_rank, :, :QK_NOPE_DIM].set(wkv3[:, :, :QK_NOPE_DIM])
        eye = jnp.eye(QK_ROPE_DIM, dtype=F32)
        wk3 = wk3.at[kv_rank:kv_rank + QK_ROPE_DIM, :, QK_NOPE_DIM:QK_HEAD_DIM].set(
            jnp.broadcast_to(eye[:, None, :], (QK_ROPE_DIM, MLA_HEADS, QK_ROPE_DIM)))
        wv = jnp.zeros((ck_w, MLA_HEADS * V_HEAD_DIM), F32).at[:kv_rank].set(
            wkv3[:, :, QK_NOPE_DIM:].reshape(kv_rank, MLA_HEADS * V_HEAD_DIM))
        flat = lambda w3: w3.reshape(w3.shape[0], MLA_HEADS * LANE).astype(BF16)
        gqm, gqr = _rope_gains(qk_q_g[layer])
        gkm, gkr = (g.reshape(LANE, 1) for g in _rope_gains(qk_k_g[layer]))
        q, kt, v = _qkv(seg, cos, sin, cos_t, sin_t, q_norm_g[layer].reshape(1, -1),
                        kv_norm_g[layer].reshape(1, -1), flat(wq3), flat(_rotate_half_slots(wq3)),
                        flat(wk3).T, flat(_rotate_half_slots(wk3)).T, wv.astype(BF16), gqm, gqr, gkm, gkr, tm, seq)
        o = _attention(q.reshape(batch, seq, -1), kt, v.reshape(batch, seq, -1), tq)

        wgate = jnp.concatenate([_block_diag(w_rg[layer]), _block_diag(w_ig[layer])], axis=1).astype(BF16)
        bgate = jnp.concatenate([b_rg[layer], b_ig[layer]]).reshape(1, -1)
        y = _lru(xl.reshape(batch, seq, c_lru), gl.reshape(batch, seq, c_lru), conv_w[layer],
                 conv_b[layer].reshape(1, -1), wgate, bgate, lru_lambda[layer].reshape(1, -1), t_lru)

        j = layer // 2
        is_moe = layer % 2 == 1
        router = None
        if is_moe:
            r_pad = jnp.pad(moe_router[j], ((0, 0), (0, LANE - N_EXPERTS)))
            r_hi = r_pad.astype(BF16)
            r_lo = (r_pad - r_hi.astype(F32)).astype(BF16)
            router = (jnp.concatenate([jnp.concatenate([r_hi, r_lo], axis=1),
                                       jnp.concatenate([r_hi, jnp.zeros_like(r_lo)], axis=1)], axis=0),)
        outs = _merge(xf, o.reshape(n, -1), y.reshape(n, -1), ga, gb, w_up_attn[layer].astype(BF16),
                      w_up_lru[layer].astype(BF16), w_o[layer].astype(BF16), norm2_g[layer].reshape(1, d),
                      router, tm)
        if not is_moe:
            xn, h2 = outs
            xf = _ffn(h2, xn, ffn_w_gate[j].astype(BF16), ffn_w_up[j].astype(BF16),
                      ffn_w_down[j].astype(BF16), tm)
        else:
            xn, h2, logits = outs
            wg_f, wu_f, wd_f, logits = lax.optimization_barrier(
                (moe_w_gate[j], moe_w_up[j], moe_w_down[j], logits))
            tok_buf, sidx, w_rows, tile_e, n_valid = _route(logits, tm_moe)
            y2 = _moe_ffn(tile_e, n_valid, tok_buf, sidx, h2, w_rows, wg_f.astype(BF16),
                          wu_f.astype(BF16), wd_f.astype(BF16), TOP_K * n + tm_moe, tm_moe)
            xf = _moe_add(xn, y2, tm)
    return xf.reshape(batch, seq, d)
```

```python
import functools

import numpy as np
import jax
import jax.numpy as jnp
from jax import lax
from jax.experimental import pallas as pl
from jax.experimental.pallas import tpu as pltpu

BF16 = jnp.bfloat16
F32 = jnp.float32

MLA_HEADS = 8
QK_NOPE_DIM = 64
QK_ROPE_DIM = 32
V_HEAD_DIM = 64
QK_HEAD_DIM = QK_NOPE_DIM + QK_ROPE_DIM
ROPE_BASE = 10000.0
LRU_HEADS = 8
CONV_WIDTH = 4
LRU_C = 8.0
N_EXPERTS = 8
TOP_K = 2
RMS_EPS = 1e-6

LANE = 128
VMEM_LIMIT = 56 * 1024 * 1024


def _params(*sem):
    return pltpu.CompilerParams(dimension_semantics=sem, vmem_limit_bytes=VMEM_LIMIT)


def _rms(x, g):
    ms = jnp.mean(x * x, axis=-1, keepdims=True)
    return x * lax.rsqrt(ms + RMS_EPS) * g


def _dot(a, b):
    return jnp.dot(a, b, preferred_element_type=F32)


def _in_proj_body(x_ref, g_ref, w_ref, seg_ref, xl_ref, gl_ref, ga_ref, gb_ref, *, widths):
    h = _rms(x_ref[...], g_ref[...]).astype(BF16)
    o = np.cumsum((0,) + widths)
    seg_ref[...] = _dot(h, w_ref[:, o[0]:o[1]])
    xl_ref[...] = _dot(h, w_ref[:, o[1]:o[2]])
    gl_ref[...] = jax.nn.gelu(_dot(h, w_ref[:, o[2]:o[3]]), approximate=True).astype(BF16)
    ga_ref[...] = jax.nn.sigmoid(_dot(h, w_ref[:, o[3]:o[4]])).astype(BF16)
    gb_ref[...] = jax.nn.sigmoid(_dot(h, w_ref[:, o[4]:o[5]])).astype(BF16)


def _in_proj(x, g, w, widths, tm):
    n, d = x.shape
    dts = (F32, F32, BF16, BF16, BF16)
    row = lambda wd: pl.BlockSpec((tm, wd), lambda i: (i, 0))
    return pl.pallas_call(
        functools.partial(_in_proj_body, widths=widths),
        grid=(n // tm,),
        in_specs=[row(d), _resident(g), _resident(w)],
        out_specs=[row(wd) for wd in widths],
        out_shape=[jax.ShapeDtypeStruct((n, wd), dt) for wd, dt in zip(widths, dts)],
        compiler_params=_params("parallel"),
        name="in_proj",
    )(x, g, w)


def _qkv_body(seg_ref, c_ref, s_ref, ct_ref, st_ref, gq_ref, gkv_ref, wq_ref, wqr_ref, wkt_ref, wkrt_ref, wv_ref,
              gqm_ref, gqr_ref, gkm_ref, gkr_ref, q_ref, kt_ref, v_ref, *, q_rank, kv_rank):
    cqn = _rms(seg_ref[:, 0:q_rank], gq_ref[...]).astype(BF16)
    ckvn = _rms(seg_ref[:, q_rank:q_rank + kv_rank], gkv_ref[...])
    ck = jnp.concatenate([ckvn, seg_ref[:, q_rank + kv_rank:]], axis=-1).astype(BF16)
    scale = QK_HEAD_DIM ** -0.5 * np.log2(np.e)
    inv_d = 1.0 / QK_HEAD_DIM

    a_all = _dot(cqn, wq_ref[...])
    b_all = _dot(cqn, wqr_ref[...])
    cm = gqm_ref[...] * c_ref[...] * scale
    sm = gqr_ref[...] * s_ref[...] * scale
    for h in range(MLA_HEADS):
        a = a_all[:, h * LANE:(h + 1) * LANE]
        b = b_all[:, h * LANE:(h + 1) * LANE]
        inv = lax.rsqrt(jnp.sum(a * a, axis=-1, keepdims=True) * inv_d + RMS_EPS)
        q_ref[:, h * LANE:(h + 1) * LANE] = (inv * (a * cm + b * sm)).astype(BF16)

    nt = (((1,), (1,)), ((), ()))
    at_all = lax.dot_general(wkt_ref[...], ck, nt, preferred_element_type=F32)
    bt_all = lax.dot_general(wkrt_ref[...], ck, nt, preferred_element_type=F32)
    cmt = gkm_ref[...] * ct_ref[...]
    smt = gkr_ref[...] * st_ref[...]
    for h in range(MLA_HEADS):
        a = at_all[h * LANE:(h + 1) * LANE, :]
        b = bt_all[h * LANE:(h + 1) * LANE, :]
        inv = lax.rsqrt(jnp.sum(a * a, axis=0, keepdims=True) * inv_d + RMS_EPS)
        kt_ref[0, h * LANE:(h + 1) * LANE, :] = (inv * (a * cmt + b * smt)).astype(BF16)
    v_ref[...] = _dot(ck, wv_ref[...]).astype(BF16)


def _qkv(seg, cos, sin, cos_t, sin_t, gq, gkv, wq, wqr, wkt, wkrt, wv, gqm, gqr, gkm, gkr, tm, seq):
    n, sw = seg.shape
    q_rank = gq.shape[1]
    kv_rank = gkv.shape[1]
    hq = wq.shape[1]
    hv = wv.shape[1]
    per_seq = seq // tm
    row = lambda wd: pl.BlockSpec((tm, wd), lambda i: (i, 0))
    col = pl.BlockSpec((LANE, tm), lambda i: (0, i))
    full = lambda a: pl.BlockSpec(a.shape, lambda i: (0, 0))
    consts = (gq, gkv, wq, wqr, wkt, wkrt, wv, gqm, gqr, gkm, gkr)
    return pl.pallas_call(
        functools.partial(_qkv_body, q_rank=q_rank, kv_rank=kv_rank),
        grid=(n // tm,),
        in_specs=[row(sw), row(LANE), row(LANE), col, col] + [full(a) for a in consts],
        out_specs=[row(hq), pl.BlockSpec((1, hq, tm), lambda i: (i // per_seq, 0, i % per_seq)), row(hv)],
        out_shape=[jax.ShapeDtypeStruct((n, hq), BF16), jax.ShapeDtypeStruct((n // seq, hq, seq), BF16),
                   jax.ShapeDtypeStruct((n, hv), BF16)],
        compiler_params=_params("parallel"),
        name="qkv_prep",
    )(seg, cos, sin, cos_t, sin_t, *consts)


ATTN_ROW_CHUNK = 256


def _attn_body(q_ref, kt_ref, v_ref, hm_ref, o_ref, m_s, l_s, acc_s, *, tq, heads):
    i = pl.program_id(2)
    rc = ATTN_ROW_CHUNK
    n_c = tq // rc
    m_s[...] = jnp.full(m_s.shape, -jnp.inf, F32)
    l_s[...] = jnp.zeros(l_s.shape, F32)
    acc_s[...] = jnp.zeros(acc_s.shape, F32)
    lane = lax.broadcasted_iota(jnp.int32, (rc, LANE), 1)
    tri = lax.broadcasted_iota(jnp.int32, (rc, rc), 1) <= lax.broadcasted_iota(jnp.int32, (rc, rc), 0)

    def per_head(vals):
        out = vals[heads - 1]
        for hh in range(heads - 2, -1, -1):
            out = jnp.where(lane < (hh + 1) * V_HEAD_DIM, vals[hh], out)
        return out

    def load_kv(start, ncols):
        kts = [kt_ref[0, hh * LANE:(hh + 1) * LANE, pl.ds(start, ncols)] for hh in range(heads)]
        vs = v_ref[0, pl.ds(start, ncols), :]
        parts = []
        for hh in range(heads):
            ind = jnp.broadcast_to(hm_ref[hh], vs.shape)
            parts.append(jnp.concatenate([vs * ind, ind], axis=1))
        return kts, jnp.concatenate(parts, axis=0)

    def chunk_update(c, kts, vcat, ncols, masked):
        rows = pl.ds(c * rc, rc)
        ps, alphas = [], []
        for hh in range(heads):
            s = _dot(q_ref[0, rows, hh * LANE:(hh + 1) * LANE], kts[hh])
            if masked:
                tail = jnp.where(tri, s[:, ncols - rc:], -jnp.inf)
                s = tail if ncols == rc else jnp.concatenate([s[:, :ncols - rc], tail], axis=1)
            m_old = m_s[hh, rows, :]
            m_new = jnp.maximum(m_old, jnp.max(s, axis=-1, keepdims=True))
            ps.append(jnp.exp2(s - jnp.concatenate([m_new] * (ncols // LANE), axis=1)).astype(BF16))
            alphas.append(jnp.exp2(m_old - m_new))
            m_s[hh, rows, :] = m_new
        pv = _dot(jnp.concatenate(ps, axis=1), vcat)
        alpha = per_head(alphas)
        acc_s[rows, :] = alpha * acc_s[rows, :] + pv[:, :LANE]
        l_s[rows, :] = alpha * l_s[rows, :] + pv[:, LANE:]

    def kv_step(j, carry):
        kts, vcat = load_kv(pl.multiple_of(j * tq, tq), tq)
        for c in range(n_c):
            chunk_update(c, kts, vcat, tq, masked=False)
        return carry

    lax.fori_loop(0, i, kv_step, 0)
    start = pl.multiple_of(i * tq, tq)
    for c in range(n_c):
        ncols = (c + 1) * rc
        kts, vcat = load_kv(start, ncols)
        chunk_update(c, kts, vcat, ncols, masked=True)
    o_ref[0] = (acc_s[...] / l_s[...]).astype(BF16)


def _attention(q, kt, v, tq):
    b, s, hq = q.shape
    heads = LANE // V_HEAD_DIM
    n_hp = hq // (heads * LANE)
    head_mask = jnp.asarray(np.arange(LANE)[None, None, :] // V_HEAD_DIM == np.arange(heads)[:, None, None], BF16)
    return pl.pallas_call(
        functools.partial(_attn_body, tq=tq, heads=heads),
        grid=(b, n_hp, s // tq),
        in_specs=[pl.BlockSpec((1, tq, heads * LANE), lambda bi, hp, i: (bi, i, hp)),
                  pl.BlockSpec((1, heads * LANE, s), lambda bi, hp, i: (bi, hp, 0)),
                  pl.BlockSpec((1, s, LANE), lambda bi, hp, i: (bi, 0, hp)),
                  pl.BlockSpec((heads, 1, LANE), lambda bi, hp, i: (0, 0, 0))],
        out_specs=pl.BlockSpec((1, tq, LANE), lambda bi, hp, i: (bi, i, hp)),
        out_shape=jax.ShapeDtypeStruct((b, s, v.shape[2]), BF16),
        scratch_shapes=[pltpu.VMEM((heads, tq, LANE), F32), pltpu.VMEM((tq, LANE), F32),
                        pltpu.VMEM((tq, LANE), F32)],
        compiler_params=_params("parallel", "parallel", "arbitrary"),
        name="attention",
    )(q, kt, v, head_mask)


F32_SUBLANES = 8


def _lru_body(x_ref, gl_ref, cw_ref, cb_ref, wg_ref, bg_ref, lam_ref, y_ref, ext_s, h_s, *, nb, t):
    c = x_ref.shape[2]
    sub = F32_SUBLANES

    @pl.when(pl.program_id(0) == 0)
    def _():
        ext_s[:, 0:sub, :] = jnp.zeros((nb, sub, c), F32)
        h_s[...] = jnp.zeros_like(h_s)

    ext_s[:, sub:sub + t, :] = x_ref[...]
    xc = cb_ref[...][None]
    for kk in range(CONV_WIDTH):
        off = sub - (CONV_WIDTH - 1) + kk
        xc = xc + ext_s[:, off:off + t, :] * cw_ref[kk:kk + 1, :][None]
    ext_s[:, 0:sub, :] = ext_s[:, t:t + sub, :]

    xc2 = xc.reshape(nb * t, c)
    g = _dot(xc2.astype(BF16), wg_ref[...]) + bg_ref[...]
    r = jax.nn.sigmoid(g[:, :c])
    ig = jax.nn.sigmoid(g[:, c:])
    log_a = -LRU_C * r * jax.nn.softplus(-lam_ref[...])
    a = jnp.exp(log_a)
    mult = jnp.sqrt(-jnp.tanh(log_a) * (a * a + 1.0))
    bv = mult * ig * xc2

    n_g = t // sub
    a3 = a.reshape(nb * n_g, sub, c)
    b3 = bv.reshape(nb * n_g, sub, c)
    row = lax.broadcasted_iota(jnp.int32, (1, sub, c), 1)
    d = 1
    while d < sub:
        keep = row >= d
        b3 = jnp.where(keep, a3 * pltpu.roll(b3, d, axis=1) + b3, b3)
        a3 = jnp.where(keep, a3 * pltpu.roll(a3, d, axis=1), a3)
        d *= 2
    a4 = a3.reshape(nb, n_g, sub, c)
    b4 = b3.reshape(nb, n_g, sub, c)
    h = h_s[...]
    groups = []
    for g in range(n_g):
        hg = a4[:, g] * h + b4[:, g]
        groups.append(hg)
        h = hg[:, sub - 1:sub, :]
    h_s[...] = h
    hseq = jnp.stack(groups, axis=1).reshape(nb, t, c)
    y_ref[...] = (hseq * gl_ref[...].astype(F32)).astype(BF16)


def _lru(xl, gl, cw, cb, wg, bg, lam, t):
    nb, s, c = xl.shape
    blk = pl.BlockSpec((nb, t, c), lambda i: (0, i, 0))
    full = lambda a: pl.BlockSpec(a.shape, lambda i: (0, 0))
    return pl.pallas_call(
        functools.partial(_lru_body, nb=nb, t=t),
        grid=(s // t,),
        in_specs=[blk, blk, full(cw), full(cb), full(wg), full(bg), full(lam)],
        out_specs=blk,
        out_shape=jax.ShapeDtypeStruct((nb, s, c), BF16),
        scratch_shapes=[pltpu.VMEM((nb, t + F32_SUBLANES, c), F32), pltpu.VMEM((nb, 1, c), F32)],
        compiler_params=_params("arbitrary"),
        name="rg_lru",
    )(xl, gl, cw, cb, wg, bg, lam)


def _merge_body(x_ref, o_ref, y_ref, ga_ref, gb_ref, wa_ref, wb_ref, wo_ref, g2_ref, *rest, router):
    ua = _dot(o_ref[...], wa_ref[...])
    ub = _dot(y_ref[...], wb_ref[...])
    merged = ga_ref[...].astype(F32) * ua + gb_ref[...].astype(F32) * ub
    xn = x_ref[...] + _dot(merged.astype(BF16), wo_ref[...])
    h2 = _rms(xn, g2_ref[...])
    if router:
        rcat_ref, xn_ref, h2_ref, lg_ref = rest
        hi = h2.astype(BF16)
        lo = (h2 - hi.astype(F32)).astype(BF16)
        prod = _dot(jnp.concatenate([hi, lo], axis=1), rcat_ref[...])
        lg_ref[...] = prod[:, :LANE] + prod[:, LANE:]
        h2_ref[...] = h2
        xn_ref[...] = xn
    else:
        wg_ref, wu_ref, wd_ref, out_ref = rest
        hb = h2.astype(BF16)
        act = (jax.nn.silu(_dot(hb, wg_ref[...])) * _dot(hb, wu_ref[...])).astype(BF16)
        out_ref[...] = xn + _dot(act, wd_ref[...])


def _merge(x, o, y, ga, gb, wa, wb, wo, g2, tm, router=None, ffn=None):
    n, d = x.shape
    row = lambda a: pl.BlockSpec((tm, a.shape[1]), lambda i: (i, 0))
    ins = [x, o, y, ga, gb]
    consts = [wa, wb, wo, g2] + list(router or ffn)
    if router:
        out_shape = [jax.ShapeDtypeStruct((n, d), F32), jax.ShapeDtypeStruct((n, d), F32),
                     jax.ShapeDtypeStruct((n, LANE), F32)]
    else:
        out_shape = [jax.ShapeDtypeStruct((n, d), F32)]
    return pl.pallas_call(
        functools.partial(_merge_body, router=bool(router)),
        grid=(n // tm,),
        in_specs=[row(a) for a in ins] + [_resident(a) for a in consts],
        out_specs=[pl.BlockSpec((tm, sh.shape[1]), lambda i: (i, 0)) for sh in out_shape],
        out_shape=out_shape,
        compiler_params=_params("parallel"),
        name="merge_router" if router else "merge_ffn",
    )(*ins, *consts)


def _resident(a):
    return pl.BlockSpec(a.shape, lambda *_: (0,) * a.ndim, pipeline_mode=pl.Buffered(1))


def _row_copy(src_hbm, dst, src_row, dst_row, sem):
    return pltpu.make_async_copy(src_hbm.at[pl.ds(src_row, 1), :], dst.at[pl.ds(dst_row, 1), :], sem)


MOE_FF_STEPS = 2


def _moe_ffn_body(te_ref, nv_ref, tok_ref, tokn_ref, sidx_ref, w_ref, h_hbm, wg_ref, wu_ref, wd_ref, y_hbm,
                  xbuf, xb16, acc_ref, obuf, gsem, ssem, *, tm):
    i = pl.program_id(0)
    j = pl.program_id(1)
    n_i = pl.num_programs(0)
    slot = i % 2

    def wait_gather(s):
        pltpu.make_async_copy(h_hbm.at[pl.ds(0, tm), :], xbuf.at[s], gsem.at[s]).wait()

    def wait_scatter(s):
        pltpu.make_async_copy(obuf.at[s], y_hbm.at[pl.ds(0, tm), :], ssem.at[s]).wait()

    @pl.when((i == 0) & (j == 0))
    def _():
        obuf[...] = jnp.zeros_like(obuf)

        def issue(r, carry):
            _row_copy(h_hbm, xbuf.at[0], tok_ref[r], r, gsem.at[0]).start()
            return carry

        lax.fori_loop(0, tm, issue, 0, unroll=8)

    @pl.when(j == 0)
    def _():
        wait_gather(slot)
        xb16[...] = xbuf[slot].astype(BF16)

    def gather_next():
        for r in range(tm):
            _row_copy(h_hbm, xbuf.at[1 - slot], tokn_ref[r], r, gsem.at[1 - slot]).start()

    def scatter_prev():
        for r in range(tm):
            _row_copy(obuf.at[1 - slot], y_hbm, r, sidx_ref[r], ssem.at[1 - slot]).start()

    def compute(first):
        h = xb16[...]
        act = (jax.nn.silu(_dot(h, wg_ref[0])) * _dot(h, wu_ref[0])).astype(BF16)
        part = _dot(act, wd_ref[0])
        if first:
            acc_ref[...] = part
        else:
            obuf[slot] = (acc_ref[...] + part) * w_ref[...]

    valid = i < nv_ref[0]

    @pl.when(valid & (j == 0))
    def _():
        gather_next()
        compute(first=True)

    @pl.when(jnp.logical_not(valid) & (j == 0))
    def _():
        gather_next()

    @pl.when((j == 1) & (i >= 1))
    def _():
        wait_scatter(slot)

    @pl.when(valid & (j == 1))
    def _():
        scatter_prev()
        compute(first=False)

    @pl.when(jnp.logical_not(valid) & (j == 1) & (i < n_i - 1))
    def _():
        scatter_prev()

    @pl.when((i == n_i - 1) & (j == 1))
    def _():
        scatter_prev()
        wait_scatter(1 - slot)
        wait_gather(1 - slot)


def _moe_ffn(tile_e, n_valid, tok, sidx, h, w_rows, wg, wu, wd, y_rows, tm):
    d = h.shape[1]
    f = wg.shape[2]
    tf = f // MOE_FF_STEPS
    n_grid = tok.shape[0] // tm
    nxt = lambda i, j, te, nv: (jnp.minimum(i + 1, n_grid - 1),)
    grid_spec = pltpu.PrefetchScalarGridSpec(
        num_scalar_prefetch=2,
        grid=(n_grid, MOE_FF_STEPS),
        in_specs=[pl.BlockSpec((tm,), lambda i, j, te, nv: (i,), memory_space=pltpu.SMEM),
                  pl.BlockSpec((tm,), nxt, memory_space=pltpu.SMEM),
                  pl.BlockSpec((tm,), lambda i, j, te, nv: (i,), memory_space=pltpu.SMEM),
                  pl.BlockSpec((tm, 1), lambda i, j, te, nv: (i, 0)),
                  pl.BlockSpec(memory_space=pl.ANY),
                  pl.BlockSpec((1, d, tf), lambda i, j, te, nv: (te[i], 0, j)),
                  pl.BlockSpec((1, d, tf), lambda i, j, te, nv: (te[i], 0, j)),
                  pl.BlockSpec((1, tf, d), lambda i, j, te, nv: (te[i], j, 0))],
        out_specs=pl.BlockSpec(memory_space=pl.ANY),
        scratch_shapes=[pltpu.VMEM((2, tm, d), F32), pltpu.VMEM((tm, d), BF16), pltpu.VMEM((tm, d), F32),
                        pltpu.VMEM((2, tm, d), F32), pltpu.SemaphoreType.DMA((2,)),
                        pltpu.SemaphoreType.DMA((2,))],
    )
    return pl.pallas_call(
        functools.partial(_moe_ffn_body, tm=tm),
        grid_spec=grid_spec,
        out_shape=jax.ShapeDtypeStruct((y_rows, d), F32),
        compiler_params=_params("arbitrary", "arbitrary"),
        name="moe_ffn",
    )(tile_e, n_valid, tok, tok, sidx, w_rows, h, wg, wu, wd)


def _moe_add_body(x_ref, ya_ref, yb_ref, o_ref):
    o_ref[...] = x_ref[...] + ya_ref[...] + yb_ref[...]


def _moe_add(x, y2, tm):
    n, d = x.shape
    return pl.pallas_call(
        _moe_add_body,
        grid=(n // tm,),
        in_specs=[pl.BlockSpec((tm, d), lambda i: (i, 0)),
                  pl.BlockSpec((tm, d), lambda i: (i, 0)),
                  pl.BlockSpec((tm, d), lambda i: (i + n // tm, 0))],
        out_specs=pl.BlockSpec((tm, d), lambda i: (i, 0)),
        out_shape=jax.ShapeDtypeStruct((n, d), F32),
        compiler_params=_params("parallel"),
        name="moe_add",
    )(x, y2, y2)


def _route(logits, tm):
    n = logits.shape[0]
    top_logit, top_e = lax.top_k(logits[:, :N_EXPERTS], TOP_K)
    top_w = jax.nn.softmax(top_logit, axis=-1)
    flat_e = top_e.reshape(-1).astype(jnp.int32)
    onehot = (flat_e[:, None] == jnp.arange(N_EXPERTS, dtype=jnp.int32)[None, :]).astype(jnp.int32)
    csum = jnp.cumsum(onehot, axis=0)
    rank = jnp.sum((csum - onehot) * onehot, axis=1)
    counts = csum[-1]
    padded = (counts + tm - 1) // tm * tm
    pend = jnp.cumsum(padded)
    pstart = pend - padded
    dest = (pstart[flat_e] + rank).astype(jnp.int32)
    n_rows = (-(-(n * TOP_K + N_EXPERTS * (tm - 1)) // tm) + 1) * tm
    assign = jnp.arange(n * TOP_K, dtype=jnp.int32)
    packed = jnp.stack([assign, lax.bitcast_convert_type(top_w.reshape(-1), jnp.int32)], axis=1)
    fill = jnp.broadcast_to(jnp.array([-1, 0], jnp.int32), (n_rows, 2))
    buf = fill.at[dest].set(packed, unique_indices=True, mode="promise_in_bounds")
    a_buf = buf[:, 0]
    w_buf = lax.bitcast_convert_type(buf[:, 1], F32)
    real = a_buf >= 0
    tok_buf = jnp.where(real, a_buf // TOP_K, 0)
    spare = TOP_K * n + jnp.arange(n_rows, dtype=jnp.int32) % tm
    sidx = jnp.where(real, (a_buf % TOP_K) * n + tok_buf, spare)
    sidx = jnp.concatenate([spare[:tm], sidx[:-tm]])
    tile_start = jnp.arange(n_rows // tm, dtype=jnp.int32) * tm
    tile_e = jnp.minimum(jnp.sum(tile_start[:, None] >= pend[None, :], axis=1), N_EXPERTS - 1)
    n_valid = (pend[-1] // tm).astype(jnp.int32).reshape(1)
    return tok_buf, sidx, w_buf.reshape(-1, 1), tile_e.astype(jnp.int32), n_valid


def _head_slots(w, heads, width):
    k = w.shape[0]
    return jnp.pad(w.reshape(k, heads, width), ((0, 0), (0, 0), (0, LANE - width)))


def _rotate_half_slots(w3):
    half = QK_ROPE_DIM // 2
    lo, mid, hi = QK_NOPE_DIM, QK_NOPE_DIM + half, QK_HEAD_DIM
    out = jnp.zeros_like(w3)
    out = out.at[:, :, lo:mid].set(-w3[:, :, mid:hi])
    out = out.at[:, :, mid:hi].set(w3[:, :, lo:mid])
    return out


def _rope_gains(g):
    half = QK_ROPE_DIM // 2
    lo, mid, hi = QK_NOPE_DIM, QK_NOPE_DIM + half, QK_HEAD_DIM
    main = jnp.pad(g, (0, LANE - QK_HEAD_DIM))
    rot = jnp.zeros((LANE,), F32).at[lo:mid].set(g[mid:hi]).at[mid:hi].set(g[lo:mid])
    return main.reshape(1, LANE), rot.reshape(1, LANE)


def _rope_tables(positions):
    half = QK_ROPE_DIM // 2
    inv_freq = ROPE_BASE ** (-jnp.arange(0, QK_ROPE_DIM, 2, dtype=F32) / QK_ROPE_DIM)
    ang = positions.astype(F32).reshape(-1, 1) * inv_freq[None, :]
    n = ang.shape[0]
    dense = ang.reshape(n * half // LANE, LANE)
    cos_h = jnp.cos(dense).reshape(n, half)
    sin_h = jnp.sin(dense).reshape(n, half)
    pad = jnp.zeros((n, LANE - QK_HEAD_DIM), F32)
    cos = jnp.concatenate([jnp.ones((n, QK_NOPE_DIM), F32), cos_h, cos_h, pad], axis=-1)
    sin = jnp.concatenate([jnp.zeros((n, QK_NOPE_DIM), F32), sin_h, sin_h, pad], axis=-1)
    return cos, sin


def _block_diag(w):
    h, a, b = w.shape
    eye = jnp.eye(h, dtype=w.dtype)
    return (eye[:, None, :, None] * w[:, :, None, :]).reshape(h * a, h * b)


def kernel(x, positions, norm1_g, w_in, q_norm_g, w_uq, kv_norm_g, w_ukv, qk_q_g, qk_k_g, w_up_attn, conv_w, conv_b, w_rg, b_rg, w_ig, b_ig, lru_lambda, w_up_lru, w_o, norm2_g, ffn_w_gate, ffn_w_up, ffn_w_down, moe_router, moe_w_gate, moe_w_up, moe_w_down):
    batch, seq, d = x.shape
    n = batch * seq
    depth = norm1_g.shape[0]
    q_rank = q_norm_g.shape[1]
    kv_rank = kv_norm_g.shape[1]
    c_lru = conv_w.shape[2]
    head_w = d - 0
    lat = q_rank + kv_rank + QK_ROPE_DIM
    seg_w = -(-lat // LANE) * LANE
    widths = (seg_w, c_lru, c_lru, head_w, head_w)

    tm = 512
    tq = 2048
    t_lru = 128
    tm_moe = 512

    cos, sin = _rope_tables(positions)
    cos_t, sin_t = cos.T, sin.T
    xf = x.reshape(n, d)
    for layer in range(depth):
        w_in_l = w_in[layer]
        w_in_p = jnp.concatenate(
            [w_in_l[:, :lat], jnp.zeros((d, seg_w - lat), F32), w_in_l[:, lat:]], axis=1).astype(BF16)
        seg, xl, gl, ga, gb = _in_proj(xf, norm1_g[layer].reshape(1, d), w_in_p, widths, 2 * tm)

        wq3 = _head_slots(w_uq[layer], MLA_HEADS, QK_HEAD_DIM)
        wkv3 = w_ukv[layer].reshape(kv_rank, MLA_HEADS, QK_NOPE_DIM + V_HEAD_DIM)
        ck_w = seg_w - q_rank
        wk3 = jnp.zeros((ck_w, MLA_HEADS, LANE), F32)
        wk3 = wk3.at[:kv_rank, :, :QK_NOPE_DIM].set(wkv3[:, :, :QK_NOPE_DIM])
        eye = jnp.eye(QK_ROPE_DIM, dtype=F32)
        wk3 = wk3.at[kv_rank:kv_rank + QK_ROPE_DIM, :, QK_NOPE_DIM:QK_HEAD_DIM].set(
            jnp.broadcast_to(eye[:, None, :], (QK_ROPE_DIM, MLA_HEADS, QK_ROPE_DIM)))
        wv = jnp.zeros((ck_w, MLA_HEADS * V_HEAD_DIM), F32).at[:kv_rank].set(
            wkv3[:, :, QK_NOPE_DIM:].reshape(kv_rank, MLA_HEADS * V_HEAD_DIM))
        flat = lambda w3: w3.reshape(w3.shape[0], MLA_HEADS * LANE).astype(BF16)
        gqm, gqr = _rope_gains(qk_q_g[layer])
        gkm, gkr = (g.reshape(LANE, 1) for g in _rope_gains(qk_k_g[layer]))
        q, kt, v = _qkv(seg, cos, sin, cos_t, sin_t, q_norm_g[layer].reshape(1, -1),
                        kv_norm_g[layer].reshape(1, -1), flat(wq3), flat(_rotate_half_slots(wq3)),
                        flat(wk3).T, flat(_rotate_half_slots(wk3)).T, wv.astype(BF16), gqm, gqr, gkm, gkr, tm, seq)
        o = _attention(q.reshape(batch, seq, -1), kt, v.reshape(batch, seq, -1), tq)

        wgate = jnp.concatenate([_block_diag(w_rg[layer]), _block_diag(w_ig[layer])], axis=1).astype(BF16)
        bgate = jnp.concatenate([b_rg[layer], b_ig[layer]]).reshape(1, -1)
        y = _lru(xl.reshape(batch, seq, c_lru), gl.reshape(batch, seq, c_lru), conv_w[layer],
                 conv_b[layer].reshape(1, -1), wgate, bgate, lru_lambda[layer].reshape(1, -1), t_lru)

        j = layer // 2
        is_moe = layer % 2 == 1
        merge_args = (xf, o.reshape(n, -1), y.reshape(n, -1), ga, gb, w_up_attn[layer].astype(BF16),
                      w_up_lru[layer].astype(BF16), w_o[layer].astype(BF16), norm2_g[layer].reshape(1, d), tm)
        if not is_moe:
            (xf,) = _merge(*merge_args, ffn=(ffn_w_gate[j].astype(BF16), ffn_w_up[j].astype(BF16),
                                             ffn_w_down[j].astype(BF16)))
        else:
            r_pad = jnp.pad(moe_router[j], ((0, 0), (0, LANE - N_EXPERTS)))
            r_hi = r_pad.astype(BF16)
            r_lo = (r_pad - r_hi.astype(F32)).astype(BF16)
            rcat = jnp.concatenate([jnp.concatenate([r_hi, r_lo], axis=1),
                                    jnp.concatenate([r_hi, jnp.zeros_like(r_lo)], axis=1)], axis=0)
            xn, h2, logits = _merge(*merge_args, router=(rcat,))
            tok_buf, sidx, w_rows, tile_e, n_valid = _route(logits, tm_moe)
            y2 = _moe_ffn(tile_e, n_valid, tok_buf, sidx, h2, w_rows, moe_w_gate[j].astype(BF16),
                          moe_w_up[j].astype(BF16), moe_w_down[j].astype(BF16), TOP_K * n + tm_moe, tm_moe)
            xf = _moe_add(xn, y2, tm)
    return xf.reshape(batch, seq, d)
```

```python
import functools

import numpy as np
import jax
import jax.numpy as jnp
from jax import lax
from jax.experimental import pallas as pl
from jax.experimental.pallas import tpu as pltpu

BF16 = jnp.bfloat16
F32 = jnp.float32

MLA_HEADS = 8
QK_NOPE_DIM = 64
QK_ROPE_DIM = 32
V_HEAD_DIM = 64
QK_HEAD_DIM = QK_NOPE_DIM + QK_ROPE_DIM
ROPE_BASE = 10000.0
LRU_HEADS = 8
CONV_WIDTH = 4
LRU_C = 8.0
N_EXPERTS = 8
TOP_K = 2
RMS_EPS = 1e-6

LANE = 128
VMEM_LIMIT = 56 * 1024 * 1024


def _params(*sem):
    return pltpu.CompilerParams(dimension_semantics=sem, vmem_limit_bytes=VMEM_LIMIT)


def _rms(x, g):
    ms = jnp.mean(x * x, axis=-1, keepdims=True)
    return x * lax.rsqrt(ms + RMS_EPS) * g


def _dot(a, b):
    return jnp.dot(a, b, preferred_element_type=F32)


def _in_proj_body(x_ref, g_ref, w_ref, seg_ref, xl_ref, gl_ref, ga_ref, gb_ref, *, widths):
    h = _rms(x_ref[...], g_ref[...]).astype(BF16)
    o = np.cumsum((0,) + widths)
    seg_ref[...] = _dot(h, w_ref[:, o[0]:o[1]])
    xl_ref[...] = _dot(h, w_ref[:, o[1]:o[2]])
    gl_ref[...] = jax.nn.gelu(_dot(h, w_ref[:, o[2]:o[3]]), approximate=True).astype(BF16)
    ga_ref[...] = jax.nn.sigmoid(_dot(h, w_ref[:, o[3]:o[4]])).astype(BF16)
    gb_ref[...] = jax.nn.sigmoid(_dot(h, w_ref[:, o[4]:o[5]])).astype(BF16)


def _in_proj(x, g, w, widths, tm):
    n, d = x.shape
    dts = (F32, F32, BF16, BF16, BF16)
    row = lambda wd: pl.BlockSpec((tm, wd), lambda i: (i, 0))
    return pl.pallas_call(
        functools.partial(_in_proj_body, widths=widths),
        grid=(n // tm,),
        in_specs=[row(d), _resident(g), _resident(w)],
        out_specs=[row(wd) for wd in widths],
        out_shape=[jax.ShapeDtypeStruct((n, wd), dt) for wd, dt in zip(widths, dts)],
        compiler_params=_params("parallel"),
        name="in_proj",
    )(x, g, w)


def _qkv_body(seg_ref, c_ref, s_ref, ct_ref, st_ref, gq_ref, gkv_ref, wq_ref, wqr_ref, wkt_ref, wkrt_ref, wv_ref,
              gqm_ref, gqr_ref, gkm_ref, gkr_ref, q_ref, kt_ref, v_ref, *, q_rank, kv_rank):
    cqn = _rms(seg_ref[:, 0:q_rank], gq_ref[...]).astype(BF16)
    ckvn = _rms(seg_ref[:, q_rank:q_rank + kv_rank], gkv_ref[...])
    ck = jnp.concatenate([ckvn, seg_ref[:, q_rank + kv_rank:]], axis=-1).astype(BF16)
    scale = QK_HEAD_DIM ** -0.5 * np.log2(np.e)
    inv_d = 1.0 / QK_HEAD_DIM

    a_all = _dot(cqn, wq_ref[...])
    b_all = _dot(cqn, wqr_ref[...])
    cm = gqm_ref[...] * c_ref[...] * scale
    sm = gqr_ref[...] * s_ref[...] * scale
    for h in range(MLA_HEADS):
        a = a_all[:, h * LANE:(h + 1) * LANE]
        b = b_all[:, h * LANE:(h + 1) * LANE]
        inv = lax.rsqrt(jnp.sum(a * a, axis=-1, keepdims=True) * inv_d + RMS_EPS)
        q_ref[:, h * LANE:(h + 1) * LANE] = (inv * (a * cm + b * sm)).astype(BF16)

    nt = (((1,), (1,)), ((), ()))
    at_all = lax.dot_general(wkt_ref[...], ck, nt, preferred_element_type=F32)
    bt_all = lax.dot_general(wkrt_ref[...], ck, nt, preferred_element_type=F32)
    cmt = gkm_ref[...] * ct_ref[...]
    smt = gkr_ref[...] * st_ref[...]
    for h in range(MLA_HEADS):
        a = at_all[h * LANE:(h + 1) * LANE, :]
        b = bt_all[h * LANE:(h + 1) * LANE, :]
        inv = lax.rsqrt(jnp.sum(a * a, axis=0, keepdims=True) * inv_d + RMS_EPS)
        kt_ref[0, h * LANE:(h + 1) * LANE, :] = (inv * (a * cmt + b * smt)).astype(BF16)
    v_ref[...] = _dot(ck, wv_ref[...]).astype(BF16)


def _qkv(seg, cos, sin, cos_t, sin_t, gq, gkv, wq, wqr, wkt, wkrt, wv, gqm, gqr, gkm, gkr, tm, seq):
    n, sw = seg.shape
    q_rank = gq.shape[1]
    kv_rank = gkv.shape[1]
    hq = wq.shape[1]
    hv = wv.shape[1]
    per_seq = seq // tm
    row = lambda wd: pl.BlockSpec((tm, wd), lambda i: (i, 0))
    col = pl.BlockSpec((LANE, tm), lambda i: (0, i))
    full = lambda a: pl.BlockSpec(a.shape, lambda i: (0, 0))
    consts = (gq, gkv, wq, wqr, wkt, wkrt, wv, gqm, gqr, gkm, gkr)
    return pl.pallas_call(
        functools.partial(_qkv_body, q_rank=q_rank, kv_rank=kv_rank),
        grid=(n // tm,),
        in_specs=[row(sw), row(LANE), row(LANE), col, col] + [full(a) for a in consts],
        out_specs=[row(hq), pl.BlockSpec((1, hq, tm), lambda i: (i // per_seq, 0, i % per_seq)), row(hv)],
        out_shape=[jax.ShapeDtypeStruct((n, hq), BF16), jax.ShapeDtypeStruct((n // seq, hq, seq), BF16),
                   jax.ShapeDtypeStruct((n, hv), BF16)],
        compiler_params=_params("parallel"),
        name="qkv_prep",
    )(seg, cos, sin, cos_t, sin_t, *consts)


ATTN_ROW_CHUNK = 256


def _attn_body(q_ref, kt_ref, v_ref, hm_ref, o_ref, m_s, l_s, acc_s, *, tq, heads):
    i = pl.program_id(2)
    rc = ATTN_ROW_CHUNK
    n_c = tq // rc
    m_s[...] = jnp.full(m_s.shape, -jnp.inf, F32)
    l_s[...] = jnp.zeros(l_s.shape, F32)
    acc_s[...] = jnp.zeros(acc_s.shape, F32)
    lane = lax.broadcasted_iota(jnp.int32, (rc, LANE), 1)
    tri = lax.broadcasted_iota(jnp.int32, (rc, rc), 1) <= lax.broadcasted_iota(jnp.int32, (rc, rc), 0)

    def per_head(vals):
        out = vals[heads - 1]
        for hh in range(heads - 2, -1, -1):
            out = jnp.where(lane < (hh + 1) * V_HEAD_DIM, vals[hh], out)
        return out

    def load_kv(start, ncols):
        kts = [kt_ref[0, hh * LANE:(hh + 1) * LANE, pl.ds(start, ncols)] for hh in range(heads)]
        vs = v_ref[0, pl.ds(start, ncols), :]
        parts = []
        for hh in range(heads):
            ind = jnp.broadcast_to(hm_ref[hh], vs.shape)
            parts.append(jnp.concatenate([vs * ind, ind], axis=1))
        return kts, jnp.concatenate(parts, axis=0)

    def chunk_update(c, kts, vcat, ncols, masked):
        rows = pl.ds(c * rc, rc)
        ps, alphas = [], []
        for hh in range(heads):
            s = _dot(q_ref[0, rows, hh * LANE:(hh + 1) * LANE], kts[hh])
            if masked:
                tail = jnp.where(tri, s[:, ncols - rc:], -jnp.inf)
                s = tail if ncols == rc else jnp.concatenate([s[:, :ncols - rc], tail], axis=1)
            m_old = m_s[hh, rows, :]
            m_new = jnp.maximum(m_old, jnp.max(s, axis=-1, keepdims=True))
            ps.append(jnp.exp2(s - jnp.concatenate([m_new] * (ncols // LANE), axis=1)).astype(BF16))
            alphas.append(jnp.exp2(m_old - m_new))
            m_s[hh, rows, :] = m_new
        pv = _dot(jnp.concatenate(ps, axis=1), vcat)
        alpha = per_head(alphas)
        acc_s[rows, :] = alpha * acc_s[rows, :] + pv[:, :LANE]
        l_s[rows, :] = alpha * l_s[rows, :] + pv[:, LANE:]

    def kv_step(j, carry):
        kts, vcat = load_kv(pl.multiple_of(j * tq, tq), tq)
        for c in range(n_c):
            chunk_update(c, kts, vcat, tq, masked=False)
        return carry

    lax.fori_loop(0, i, kv_step, 0)
    start = pl.multiple_of(i * tq, tq)
    for c in range(n_c):
        ncols = (c + 1) * rc
        kts, vcat = load_kv(start, ncols)
        chunk_update(c, kts, vcat, ncols, masked=True)
    o_ref[0] = (acc_s[...] / l_s[...]).astype(BF16)


def _attention(q, kt, v, tq):
    b, s, hq = q.shape
    heads = LANE // V_HEAD_DIM
    n_hp = hq // (heads * LANE)
    head_mask = jnp.asarray(np.arange(LANE)[None, None, :] // V_HEAD_DIM == np.arange(heads)[:, None, None], BF16)
    return pl.pallas_call(
        functools.partial(_attn_body, tq=tq, heads=heads),
        grid=(b, n_hp, s // tq),
        in_specs=[pl.BlockSpec((1, tq, heads * LANE), lambda bi, hp, i: (bi, i, hp)),
                  pl.BlockSpec((1, heads * LANE, s), lambda bi, hp, i: (bi, hp, 0)),
                  pl.BlockSpec((1, s, LANE), lambda bi, hp, i: (bi, 0, hp)),
                  pl.BlockSpec((heads, 1, LANE), lambda bi, hp, i: (0, 0, 0))],
        out_specs=pl.BlockSpec((1, tq, LANE), lambda bi, hp, i: (bi, i, hp)),
        out_shape=jax.ShapeDtypeStruct((b, s, v.shape[2]), BF16),
        scratch_shapes=[pltpu.VMEM((heads, tq, LANE), F32), pltpu.VMEM((tq, LANE), F32),
                        pltpu.VMEM((tq, LANE), F32)],
        compiler_params=_params("parallel", "parallel", "arbitrary"),
        name="attention",
    )(q, kt, v, head_mask)


F32_SUBLANES = 8


def _lru_body(x_ref, gl_ref, cw_ref, cb_ref, wg_ref, bg_ref, lam_ref, y_ref, ext_s, h_s, *, nb, t):
    c = x_ref.shape[2]
    sub = F32_SUBLANES

    @pl.when(pl.program_id(0) == 0)
    def _():
        ext_s[:, 0:sub, :] = jnp.zeros((nb, sub, c), F32)
        h_s[...] = jnp.zeros_like(h_s)

    ext_s[:, sub:sub + t, :] = x_ref[...]
    xc = cb_ref[...][None]
    for kk in range(CONV_WIDTH):
        off = sub - (CONV_WIDTH - 1) + kk
        xc = xc + ext_s[:, off:off + t, :] * cw_ref[kk:kk + 1, :][None]
    ext_s[:, 0:sub, :] = ext_s[:, t:t + sub, :]

    xc2 = xc.reshape(nb * t, c)
    g = _dot(xc2.astype(BF16), wg_ref[...]) + bg_ref[...]
    r = jax.nn.sigmoid(g[:, :c])
    ig = jax.nn.sigmoid(g[:, c:])
    log_a = -LRU_C * r * jax.nn.softplus(-lam_ref[...])
    a = jnp.exp(log_a)
    mult = jnp.sqrt(-jnp.tanh(log_a) * (a * a + 1.0))
    bv = mult * ig * xc2

    n_g = t // sub
    a3 = a.reshape(nb * n_g, sub, c)
    b3 = bv.reshape(nb * n_g, sub, c)
    row = lax.broadcasted_iota(jnp.int32, (1, sub, c), 1)
    d = 1
    while d < sub:
        keep = row >= d
        b3 = jnp.where(keep, a3 * pltpu.roll(b3, d, axis=1) + b3, b3)
        a3 = jnp.where(keep, a3 * pltpu.roll(a3, d, axis=1), a3)
        d *= 2
    a4 = a3.reshape(nb, n_g, sub, c)
    b4 = b3.reshape(nb, n_g, sub, c)
    h = h_s[...]
    groups = []
    for g in range(n_g):
        hg = a4[:, g] * h + b4[:, g]
        groups.append(hg)
        h = hg[:, sub - 1:sub, :]
    h_s[...] = h
    hseq = jnp.stack(groups, axis=1).reshape(nb, t, c)
    y_ref[...] = (hseq * gl_ref[...].astype(F32)).astype(BF16)


def _lru(xl, gl, cw, cb, wg, bg, lam, t):
    nb, s, c = xl.shape
    blk = pl.BlockSpec((nb, t, c), lambda i: (0, i, 0))
    full = lambda a: pl.BlockSpec(a.shape, lambda i: (0, 0))
    return pl.pallas_call(
        functools.partial(_lru_body, nb=nb, t=t),
        grid=(s // t,),
        in_specs=[blk, blk, full(cw), full(cb), full(wg), full(bg), full(lam)],
        out_specs=blk,
        out_shape=jax.ShapeDtypeStruct((nb, s, c), BF16),
        scratch_shapes=[pltpu.VMEM((nb, t + F32_SUBLANES, c), F32), pltpu.VMEM((nb, 1, c), F32)],
        compiler_params=_params("arbitrary"),
        name="rg_lru",
    )(xl, gl, cw, cb, wg, bg, lam)


def _merge_body(x_ref, o_ref, y_ref, ga_ref, gb_ref, wa_ref, wb_ref, wo_ref, g2_ref, *rest, router):
    ua = _dot(o_ref[...], wa_ref[...])
    ub = _dot(y_ref[...], wb_ref[...])
    merged = ga_ref[...].astype(F32) * ua + gb_ref[...].astype(F32) * ub
    xn = x_ref[...] + _dot(merged.astype(BF16), wo_ref[...])
    h2 = _rms(xn, g2_ref[...])
    if router:
        rcat_ref, xn_ref, h2_ref, lg_ref = rest
        hi = h2.astype(BF16)
        lo = (h2 - hi.astype(F32)).astype(BF16)
        prod = _dot(jnp.concatenate([hi, lo], axis=1), rcat_ref[...])
        lg_ref[...] = prod[:, :LANE] + prod[:, LANE:]
        h2_ref[...] = h2
        xn_ref[...] = xn
    else:
        wg_ref, wu_ref, wd_ref, out_ref = rest
        hb = h2.astype(BF16)
        act = (jax.nn.silu(_dot(hb, wg_ref[...])) * _dot(hb, wu_ref[...])).astype(BF16)
        out_ref[...] = xn + _dot(act, wd_ref[...])


def _merge(x, o, y, ga, gb, wa, wb, wo, g2, tm, router=None, ffn=None):
    n, d = x.shape
    row = lambda a: pl.BlockSpec((tm, a.shape[1]), lambda i: (i, 0))
    ins = [x, o, y, ga, gb]
    consts = [wa, wb, wo, g2] + list(router or ffn)
    if router:
        out_shape = [jax.ShapeDtypeStruct((n, d), F32), jax.ShapeDtypeStruct((n, d), F32),
                     jax.ShapeDtypeStruct((n, LANE), F32)]
    else:
        out_shape = [jax.ShapeDtypeStruct((n, d), F32)]
    return pl.pallas_call(
        functools.partial(_merge_body, router=bool(router)),
        grid=(n // tm,),
        in_specs=[row(a) for a in ins] + [_resident(a) for a in consts],
        out_specs=[pl.BlockSpec((tm, sh.shape[1]), lambda i: (i, 0)) for sh in out_shape],
        out_shape=out_shape,
        compiler_params=_params("parallel"),
        name="merge_router" if router else "merge_ffn",
    )(*ins, *consts)


def _resident(a):
    return pl.BlockSpec(a.shape, lambda *_: (0,) * a.ndim, pipeline_mode=pl.Buffered(1))


def _row_copy(src_hbm, dst, src_row, dst_row, sem):
    return pltpu.make_async_copy(src_hbm.at[pl.ds(src_row, 1), :], dst.at[pl.ds(dst_row, 1), :], sem)


MOE_FF_STEPS = 2


def _moe_ffn_body(te_ref, nv_ref, tok_ref, tokn_ref, sidx_ref, w_ref, h_hbm, wg_ref, wu_ref, wd_ref, y_hbm,
                  xbuf, xb16, acc_ref, obuf, gsem, ssem, *, tm):
    i = pl.program_id(0)
    j = pl.program_id(1)
    n_i = pl.num_programs(0)
    slot = i % 2

    def wait_gather(s):
        pltpu.make_async_copy(h_hbm.at[pl.ds(0, tm), :], xbuf.at[s], gsem.at[s]).wait()

    def wait_scatter(s):
        pltpu.make_async_copy(obuf.at[s], y_hbm.at[pl.ds(0, tm), :], ssem.at[s]).wait()

    @pl.when((i == 0) & (j == 0))
    def _():
        obuf[...] = jnp.zeros_like(obuf)

        def issue(r, carry):
            _row_copy(h_hbm, xbuf.at[0], tok_ref[r], r, gsem.at[0]).start()
            return carry

        lax.fori_loop(0, tm, issue, 0, unroll=8)

    @pl.when(j == 0)
    def _():
        wait_gather(slot)
        xb16[...] = xbuf[slot].astype(BF16)

    def gather_next():
        for r in range(tm):
            _row_copy(h_hbm, xbuf.at[1 - slot], tokn_ref[r], r, gsem.at[1 - slot]).start()

    def scatter_prev():
        for r in range(tm):
            _row_copy(obuf.at[1 - slot], y_hbm, r, sidx_ref[r], ssem.at[1 - slot]).start()

    def compute(first):
        h = xb16[...]
        act = (jax.nn.silu(_dot(h, wg_ref[0])) * _dot(h, wu_ref[0])).astype(BF16)
        part = _dot(act, wd_ref[0])
        if first:
            acc_ref[...] = part
        else:
            obuf[slot] = (acc_ref[...] + part) * w_ref[...]

    valid = i < nv_ref[0]

    @pl.when(valid & (j == 0))
    def _():
        gather_next()
        compute(first=True)

    @pl.when(jnp.logical_not(valid) & (j == 0))
    def _():
        gather_next()

    @pl.when((j == 1) & (i >= 1))
    def _():
        wait_scatter(slot)

    @pl.when(valid & (j == 1))
    def _():
        scatter_prev()
        compute(first=False)

    @pl.when(jnp.logical_not(valid) & (j == 1) & (i < n_i - 1))
    def _():
        scatter_prev()

    @pl.when((i == n_i - 1) & (j == 1))
    def _():
        scatter_prev()
        wait_scatter(1 - slot)
        wait_gather(1 - slot)


def _moe_ffn(tile_e, n_valid, tok, sidx, h, w_rows, wg, wu, wd, y_rows, tm):
    d = h.shape[1]
    f = wg.shape[2]
    tf = f // MOE_FF_STEPS
    n_grid = tok.shape[0] // tm
    nxt = lambda i, j, te, nv: (jnp.minimum(i + 1, n_grid - 1),)
    grid_spec = pltpu.PrefetchScalarGridSpec(
        num_scalar_prefetch=2,
        grid=(n_grid, MOE_FF_STEPS),
        in_specs=[pl.BlockSpec((tm,), lambda i, j, te, nv: (i,), memory_space=pltpu.SMEM),
                  pl.BlockSpec((tm,), nxt, memory_space=pltpu.SMEM),
                  pl.BlockSpec((tm,), lambda i, j, te, nv: (i,), memory_space=pltpu.SMEM),
                  pl.BlockSpec((tm, 1), lambda i, j, te, nv: (i, 0)),
                  pl.BlockSpec(memory_space=pl.ANY),
                  pl.BlockSpec((1, d, tf), lambda i, j, te, nv: (te[i], 0, j)),
                  pl.BlockSpec((1, d, tf), lambda i, j, te, nv: (te[i], 0, j)),
                  pl.BlockSpec((1, tf, d), lambda i, j, te, nv: (te[i], j, 0))],
        out_specs=pl.BlockSpec(memory_space=pl.ANY),
        scratch_shapes=[pltpu.VMEM((2, tm, d), F32), pltpu.VMEM((tm, d), BF16), pltpu.VMEM((tm, d), F32),
                        pltpu.VMEM((2, tm, d), F32), pltpu.SemaphoreType.DMA((2,)),
                        pltpu.SemaphoreType.DMA((2,))],
    )
    return pl.pallas_call(
        functools.partial(_moe_ffn_body, tm=tm),
        grid_spec=grid_spec,
        out_shape=jax.ShapeDtypeStruct((y_rows, d), F32),
        compiler_params=_params("arbitrary", "arbitrary"),
        name="moe_ffn",
    )(tile_e, n_valid, tok, tok, sidx, w_rows, h, wg, wu, wd)


def _moe_add_body(x_ref, ya_ref, yb_ref, o_ref):
    o_ref[...] = x_ref[...] + ya_ref[...] + yb_ref[...]


def _moe_add(x, y2, tm):
    n, d = x.shape
    return pl.pallas_call(
        _moe_add_body,
        grid=(n // tm,),
        in_specs=[pl.BlockSpec((tm, d), lambda i: (i, 0)),
                  pl.BlockSpec((tm, d), lambda i: (i, 0)),
                  pl.BlockSpec((tm, d), lambda i: (i + n // tm, 0))],
        out_specs=pl.BlockSpec((tm, d), lambda i: (i, 0)),
        out_shape=jax.ShapeDtypeStruct((n, d), F32),
        compiler_params=_params("parallel"),
        name="moe_add",
    )(x, y2, y2)


def _topk_body(lg_ref, tri_ref, info_ref, cnt_ref):
    tm = lg_ref.shape[0]
    l8 = lg_ref[...].T[0:N_EXPERTS, :]
    eidx = lax.broadcasted_iota(jnp.int32, l8.shape, 0)
    m1 = jnp.max(l8, axis=0, keepdims=True)
    e1 = jnp.min(jnp.where(l8 == m1, eidx, N_EXPERTS), axis=0, keepdims=True)
    sel1 = eidx == e1
    rest = jnp.where(sel1, -jnp.inf, l8)
    m2 = jnp.max(rest, axis=0, keepdims=True)
    e2 = jnp.min(jnp.where(rest == m2, eidx, N_EXPERTS), axis=0, keepdims=True)
    sel2 = eidx == e2
    ex = jnp.exp(m2 - m1)
    w1 = 1.0 / (1.0 + ex)
    w2 = ex / (1.0 + ex)
    onehot = jnp.where(sel1 | sel2, 1.0, 0.0)
    incl = _dot(onehot.astype(BF16), tri_ref[...])
    excl = incl - onehot
    r1 = jnp.sum(jnp.where(sel1, excl, 0.0), axis=0, keepdims=True)
    r2 = jnp.sum(jnp.where(sel2, excl, 0.0), axis=0, keepdims=True)
    as_i32 = lambda v: lax.bitcast_convert_type(v, jnp.int32)
    rows = (e1, e2, as_i32(w1), as_i32(w2), r1.astype(jnp.int32), r2.astype(jnp.int32))
    for k, val in enumerate(rows):
        info_ref[k:k + 1, :] = val
    info_ref[len(rows):, :] = jnp.zeros((info_ref.shape[0] - len(rows), tm), jnp.int32)
    cnt_ref[0] = jnp.broadcast_to(incl[:, tm - 1:tm], (N_EXPERTS, LANE)).astype(jnp.int32)


def _topk(logits, tm):
    n = logits.shape[0]
    tri = jnp.asarray(np.triu(np.ones((tm, tm), np.float32)), BF16)
    return pl.pallas_call(
        _topk_body,
        grid=(n // tm,),
        in_specs=[pl.BlockSpec((tm, LANE), lambda i: (i, 0)), _resident(tri)],
        out_specs=[pl.BlockSpec((F32_SUBLANES, tm), lambda i: (0, i)),
                   pl.BlockSpec((1, N_EXPERTS, LANE), lambda i: (i, 0, 0))],
        out_shape=[jax.ShapeDtypeStruct((F32_SUBLANES, n), jnp.int32),
                   jax.ShapeDtypeStruct((n // tm, N_EXPERTS, LANE), jnp.int32)],
        compiler_params=_params("parallel"),
        name="route_topk",
    )(logits, tri)


def _route(logits, tm, tr):
    n = logits.shape[0]
    info, cnt = _topk(logits, tr)
    cnt = cnt[:, :, 0]
    counts = jnp.sum(cnt, axis=0)
    padded = (counts + tm - 1) // tm * tm
    pend = jnp.cumsum(padded)
    pstart = pend - padded
    base = (pstart[None, :] + jnp.cumsum(cnt, axis=0) - cnt).T
    base_tok = jnp.repeat(base, tr, axis=1)
    eidx = jnp.arange(N_EXPERTS, dtype=jnp.int32)[:, None]
    slot = lambda e, r: jnp.sum(jnp.where(eidx == e[None, :], base_tok, 0), axis=0) + r
    dest = jnp.concatenate([slot(info[0], info[4]), slot(info[1], info[5])]).astype(jnp.int32)
    n_rows = (-(-(n * TOP_K + N_EXPERTS * (tm - 1)) // tm) + 1) * tm
    assign = jnp.arange(n * TOP_K, dtype=jnp.int32)
    packed = jnp.stack([assign, jnp.concatenate([info[2], info[3]])], axis=1)
    fill = jnp.broadcast_to(jnp.array([-1, 0], jnp.int32), (n_rows, 2))
    buf = fill.at[dest].set(packed, unique_indices=True, mode="promise_in_bounds")
    a_buf = buf[:, 0]
    w_buf = lax.bitcast_convert_type(buf[:, 1], F32)
    real = a_buf >= 0
    tok_buf = jnp.where(real, a_buf % n, 0)
    spare = TOP_K * n + jnp.arange(n_rows, dtype=jnp.int32) % tm
    sidx = jnp.where(real, a_buf, spare)
    sidx = jnp.concatenate([spare[:tm], sidx[:-tm]])
    tile_start = jnp.arange(n_rows // tm, dtype=jnp.int32) * tm
    tile_e = jnp.minimum(jnp.sum(tile_start[:, None] >= pend[None, :], axis=1), N_EXPERTS - 1)
    n_valid = (pend[-1] // tm).astype(jnp.int32).reshape(1)
    return tok_buf, sidx, w_buf.reshape(-1, 1), tile_e.astype(jnp.int32), n_valid


def _head_slots(w, heads, width):
    k = w.shape[0]
    return jnp.pad(w.reshape(k, heads, width), ((0, 0), (0, 0), (0, LANE - width)))


def _rotate_half_slots(w3):
    half = QK_ROPE_DIM // 2
    lo, mid, hi = QK_NOPE_DIM, QK_NOPE_DIM + half, QK_HEAD_DIM
    out = jnp.zeros_like(w3)
    out = out.at[:, :, lo:mid].set(-w3[:, :, mid:hi])
    out = out.at[:, :, mid:hi].set(w3[:, :, lo:mid])
    return out


def _rope_gains(g):
    half = QK_ROPE_DIM // 2
    lo, mid, hi = QK_NOPE_DIM, QK_NOPE_DIM + half, QK_HEAD_DIM
    main = jnp.pad(g, (0, LANE - QK_HEAD_DIM))
    rot = jnp.zeros((LANE,), F32).at[lo:mid].set(g[mid:hi]).at[mid:hi].set(g[lo:mid])
    return main.reshape(1, LANE), rot.reshape(1, LANE)


def _rope_tables(positions):
    half = QK_ROPE_DIM // 2
    inv_freq = ROPE_BASE ** (-jnp.arange(0, QK_ROPE_DIM, 2, dtype=F32) / QK_ROPE_DIM)
    ang = positions.astype(F32).reshape(-1, 1) * inv_freq[None, :]
    n = ang.shape[0]
    dense = ang.reshape(n * half // LANE, LANE)
    cos_h = jnp.cos(dense).reshape(n, half)
    sin_h = jnp.sin(dense).reshape(n, half)
    pad = jnp.zeros((n, LANE - QK_HEAD_DIM), F32)
    cos = jnp.concatenate([jnp.ones((n, QK_NOPE_DIM), F32), cos_h, cos_h, pad], axis=-1)
    sin = jnp.concatenate([jnp.zeros((n, QK_NOPE_DIM), F32), sin_h, sin_h, pad], axis=-1)
    return cos, sin


def _block_diag(w):
    h, a, b = w.shape
    eye = jnp.eye(h, dtype=w.dtype)
    return (eye[:, None, :, None] * w[:, :, None, :]).reshape(h * a, h * b)


def kernel(x, positions, norm1_g, w_in, q_norm_g, w_uq, kv_norm_g, w_ukv, qk_q_g, qk_k_g, w_up_attn, conv_w, conv_b, w_rg, b_rg, w_ig, b_ig, lru_lambda, w_up_lru, w_o, norm2_g, ffn_w_gate, ffn_w_up, ffn_w_down, moe_router, moe_w_gate, moe_w_up, moe_w_down):
    batch, seq, d = x.shape
    n = batch * seq
    depth = norm1_g.shape[0]
    q_rank = q_norm_g.shape[1]
    kv_rank = kv_norm_g.shape[1]
    c_lru = conv_w.shape[2]
    head_w = d - 0
    lat = q_rank + kv_rank + QK_ROPE_DIM
    seg_w = -(-lat // LANE) * LANE
    widths = (seg_w, c_lru, c_lru, head_w, head_w)

    tm = 512
    tq = 2048
    t_lru = 128
    tm_moe = 512

    cos, sin = _rope_tables(positions)
    cos_t, sin_t = cos.T, sin.T
    xf = x.reshape(n, d)
    for layer in range(depth):
        w_in_l = w_in[layer]
        w_in_p = jnp.concatenate(
            [w_in_l[:, :lat], jnp.zeros((d, seg_w - lat), F32), w_in_l[:, lat:]], axis=1).astype(BF16)
        seg, xl, gl, ga, gb = _in_proj(xf, norm1_g[layer].reshape(1, d), w_in_p, widths, 2 * tm)

        wq3 = _head_slots(w_uq[layer], MLA_HEADS, QK_HEAD_DIM)
        wkv3 = w_ukv[layer].reshape(kv_rank, MLA_HEADS, QK_NOPE_DIM + V_HEAD_DIM)
        ck_w = seg_w - q_rank
        wk3 = jnp.zeros((ck_w, MLA_HEADS, LANE), F32)
        wk3 = wk3.at[:kv_rank, :, :QK_NOPE_DIM].set(wkv3[:, :, :QK_NOPE_DIM])
        eye = jnp.eye(QK_ROPE_DIM, dtype=F32)
        wk3 = wk3.at[kv_rank:kv_rank + QK_ROPE_DIM, :, QK_NOPE_DIM:QK_HEAD_DIM].set(
            jnp.broadcast_to(eye[:, None, :], (QK_ROPE_DIM, MLA_HEADS, QK_ROPE_DIM)))
        wv = jnp.zeros((ck_w, MLA_HEADS * V_HEAD_DIM), F32).at[:kv_rank].set(
            wkv3[:, :, QK_NOPE_DIM:].reshape(kv_rank, MLA_HEADS * V_HEAD_DIM))
        flat = lambda w3: w3.reshape(w3.shape[0], MLA_HEADS * LANE).astype(BF16)
        gqm, gqr = _rope_gains(qk_q_g[layer])
        gkm, gkr = (g.reshape(LANE, 1) for g in _rope_gains(qk_k_g[layer]))
        q, kt, v = _qkv(seg, cos, sin, cos_t, sin_t, q_norm_g[layer].reshape(1, -1),
                        kv_norm_g[layer].reshape(1, -1), flat(wq3), flat(_rotate_half_slots(wq3)),
                        flat(wk3).T, flat(_rotate_half_slots(wk3)).T, wv.astype(BF16), gqm, gqr, gkm, gkr, tm, seq)
        o = _attention(q.reshape(batch, seq, -1), kt, v.reshape(batch, seq, -1), tq)

        wgate = jnp.concatenate([_block_diag(w_rg[layer]), _block_diag(w_ig[layer])], axis=1).astype(BF16)
        bgate = jnp.concatenate([b_rg[layer], b_ig[layer]]).reshape(1, -1)
        y = _lru(xl.reshape(batch, seq, c_lru), gl.reshape(batch, seq, c_lru), conv_w[layer],
                 conv_b[layer].reshape(1, -1), wgate, bgate, lru_lambda[layer].reshape(1, -1), t_lru)

        j = layer // 2
        is_moe = layer % 2 == 1
        merge_args = (xf, o.reshape(n, -1), y.reshape(n, -1), ga, gb, w_up_attn[layer].astype(BF16),
                      w_up_lru[layer].astype(BF16), w_o[layer].astype(BF16), norm2_g[layer].reshape(1, d), tm)
        if not is_moe:
            (xf,) = _merge(*merge_args, ffn=(ffn_w_gate[j].astype(BF16), ffn_w_up[j].astype(BF16),
                                             ffn_w_down[j].astype(BF16)))
        else:
            r_pad = jnp.pad(moe_router[j], ((0, 0), (0, LANE - N_EXPERTS)))
            r_hi = r_pad.astype(BF16)
            r_lo = (r_pad - r_hi.astype(F32)).astype(BF16)
            rcat = jnp.concatenate([jnp.concatenate([r_hi, r_lo], axis=1),
                                    jnp.concatenate([r_hi, jnp.zeros_like(r_lo)], axis=1)], axis=0)
            xn, h2, logits = _merge(*merge_args, router=(rcat,))
            tok_buf, sidx, w_rows, tile_e, n_valid = _route(logits, tm_moe, tm)
            y2 = _moe_ffn(tile_e, n_valid, tok_buf, sidx, h2, w_rows, moe_w_gate[j].astype(BF16),
                          moe_w_up[j].astype(BF16), moe_w_down[j].astype(BF16), TOP_K * n + tm_moe, tm_moe)
            xf = _moe_add(xn, y2, tm)
    return xf.reshape(batch, seq, d)
```

```python
import functools

import numpy as np
import jax
import jax.numpy as jnp
from jax import lax
from jax.experimental import pallas as pl
from jax.experimental.pallas import tpu as pltpu

BF16 = jnp.bfloat16
F32 = jnp.float32

MLA_HEADS = 8
QK_NOPE_DIM = 64
QK_ROPE_DIM = 32
V_HEAD_DIM = 64
QK_HEAD_DIM = QK_NOPE_DIM + QK_ROPE_DIM
ROPE_BASE = 10000.0
LRU_HEADS = 8
CONV_WIDTH = 4
LRU_C = 8.0
N_EXPERTS = 8
TOP_K = 2
RMS_EPS = 1e-6

LANE = 128
VMEM_LIMIT = 56 * 1024 * 1024


def _params(*sem):
    return pltpu.CompilerParams(dimension_semantics=sem, vmem_limit_bytes=VMEM_LIMIT)


def _rms(x, g):
    ms = jnp.mean(x * x, axis=-1, keepdims=True)
    return x * lax.rsqrt(ms + RMS_EPS) * g


def _dot(a, b):
    return jnp.dot(a, b, preferred_element_type=F32)


def _in_qkv_body(x_ref, g_ref, w_ref, c_ref, s_ref, ct_ref, st_ref, gq_ref, gkv_ref, wq_ref, wqr_ref, wkt_ref,
                 wkrt_ref, wv_ref, gqm_ref, gqr_ref, gkm_ref, gkr_ref, xl_ref, gl_ref, ga_ref, gb_ref,
                 q_ref, kt_ref, v_ref, *, widths, q_rank, kv_rank):
    hn = _rms(x_ref[...], g_ref[...]).astype(BF16)
    o = np.cumsum((0,) + widths)
    seg = _dot(hn, w_ref[:, o[0]:o[1]])
    xl_ref[...] = _dot(hn, w_ref[:, o[1]:o[2]])
    gl_ref[...] = jax.nn.gelu(_dot(hn, w_ref[:, o[2]:o[3]]), approximate=True).astype(BF16)
    ga_ref[...] = jax.nn.sigmoid(_dot(hn, w_ref[:, o[3]:o[4]])).astype(BF16)
    gb_ref[...] = jax.nn.sigmoid(_dot(hn, w_ref[:, o[4]:o[5]])).astype(BF16)

    cqn = _rms(seg[:, 0:q_rank], gq_ref[...]).astype(BF16)
    ckvn = _rms(seg[:, q_rank:q_rank + kv_rank], gkv_ref[...])
    ck = jnp.concatenate([ckvn, seg[:, q_rank + kv_rank:]], axis=-1).astype(BF16)
    scale = QK_HEAD_DIM ** -0.5 * np.log2(np.e)
    inv_d = 1.0 / QK_HEAD_DIM

    a_all = _dot(cqn, wq_ref[...])
    b_all = _dot(cqn, wqr_ref[...])
    cm = gqm_ref[...] * c_ref[...] * scale
    sm = gqr_ref[...] * s_ref[...] * scale
    for h in range(MLA_HEADS):
        a = a_all[:, h * LANE:(h + 1) * LANE]
        b = b_all[:, h * LANE:(h + 1) * LANE]
        inv = lax.rsqrt(jnp.sum(a * a, axis=-1, keepdims=True) * inv_d + RMS_EPS)
        q_ref[:, h * LANE:(h + 1) * LANE] = (inv * (a * cm + b * sm)).astype(BF16)

    nt = (((1,), (1,)), ((), ()))
    at_all = lax.dot_general(wkt_ref[...], ck, nt, preferred_element_type=F32)
    bt_all = lax.dot_general(wkrt_ref[...], ck, nt, preferred_element_type=F32)
    cmt = gkm_ref[...] * ct_ref[...]
    smt = gkr_ref[...] * st_ref[...]
    for h in range(MLA_HEADS):
        a = at_all[h * LANE:(h + 1) * LANE, :]
        b = bt_all[h * LANE:(h + 1) * LANE, :]
        inv = lax.rsqrt(jnp.sum(a * a, axis=0, keepdims=True) * inv_d + RMS_EPS)
        kt_ref[0, h * LANE:(h + 1) * LANE, :] = (inv * (a * cmt + b * smt)).astype(BF16)
    v_ref[...] = _dot(ck, wv_ref[...]).astype(BF16)


def _in_qkv(x, g, w, widths, cos, sin, cos_t, sin_t, gq, gkv, wq, wqr, wkt, wkrt, wv, gqm, gqr, gkm, gkr, tm, seq):
    n, d = x.shape
    q_rank = gq.shape[1]
    kv_rank = gkv.shape[1]
    hq = wq.shape[1]
    hv = wv.shape[1]
    per_seq = seq // tm
    row = lambda wd: pl.BlockSpec((tm, wd), lambda i: (i, 0))
    col = pl.BlockSpec((LANE, tm), lambda i: (0, i))
    consts = (gq, gkv, wq, wqr, wkt, wkrt, wv, gqm, gqr, gkm, gkr)
    out_w = widths[1:] + (hq,)
    out_dt = (F32, BF16, BF16, BF16, BF16)
    return pl.pallas_call(
        functools.partial(_in_qkv_body, widths=widths, q_rank=q_rank, kv_rank=kv_rank),
        grid=(n // tm,),
        in_specs=[row(d), _resident(g), _resident(w), row(LANE), row(LANE), col, col]
        + [_resident(a) for a in consts],
        out_specs=[row(wd) for wd in out_w]
        + [pl.BlockSpec((1, hq, tm), lambda i: (i // per_seq, 0, i % per_seq)), row(hv)],
        out_shape=[jax.ShapeDtypeStruct((n, wd), dt) for wd, dt in zip(out_w, out_dt)]
        + [jax.ShapeDtypeStruct((n // seq, hq, seq), BF16), jax.ShapeDtypeStruct((n, hv), BF16)],
        compiler_params=_params("parallel"),
        name="in_qkv",
    )(x, g, w, cos, sin, cos_t, sin_t, *consts)


ATTN_ROW_CHUNK = 256


def _attn_body(q_ref, kt_ref, v_ref, hm_ref, o_ref, m_s, l_s, acc_s, *, tq, heads):
    i = pl.program_id(2)
    rc = ATTN_ROW_CHUNK
    n_c = tq // rc
    m_s[...] = jnp.full(m_s.shape, -jnp.inf, F32)
    l_s[...] = jnp.zeros(l_s.shape, F32)
    acc_s[...] = jnp.zeros(acc_s.shape, F32)
    lane = lax.broadcasted_iota(jnp.int32, (rc, LANE), 1)
    tri = lax.broadcasted_iota(jnp.int32, (rc, rc), 1) <= lax.broadcasted_iota(jnp.int32, (rc, rc), 0)

    def per_head(vals):
        out = vals[heads - 1]
        for hh in range(heads - 2, -1, -1):
            out = jnp.where(lane < (hh + 1) * V_HEAD_DIM, vals[hh], out)
        return out

    def load_kv(start, ncols):
        kts = [kt_ref[0, hh * LANE:(hh + 1) * LANE, pl.ds(start, ncols)] for hh in range(heads)]
        vs = v_ref[0, pl.ds(start, ncols), :]
        parts = []
        for hh in range(heads):
            ind = jnp.broadcast_to(hm_ref[hh], vs.shape)
            parts.append(jnp.concatenate([vs * ind, ind], axis=1))
        return kts, jnp.concatenate(parts, axis=0)

    def chunk_update(c, kts, vcat, ncols, masked):
        rows = pl.ds(c * rc, rc)
        ps, alphas = [], []
        for hh in range(heads):
            s = _dot(q_ref[0, rows, hh * LANE:(hh + 1) * LANE], kts[hh])
            if masked:
                tail = jnp.where(tri, s[:, ncols - rc:], -jnp.inf)
                s = tail if ncols == rc else jnp.concatenate([s[:, :ncols - rc], tail], axis=1)
            m_old = m_s[hh, rows, :]
            m_new = jnp.maximum(m_old, jnp.max(s, axis=-1, keepdims=True))
            ps.append(jnp.exp2(s - jnp.concatenate([m_new] * (ncols // LANE), axis=1)).astype(BF16))
            alphas.append(jnp.exp2(m_old - m_new))
            m_s[hh, rows, :] = m_new
        pv = _dot(jnp.concatenate(ps, axis=1), vcat)
        alpha = per_head(alphas)
        acc_s[rows, :] = alpha * acc_s[rows, :] + pv[:, :LANE]
        l_s[rows, :] = alpha * l_s[rows, :] + pv[:, LANE:]

    def kv_step(j, carry):
        kts, vcat = load_kv(pl.multiple_of(j * tq, tq), tq)
        for c in range(n_c):
            chunk_update(c, kts, vcat, tq, masked=False)
        return carry

    lax.fori_loop(0, i, kv_step, 0)
    start = pl.multiple_of(i * tq, tq)
    for c in range(n_c):
        ncols = (c + 1) * rc
        kts, vcat = load_kv(start, ncols)
        chunk_update(c, kts, vcat, ncols, masked=True)
    o_ref[0] = (acc_s[...] / l_s[...]).astype(BF16)


def _attention(q, kt, v, tq):
    b, s, hq = q.shape
    heads = LANE // V_HEAD_DIM
    n_hp = hq // (heads * LANE)
    head_mask = jnp.asarray(np.arange(LANE)[None, None, :] // V_HEAD_DIM == np.arange(heads)[:, None, None], BF16)
    return pl.pallas_call(
        functools.partial(_attn_body, tq=tq, heads=heads),
        grid=(b, n_hp, s // tq),
        in_specs=[pl.BlockSpec((1, tq, heads * LANE), lambda bi, hp, i: (bi, i, hp)),
                  pl.BlockSpec((1, heads * LANE, s), lambda bi, hp, i: (bi, hp, 0)),
                  pl.BlockSpec((1, s, LANE), lambda bi, hp, i: (bi, 0, hp)),
                  pl.BlockSpec((heads, 1, LANE), lambda bi, hp, i: (0, 0, 0))],
        out_specs=pl.BlockSpec((1, tq, LANE), lambda bi, hp, i: (bi, i, hp)),
        out_shape=jax.ShapeDtypeStruct((b, s, v.shape[2]), BF16),
        scratch_shapes=[pltpu.VMEM((heads, tq, LANE), F32), pltpu.VMEM((tq, LANE), F32),
                        pltpu.VMEM((tq, LANE), F32)],
        compiler_params=_params("parallel", "parallel", "arbitrary"),
        name="attention",
    )(q, kt, v, head_mask)


F32_SUBLANES = 8


def _lru_body(x_ref, gl_ref, cw_ref, cb_ref, wg_ref, bg_ref, lam_ref, y_ref, ext_s, h_s, *, nb, t):
    c = x_ref.shape[2]
    sub = F32_SUBLANES

    @pl.when(pl.program_id(0) == 0)
    def _():
        ext_s[:, 0:sub, :] = jnp.zeros((nb, sub, c), F32)
        h_s[...] = jnp.zeros_like(h_s)

    ext_s[:, sub:sub + t, :] = x_ref[...]
    xc = cb_ref[...][None]
    for kk in range(CONV_WIDTH):
        off = sub - (CONV_WIDTH - 1) + kk
        xc = xc + ext_s[:, off:off + t, :] * cw_ref[kk:kk + 1, :][None]
    ext_s[:, 0:sub, :] = ext_s[:, t:t + sub, :]

    xc2 = xc.reshape(nb * t, c)
    g = _dot(xc2.astype(BF16), wg_ref[...]) + bg_ref[...]
    r = jax.nn.sigmoid(g[:, :c])
    ig = jax.nn.sigmoid(g[:, c:])
    log_a = -LRU_C * r * jax.nn.softplus(-lam_ref[...])
    a = jnp.exp(log_a)
    mult = jnp.sqrt(-jnp.tanh(log_a) * (a * a + 1.0))
    bv = mult * ig * xc2

    n_g = t // sub
    a3 = a.reshape(nb * n_g, sub, c)
    b3 = bv.reshape(nb * n_g, sub, c)
    row = lax.broadcasted_iota(jnp.int32, (1, sub, c), 1)
    d = 1
    while d < sub:
        keep = row >= d
        b3 = jnp.where(keep, a3 * pltpu.roll(b3, d, axis=1) + b3, b3)
        a3 = jnp.where(keep, a3 * pltpu.roll(a3, d, axis=1), a3)
        d *= 2
    a4 = a3.reshape(nb, n_g, sub, c)
    b4 = b3.reshape(nb, n_g, sub, c)
    h = h_s[...]
    groups = []
    for g in range(n_g):
        hg = a4[:, g] * h + b4[:, g]
        groups.append(hg)
        h = hg[:, sub - 1:sub, :]
    h_s[...] = h
    hseq = jnp.stack(groups, axis=1).reshape(nb, t, c)
    y_ref[...] = (hseq * gl_ref[...].astype(F32)).astype(BF16)


def _lru(xl, gl, cw, cb, wg, bg, lam, t):
    nb, s, c = xl.shape
    blk = pl.BlockSpec((nb, t, c), lambda i: (0, i, 0))
    full = lambda a: pl.BlockSpec(a.shape, lambda i: (0, 0))
    return pl.pallas_call(
        functools.partial(_lru_body, nb=nb, t=t),
        grid=(s // t,),
        in_specs=[blk, blk, full(cw), full(cb), full(wg), full(bg), full(lam)],
        out_specs=blk,
        out_shape=jax.ShapeDtypeStruct((nb, s, c), BF16),
        scratch_shapes=[pltpu.VMEM((nb, t + F32_SUBLANES, c), F32), pltpu.VMEM((nb, 1, c), F32)],
        compiler_params=_params("arbitrary"),
        name="rg_lru",
    )(xl, gl, cw, cb, wg, bg, lam)


def _merge_body(x_ref, o_ref, y_ref, ga_ref, gb_ref, wa_ref, wb_ref, wo_ref, g2_ref, *rest, router):
    ua = _dot(o_ref[...], wa_ref[...])
    ub = _dot(y_ref[...], wb_ref[...])
    merged = ga_ref[...].astype(F32) * ua + gb_ref[...].astype(F32) * ub
    xn = x_ref[...] + _dot(merged.astype(BF16), wo_ref[...])
    h2 = _rms(xn, g2_ref[...])
    if router:
        rcat_ref, xn_ref, h2_ref, lg_ref = rest
        hi = h2.astype(BF16)
        lo = (h2 - hi.astype(F32)).astype(BF16)
        prod = _dot(jnp.concatenate([hi, lo], axis=1), rcat_ref[...])
        lg_ref[...] = prod[:, :LANE] + prod[:, LANE:]
        h2_ref[...] = h2
        xn_ref[...] = xn
    else:
        wg_ref, wu_ref, wd_ref, out_ref = rest
        hb = h2.astype(BF16)
        act = (jax.nn.silu(_dot(hb, wg_ref[...])) * _dot(hb, wu_ref[...])).astype(BF16)
        out_ref[...] = xn + _dot(act, wd_ref[...])


def _merge(x, o, y, ga, gb, wa, wb, wo, g2, tm, router=None, ffn=None):
    n, d = x.shape
    row = lambda a: pl.BlockSpec((tm, a.shape[1]), lambda i: (i, 0))
    ins = [x, o, y, ga, gb]
    consts = [wa, wb, wo, g2] + list(router or ffn)
    if router:
        out_shape = [jax.ShapeDtypeStruct((n, d), F32), jax.ShapeDtypeStruct((n, d), F32),
                     jax.ShapeDtypeStruct((n, LANE), F32)]
    else:
        out_shape = [jax.ShapeDtypeStruct((n, d), F32)]
    return pl.pallas_call(
        functools.partial(_merge_body, router=bool(router)),
        grid=(n // tm,),
        in_specs=[row(a) for a in ins] + [_resident(a) for a in consts],
        out_specs=[pl.BlockSpec((tm, sh.shape[1]), lambda i: (i, 0)) for sh in out_shape],
        out_shape=out_shape,
        compiler_params=_params("parallel"),
        name="merge_router" if router else "merge_ffn",
    )(*ins, *consts)


def _resident(a):
    return pl.BlockSpec(a.shape, lambda *_: (0,) * a.ndim, pipeline_mode=pl.Buffered(1))


def _row_copy(src_hbm, dst, src_row, dst_row, sem):
    return pltpu.make_async_copy(src_hbm.at[pl.ds(src_row, 1), :], dst.at[pl.ds(dst_row, 1), :], sem)


MOE_FF_STEPS = 2


def _moe_ffn_body(te_ref, nv_ref, tok_ref, tokn_ref, sidx_ref, w_ref, h_hbm, wg_ref, wu_ref, wd_ref, y_hbm,
                  xbuf, xb16, acc_ref, obuf, gsem, ssem, *, tm):
    i = pl.program_id(0)
    j = pl.program_id(1)
    n_i = pl.num_programs(0)
    slot = i % 2

    def wait_gather(s):
        pltpu.make_async_copy(h_hbm.at[pl.ds(0, tm), :], xbuf.at[s], gsem.at[s]).wait()

    def wait_scatter(s):
        pltpu.make_async_copy(obuf.at[s], y_hbm.at[pl.ds(0, tm), :], ssem.at[s]).wait()

    @pl.when((i == 0) & (j == 0))
    def _():
        obuf[...] = jnp.zeros_like(obuf)

        def issue(r, carry):
            _row_copy(h_hbm, xbuf.at[0], tok_ref[r], r, gsem.at[0]).start()
            return carry

        lax.fori_loop(0, tm, issue, 0, unroll=8)

    @pl.when(j == 0)
    def _():
        wait_gather(slot)
        xb16[...] = xbuf[slot].astype(BF16)

    def gather_next():
        for r in range(tm):
            _row_copy(h_hbm, xbuf.at[1 - slot], tokn_ref[r], r, gsem.at[1 - slot]).start()

    def scatter_prev():
        for r in range(tm):
            _row_copy(obuf.at[1 - slot], y_hbm, r, sidx_ref[r], ssem.at[1 - slot]).start()

    def compute(first):
        h = xb16[...]
        act = (jax.nn.silu(_dot(h, wg_ref[0])) * _dot(h, wu_ref[0])).astype(BF16)
        part = _dot(act, wd_ref[0])
        if first:
            acc_ref[...] = part
        else:
            obuf[slot] = (acc_ref[...] + part) * w_ref[...]

    valid = i < nv_ref[0]

    @pl.when(valid & (j == 0))
    def _():
        gather_next()
        compute(first=True)

    @pl.when(jnp.logical_not(valid) & (j == 0))
    def _():
        gather_next()

    @pl.when((j == 1) & (i >= 1))
    def _():
        wait_scatter(slot)

    @pl.when(valid & (j == 1))
    def _():
        scatter_prev()
        compute(first=False)

    @pl.when(jnp.logical_not(valid) & (j == 1) & (i < n_i - 1))
    def _():
        scatter_prev()

    @pl.when((i == n_i - 1) & (j == 1))
    def _():
        scatter_prev()
        wait_scatter(1 - slot)
        wait_gather(1 - slot)


def _moe_ffn(tile_e, n_valid, tok, sidx, h, w_rows, wg, wu, wd, y_rows, tm):
    d = h.shape[1]
    f = wg.shape[2]
    tf = f // MOE_FF_STEPS
    n_grid = tok.shape[0] // tm
    nxt = lambda i, j, te, nv: (jnp.minimum(i + 1, n_grid - 1),)
    grid_spec = pltpu.PrefetchScalarGridSpec(
        num_scalar_prefetch=2,
        grid=(n_grid, MOE_FF_STEPS),
        in_specs=[pl.BlockSpec((tm,), lambda i, j, te, nv: (i,), memory_space=pltpu.SMEM),
                  pl.BlockSpec((tm,), nxt, memory_space=pltpu.SMEM),
                  pl.BlockSpec((tm,), lambda i, j, te, nv: (i,), memory_space=pltpu.SMEM),
                  pl.BlockSpec((tm, 1), lambda i, j, te, nv: (i, 0)),
                  pl.BlockSpec(memory_space=pl.ANY),
                  pl.BlockSpec((1, d, tf), lambda i, j, te, nv: (te[i], 0, j)),
                  pl.BlockSpec((1, d, tf), lambda i, j, te, nv: (te[i], 0, j)),
                  pl.BlockSpec((1, tf, d), lambda i, j, te, nv: (te[i], j, 0))],
        out_specs=pl.BlockSpec(memory_space=pl.ANY),
        scratch_shapes=[pltpu.VMEM((2, tm, d), F32), pltpu.VMEM((tm, d), BF16), pltpu.VMEM((tm, d), F32),
                        pltpu.VMEM((2, tm, d), F32), pltpu.SemaphoreType.DMA((2,)),
                        pltpu.SemaphoreType.DMA((2,))],
    )
    return pl.pallas_call(
        functools.partial(_moe_ffn_body, tm=tm),
        grid_spec=grid_spec,
        out_shape=jax.ShapeDtypeStruct((y_rows, d), F32),
        compiler_params=_params("arbitrary", "arbitrary"),
        name="moe_ffn",
    )(tile_e, n_valid, tok, tok, sidx, w_rows, h, wg, wu, wd)


def _moe_add_body(x_ref, ya_ref, yb_ref, o_ref):
    o_ref[...] = x_ref[...] + ya_ref[...] + yb_ref[...]


def _moe_add(x, y2, tm):
    n, d = x.shape
    return pl.pallas_call(
        _moe_add_body,
        grid=(n // tm,),
        in_specs=[pl.BlockSpec((tm, d), lambda i: (i, 0)),
                  pl.BlockSpec((tm, d), lambda i: (i, 0)),
                  pl.BlockSpec((tm, d), lambda i: (i + n // tm, 0))],
        out_specs=pl.BlockSpec((tm, d), lambda i: (i, 0)),
        out_shape=jax.ShapeDtypeStruct((n, d), F32),
        compiler_params=_params("parallel"),
        name="moe_add",
    )(x, y2, y2)


def _topk_body(lg_ref, tri_ref, info_ref, cnt_ref):
    tm = lg_ref.shape[0]
    l8 = lg_ref[...].T[0:N_EXPERTS, :]
    eidx = lax.broadcasted_iota(jnp.int32, l8.shape, 0)
    m1 = jnp.max(l8, axis=0, keepdims=True)
    e1 = jnp.min(jnp.where(l8 == m1, eidx, N_EXPERTS), axis=0, keepdims=True)
    sel1 = eidx == e1
    rest = jnp.where(sel1, -jnp.inf, l8)
    m2 = jnp.max(rest, axis=0, keepdims=True)
    e2 = jnp.min(jnp.where(rest == m2, eidx, N_EXPERTS), axis=0, keepdims=True)
    sel2 = eidx == e2
    ex = jnp.exp(m2 - m1)
    w1 = 1.0 / (1.0 + ex)
    w2 = ex / (1.0 + ex)
    onehot = jnp.where(sel1 | sel2, 1.0, 0.0)
    incl = _dot(onehot.astype(BF16), tri_ref[...])
    excl = incl - onehot
    r1 = jnp.sum(jnp.where(sel1, excl, 0.0), axis=0, keepdims=True)
    r2 = jnp.sum(jnp.where(sel2, excl, 0.0), axis=0, keepdims=True)
    as_i32 = lambda v: lax.bitcast_convert_type(v, jnp.int32)
    rows = (e1, e2, as_i32(w1), as_i32(w2), r1.astype(jnp.int32), r2.astype(jnp.int32))
    for k, val in enumerate(rows):
        info_ref[k:k + 1, :] = val
    info_ref[len(rows):, :] = jnp.zeros((info_ref.shape[0] - len(rows), tm), jnp.int32)
    cnt_ref[0] = jnp.broadcast_to(incl[:, tm - 1:tm], (N_EXPERTS, LANE)).astype(jnp.int32)


def _topk(logits, tm):
    n = logits.shape[0]
    tri = jnp.asarray(np.triu(np.ones((tm, tm), np.float32)), BF16)
    return pl.pallas_call(
        _topk_body,
        grid=(n // tm,),
        in_specs=[pl.BlockSpec((tm, LANE), lambda i: (i, 0)), _resident(tri)],
        out_specs=[pl.BlockSpec((F32_SUBLANES, tm), lambda i: (0, i)),
                   pl.BlockSpec((1, N_EXPERTS, LANE), lambda i: (i, 0, 0))],
        out_shape=[jax.ShapeDtypeStruct((F32_SUBLANES, n), jnp.int32),
                   jax.ShapeDtypeStruct((n // tm, N_EXPERTS, LANE), jnp.int32)],
        compiler_params=_params("parallel"),
        name="route_topk",
    )(logits, tri)


def _route(logits, tm, tr):
    n = logits.shape[0]
    info, cnt = _topk(logits, tr)
    cnt = cnt[:, :, 0]
    counts = jnp.sum(cnt, axis=0)
    padded = (counts + tm - 1) // tm * tm
    pend = jnp.cumsum(padded)
    pstart = pend - padded
    base = (pstart[None, :] + jnp.cumsum(cnt, axis=0) - cnt).T
    base_tok = jnp.repeat(base, tr, axis=1)
    eidx = jnp.arange(N_EXPERTS, dtype=jnp.int32)[:, None]
    slot = lambda e, r: jnp.sum(jnp.where(eidx == e[None, :], base_tok, 0), axis=0) + r
    dest = jnp.concatenate([slot(info[0], info[4]), slot(info[1], info[5])]).astype(jnp.int32)
    n_rows = (-(-(n * TOP_K + N_EXPERTS * (tm - 1)) // tm) + 1) * tm
    assign = jnp.arange(n * TOP_K, dtype=jnp.int32)
    packed = jnp.stack([assign, jnp.concatenate([info[2], info[3]])], axis=1)
    fill = jnp.broadcast_to(jnp.array([-1, 0], jnp.int32), (n_rows, 2))
    buf = fill.at[dest].set(packed, unique_indices=True, mode="promise_in_bounds")
    a_buf = buf[:, 0]
    w_buf = lax.bitcast_convert_type(buf[:, 1], F32)
    real = a_buf >= 0
    tok_buf = jnp.where(real, a_buf % n, 0)
    spare = TOP_K * n + jnp.arange(n_rows, dtype=jnp.int32) % tm
    sidx = jnp.where(real, a_buf, spare)
    sidx = jnp.concatenate([spare[:tm], sidx[:-tm]])
    tile_start = jnp.arange(n_rows // tm, dtype=jnp.int32) * tm
    tile_e = jnp.minimum(jnp.sum(tile_start[:, None] >= pend[None, :], axis=1), N_EXPERTS - 1)
    n_valid = (pend[-1] // tm).astype(jnp.int32).reshape(1)
    return tok_buf, sidx, w_buf.reshape(-1, 1), tile_e.astype(jnp.int32), n_valid


def _head_slots(w, heads, width):
    k = w.shape[0]
    return jnp.pad(w.reshape(k, heads, width), ((0, 0), (0, 0), (0, LANE - width)))


def _rotate_half_slots(w3):
    half = QK_ROPE_DIM // 2
    lo, mid, hi = QK_NOPE_DIM, QK_NOPE_DIM + half, QK_HEAD_DIM
    out = jnp.zeros_like(w3)
    out = out.at[:, :, lo:mid].set(-w3[:, :, mid:hi])
    out = out.at[:, :, mid:hi].set(w3[:, :, lo:mid])
    return out


def _rope_gains(g):
    half = QK_ROPE_DIM // 2
    lo, mid, hi = QK_NOPE_DIM, QK_NOPE_DIM + half, QK_HEAD_DIM
    main = jnp.pad(g, (0, LANE - QK_HEAD_DIM))
    rot = jnp.zeros((LANE,), F32).at[lo:mid].set(g[mid:hi]).at[mid:hi].set(g[lo:mid])
    return main.reshape(1, LANE), rot.reshape(1, LANE)


def _rope_tables(positions):
    half = QK_ROPE_DIM // 2
    inv_freq = ROPE_BASE ** (-jnp.arange(0, QK_ROPE_DIM, 2, dtype=F32) / QK_ROPE_DIM)
    ang = positions.astype(F32).reshape(-1, 1) * inv_freq[None, :]
    n = ang.shape[0]
    dense = ang.reshape(n * half // LANE, LANE)
    cos_h = jnp.cos(dense).reshape(n, half)
    sin_h = jnp.sin(dense).reshape(n, half)
    pad = jnp.zeros((n, LANE - QK_HEAD_DIM), F32)
    cos = jnp.concatenate([jnp.ones((n, QK_NOPE_DIM), F32), cos_h, cos_h, pad], axis=-1)
    sin = jnp.concatenate([jnp.zeros((n, QK_NOPE_DIM), F32), sin_h, sin_h, pad], axis=-1)
    return cos, sin


def _block_diag(w):
    h, a, b = w.shape
    eye = jnp.eye(h, dtype=w.dtype)
    return (eye[:, None, :, None] * w[:, :, None, :]).reshape(h * a, h * b)


def kernel(x, positions, norm1_g, w_in, q_norm_g, w_uq, kv_norm_g, w_ukv, qk_q_g, qk_k_g, w_up_attn, conv_w, conv_b, w_rg, b_rg, w_ig, b_ig, lru_lambda, w_up_lru, w_o, norm2_g, ffn_w_gate, ffn_w_up, ffn_w_down, moe_router, moe_w_gate, moe_w_up, moe_w_down):
    batch, seq, d = x.shape
    n = batch * seq
    depth = norm1_g.shape[0]
    q_rank = q_norm_g.shape[1]
    kv_rank = kv_norm_g.shape[1]
    c_lru = conv_w.shape[2]
    head_w = d - 0
    lat = q_rank + kv_rank + QK_ROPE_DIM
    seg_w = -(-lat // LANE) * LANE
    widths = (seg_w, c_lru, c_lru, head_w, head_w)

    tm = 512
    tq = 2048
    t_lru = 128
    tm_moe = 512

    cos, sin = _rope_tables(positions)
    cos_t, sin_t = cos.T, sin.T
    xf = x.reshape(n, d)
    for layer in range(depth):
        w_in_l = w_in[layer]
        w_in_p = jnp.concatenate(
            [w_in_l[:, :lat], jnp.zeros((d, seg_w - lat), F32), w_in_l[:, lat:]], axis=1).astype(BF16)
        wq3 = _head_slots(w_uq[layer], MLA_HEADS, QK_HEAD_DIM)
        wkv3 = w_ukv[layer].reshape(kv_rank, MLA_HEADS, QK_NOPE_DIM + V_HEAD_DIM)
        ck_w = seg_w - q_rank
        wk3 = jnp.zeros((ck_w, MLA_HEADS, LANE), F32)
        wk3 = wk3.at[:kv_rank, :, :QK_NOPE_DIM].set(wkv3[:, :, :QK_NOPE_DIM])
        eye = jnp.eye(QK_ROPE_DIM, dtype=F32)
        wk3 = wk3.at[kv_rank:kv_rank + QK_ROPE_DIM, :, QK_NOPE_DIM:QK_HEAD_DIM].set(
            jnp.broadcast_to(eye[:, None, :], (QK_ROPE_DIM, MLA_HEADS, QK_ROPE_DIM)))
        wv = jnp.zeros((ck_w, MLA_HEADS * V_HEAD_DIM), F32).at[:kv_rank].set(
            wkv3[:, :, QK_NOPE_DIM:].reshape(kv_rank, MLA_HEADS * V_HEAD_DIM))
        flat = lambda w3: w3.reshape(w3.shape[0], MLA_HEADS * LANE).astype(BF16)
        gqm, gqr = _rope_gains(qk_q_g[layer])
        gkm, gkr = (g.reshape(LANE, 1) for g in _rope_gains(qk_k_g[layer]))
        xl, gl, ga, gb, q, kt, v = _in_qkv(
            xf, norm1_g[layer].reshape(1, d), w_in_p, widths, cos, sin, cos_t, sin_t,
            q_norm_g[layer].reshape(1, -1), kv_norm_g[layer].reshape(1, -1), flat(wq3),
            flat(_rotate_half_slots(wq3)), flat(wk3).T, flat(_rotate_half_slots(wk3)).T, wv.astype(BF16),
            gqm, gqr, gkm, gkr, tm, seq)
        o = _attention(q.reshape(batch, seq, -1), kt, v.reshape(batch, seq, -1), tq)

        wgate = jnp.concatenate([_block_diag(w_rg[layer]), _block_diag(w_ig[layer])], axis=1).astype(BF16)
        bgate = jnp.concatenate([b_rg[layer], b_ig[layer]]).reshape(1, -1)
        y = _lru(xl.reshape(batch, seq, c_lru), gl.reshape(batch, seq, c_lru), conv_w[layer],
                 conv_b[layer].reshape(1, -1), wgate, bgate, lru_lambda[layer].reshape(1, -1), t_lru)

        j = layer // 2
        is_moe = layer % 2 == 1
        merge_args = (xf, o.reshape(n, -1), y.reshape(n, -1), ga, gb, w_up_attn[layer].astype(BF16),
                      w_up_lru[layer].astype(BF16), w_o[layer].astype(BF16), norm2_g[layer].reshape(1, d), tm)
        if not is_moe:
            (xf,) = _merge(*merge_args, ffn=(ffn_w_gate[j].astype(BF16), ffn_w_up[j].astype(BF16),
                                             ffn_w_down[j].astype(BF16)))
        else:
            r_pad = jnp.pad(moe_router[j], ((0, 0), (0, LANE - N_EXPERTS)))
            r_hi = r_pad.astype(BF16)
            r_lo = (r_pad - r_hi.astype(F32)).astype(BF16)
            rcat = jnp.concatenate([jnp.concatenate([r_hi, r_lo], axis=1),
                                    jnp.concatenate([r_hi, jnp.zeros_like(r_lo)], axis=1)], axis=0)
            xn, h2, logits = _merge(*merge_args, router=(rcat,))
            tok_buf, sidx, w_rows, tile_e, n_valid = _route(logits, tm_moe, tm)
            y2 = _moe_ffn(tile_e, n_valid, tok_buf, sidx, h2, w_rows, moe_w_gate[j].astype(BF16),
                          moe_w_up[j].astype(BF16), moe_w_down[j].astype(BF16), TOP_K * n + tm_moe, tm_moe)
            xf = _moe_add(xn, y2, tm)
    return xf.reshape(batch, seq, d)
```

```python
import functools

import numpy as np
import jax
import jax.numpy as jnp
from jax import lax
from jax.experimental import pallas as pl
from jax.experimental.pallas import tpu as pltpu

BF16 = jnp.bfloat16
F32 = jnp.float32

MLA_HEADS = 8
QK_NOPE_DIM = 64
QK_ROPE_DIM = 32
V_HEAD_DIM = 64
QK_HEAD_DIM = QK_NOPE_DIM + QK_ROPE_DIM
ROPE_BASE = 10000.0
LRU_HEADS = 8
CONV_WIDTH = 4
LRU_C = 8.0
N_EXPERTS = 8
TOP_K = 2
RMS_EPS = 1e-6

LANE = 128
VMEM_LIMIT = 56 * 1024 * 1024


def _params(*sem):
    return pltpu.CompilerParams(dimension_semantics=sem, vmem_limit_bytes=VMEM_LIMIT)


def _rms(x, g):
    ms = jnp.mean(x * x, axis=-1, keepdims=True)
    return x * lax.rsqrt(ms + RMS_EPS) * g


def _dot(a, b):
    return jnp.dot(a, b, preferred_element_type=F32)


def _in_qkv_body(x_ref, g_ref, w_ref, c_ref, s_ref, ct_ref, st_ref, gq_ref, gkv_ref, wq_ref, wqr_ref, wkt_ref,
                 wkrt_ref, wv_ref, gqm_ref, gqr_ref, gkm_ref, gkr_ref, xl_ref, gl_ref, ga_ref, gb_ref,
                 q_ref, kt_ref, v_ref, *, widths, q_rank, kv_rank):
    hn = _rms(x_ref[...], g_ref[...]).astype(BF16)
    o = np.cumsum((0,) + widths)
    seg = _dot(hn, w_ref[:, o[0]:o[1]])
    xl_ref[...] = _dot(hn, w_ref[:, o[1]:o[2]])
    gl_ref[...] = jax.nn.gelu(_dot(hn, w_ref[:, o[2]:o[3]]), approximate=True).astype(BF16)
    ga_ref[...] = jax.nn.sigmoid(_dot(hn, w_ref[:, o[3]:o[4]])).astype(BF16)
    gb_ref[...] = jax.nn.sigmoid(_dot(hn, w_ref[:, o[4]:o[5]])).astype(BF16)

    cqn = _rms(seg[:, 0:q_rank], gq_ref[...]).astype(BF16)
    ckvn = _rms(seg[:, q_rank:q_rank + kv_rank], gkv_ref[...])
    ck = jnp.concatenate([ckvn, seg[:, q_rank + kv_rank:]], axis=-1).astype(BF16)
    scale = QK_HEAD_DIM ** -0.5 * np.log2(np.e)
    inv_d = 1.0 / QK_HEAD_DIM

    a_all = _dot(cqn, wq_ref[...])
    b_all = _dot(cqn, wqr_ref[...])
    cm = gqm_ref[...] * c_ref[...] * scale
    sm = gqr_ref[...] * s_ref[...] * scale
    for h in range(MLA_HEADS):
        a = a_all[:, h * LANE:(h + 1) * LANE]
        b = b_all[:, h * LANE:(h + 1) * LANE]
        inv = lax.rsqrt(jnp.sum(a * a, axis=-1, keepdims=True) * inv_d + RMS_EPS)
        q_ref[:, h * LANE:(h + 1) * LANE] = (inv * (a * cm + b * sm)).astype(BF16)

    nt = (((1,), (1,)), ((), ()))
    at_all = lax.dot_general(wkt_ref[...], ck, nt, preferred_element_type=F32)
    bt_all = lax.dot_general(wkrt_ref[...], ck, nt, preferred_element_type=F32)
    cmt = gkm_ref[...] * ct_ref[...]
    smt = gkr_ref[...] * st_ref[...]
    for h in range(MLA_HEADS):
        a = at_all[h * LANE:(h + 1) * LANE, :]
        b = bt_all[h * LANE:(h + 1) * LANE, :]
        inv = lax.rsqrt(jnp.sum(a * a, axis=0, keepdims=True) * inv_d + RMS_EPS)
        kt_ref[0, h * LANE:(h + 1) * LANE, :] = (inv * (a * cmt + b * smt)).astype(BF16)
    v_ref[...] = _dot(ck, wv_ref[...]).astype(BF16)


def _in_qkv(x, g, w, widths, cos, sin, cos_t, sin_t, gq, gkv, wq, wqr, wkt, wkrt, wv, gqm, gqr, gkm, gkr, tm, seq):
    n, d = x.shape
    q_rank = gq.shape[1]
    kv_rank = gkv.shape[1]
    hq = wq.shape[1]
    hv = wv.shape[1]
    per_seq = seq // tm
    row = lambda wd: pl.BlockSpec((tm, wd), lambda i: (i, 0))
    col = pl.BlockSpec((LANE, tm), lambda i: (0, i))
    consts = (gq, gkv, wq, wqr, wkt, wkrt, wv, gqm, gqr, gkm, gkr)
    out_w = widths[1:] + (hq,)
    out_dt = (F32, BF16, BF16, BF16, BF16)
    return pl.pallas_call(
        functools.partial(_in_qkv_body, widths=widths, q_rank=q_rank, kv_rank=kv_rank),
        grid=(n // tm,),
        in_specs=[row(d), _resident(g), _resident(w), row(LANE), row(LANE), col, col]
        + [_resident(a) for a in consts],
        out_specs=[row(wd) for wd in out_w]
        + [pl.BlockSpec((1, hq, tm), lambda i: (i // per_seq, 0, i % per_seq)), row(hv)],
        out_shape=[jax.ShapeDtypeStruct((n, wd), dt) for wd, dt in zip(out_w, out_dt)]
        + [jax.ShapeDtypeStruct((n // seq, hq, seq), BF16), jax.ShapeDtypeStruct((n, hv), BF16)],
        compiler_params=_params("parallel"),
        name="in_qkv",
    )(x, g, w, cos, sin, cos_t, sin_t, *consts)


ATTN_ROW_CHUNK = 256


def _attn_body(q_ref, kt_ref, v_ref, hm_ref, o_ref, m_s, l_s, acc_s, *, tq, heads):
    i = pl.program_id(2)
    rc = ATTN_ROW_CHUNK
    n_c = tq // rc
    m_s[...] = jnp.full(m_s.shape, -jnp.inf, F32)
    l_s[...] = jnp.zeros(l_s.shape, F32)
    acc_s[...] = jnp.zeros(acc_s.shape, F32)
    lane = lax.broadcasted_iota(jnp.int32, (rc, LANE), 1)
    tri = lax.broadcasted_iota(jnp.int32, (rc, rc), 1) <= lax.broadcasted_iota(jnp.int32, (rc, rc), 0)

    def per_head(vals):
        out = vals[heads - 1]
        for hh in range(heads - 2, -1, -1):
            out = jnp.where(lane < (hh + 1) * V_HEAD_DIM, vals[hh], out)
        return out

    def load_kv(start, ncols):
        kts = [kt_ref[0, hh * LANE:(hh + 1) * LANE, pl.ds(start, ncols)] for hh in range(heads)]
        vs = v_ref[0, pl.ds(start, ncols), :]
        parts = []
        for hh in range(heads):
            ind = jnp.broadcast_to(hm_ref[hh], vs.shape)
            parts.append(jnp.concatenate([vs * ind, ind], axis=1))
        return kts, jnp.concatenate(parts, axis=0)

    def chunk_update(c, kts, vcat, ncols, masked):
        rows = pl.ds(c * rc, rc)
        ps, alphas = [], []
        for hh in range(heads):
            s = _dot(q_ref[0, rows, hh * LANE:(hh + 1) * LANE], kts[hh])
            if masked:
                tail = jnp.where(tri, s[:, ncols - rc:], -jnp.inf)
                s = tail if ncols == rc else jnp.concatenate([s[:, :ncols - rc], tail], axis=1)
            m_old = m_s[hh, rows, :]
            m_new = jnp.maximum(m_old, jnp.max(s, axis=-1, keepdims=True))
            ps.append(jnp.exp2(s - jnp.concatenate([m_new] * (ncols // LANE), axis=1)).astype(BF16))
            alphas.append(jnp.exp2(m_old - m_new))
            m_s[hh, rows, :] = m_new
        pv = _dot(jnp.concatenate(ps, axis=1), vcat)
        alpha = per_head(alphas)
        acc_s[rows, :] = alpha * acc_s[rows, :] + pv[:, :LANE]
        l_s[rows, :] = alpha * l_s[rows, :] + pv[:, LANE:]

    def kv_step(j, carry):
        kts, vcat = load_kv(pl.multiple_of(j * tq, tq), tq)
        for c in range(n_c):
            chunk_update(c, kts, vcat, tq, masked=False)
        return carry

    lax.fori_loop(0, i, kv_step, 0)
    start = pl.multiple_of(i * tq, tq)
    for c in range(n_c):
        ncols = (c + 1) * rc
        kts, vcat = load_kv(start, ncols)
        chunk_update(c, kts, vcat, ncols, masked=True)
    o_ref[0] = (acc_s[...] / l_s[...]).astype(BF16)


def _attention(q, kt, v, tq):
    b, s, hq = q.shape
    heads = LANE // V_HEAD_DIM
    n_hp = hq // (heads * LANE)
    head_mask = jnp.asarray(np.arange(LANE)[None, None, :] // V_HEAD_DIM == np.arange(heads)[:, None, None], BF16)
    return pl.pallas_call(
        functools.partial(_attn_body, tq=tq, heads=heads),
        grid=(b, n_hp, s // tq),
        in_specs=[pl.BlockSpec((1, tq, heads * LANE), lambda bi, hp, i: (bi, i, hp)),
                  pl.BlockSpec((1, heads * LANE, s), lambda bi, hp, i: (bi, hp, 0)),
                  pl.BlockSpec((1, s, LANE), lambda bi, hp, i: (bi, 0, hp)),
                  pl.BlockSpec((heads, 1, LANE), lambda bi, hp, i: (0, 0, 0))],
        out_specs=pl.BlockSpec((1, tq, LANE), lambda bi, hp, i: (bi, i, hp)),
        out_shape=jax.ShapeDtypeStruct((b, s, v.shape[2]), BF16),
        scratch_shapes=[pltpu.VMEM((heads, tq, LANE), F32), pltpu.VMEM((tq, LANE), F32),
                        pltpu.VMEM((tq, LANE), F32)],
        compiler_params=_params("parallel", "parallel", "arbitrary"),
        name="attention",
    )(q, kt, v, head_mask)


F32_SUBLANES = 8


def _lru_body(x_ref, gl_ref, cw_ref, cb_ref, wg_ref, bg_ref, lam_ref, y_ref, ext_s, h_s, *, nb, t):
    c = x_ref.shape[2]
    sub = F32_SUBLANES

    @pl.when(pl.program_id(0) == 0)
    def _():
        ext_s[:, 0:sub, :] = jnp.zeros((nb, sub, c), F32)
        h_s[...] = jnp.zeros_like(h_s)

    ext_s[:, sub:sub + t, :] = x_ref[...]
    xc = cb_ref[...][None]
    for kk in range(CONV_WIDTH):
        off = sub - (CONV_WIDTH - 1) + kk
        xc = xc + ext_s[:, off:off + t, :] * cw_ref[kk:kk + 1, :][None]
    ext_s[:, 0:sub, :] = ext_s[:, t:t + sub, :]

    xc2 = xc.reshape(nb * t, c)
    g = _dot(xc2.astype(BF16), wg_ref[...]) + bg_ref[...]
    r = jax.nn.sigmoid(g[:, :c])
    ig = jax.nn.sigmoid(g[:, c:])
    log_a = -LRU_C * r * jax.nn.softplus(-lam_ref[...])
    a = jnp.exp(log_a)
    mult = jnp.sqrt(-jnp.tanh(log_a) * (a * a + 1.0))
    bv = mult * ig * xc2

    n_g = t // sub
    a3 = a.reshape(nb * n_g, sub, c)
    b3 = bv.reshape(nb * n_g, sub, c)
    row = lax.broadcasted_iota(jnp.int32, (1, sub, c), 1)
    d = 1
    while d < sub:
        keep = row >= d
        b3 = jnp.where(keep, a3 * pltpu.roll(b3, d, axis=1) + b3, b3)
        a3 = jnp.where(keep, a3 * pltpu.roll(a3, d, axis=1), a3)
        d *= 2
    a4 = a3.reshape(nb, n_g, sub, c)
    b4 = b3.reshape(nb, n_g, sub, c)
    h = h_s[...]
    groups = []
    for g in range(n_g):
        hg = a4[:, g] * h + b4[:, g]
        groups.append(hg)
        h = hg[:, sub - 1:sub, :]
    h_s[...] = h
    hseq = jnp.stack(groups, axis=1).reshape(nb, t, c)
    y_ref[...] = (hseq * gl_ref[...].astype(F32)).astype(BF16)


def _lru(xl, gl, cw, cb, wg, bg, lam, t):
    nb, s, c = xl.shape
    blk = pl.BlockSpec((nb, t, c), lambda i: (0, i, 0))
    full = lambda a: pl.BlockSpec(a.shape, lambda i: (0, 0))
    return pl.pallas_call(
        functools.partial(_lru_body, nb=nb, t=t),
        grid=(s // t,),
        in_specs=[blk, blk, full(cw), full(cb), full(wg), full(bg), full(lam)],
        out_specs=blk,
        out_shape=jax.ShapeDtypeStruct((nb, s, c), BF16),
        scratch_shapes=[pltpu.VMEM((nb, t + F32_SUBLANES, c), F32), pltpu.VMEM((nb, 1, c), F32)],
        compiler_params=_params("arbitrary"),
        name="rg_lru",
    )(xl, gl, cw, cb, wg, bg, lam)


def _merge_body(x_ref, o_ref, y_ref, ga_ref, gb_ref, wa_ref, wb_ref, wo_ref, g2_ref, *rest, router):
    ua = _dot(o_ref[...], wa_ref[...])
    ub = _dot(y_ref[...], wb_ref[...])
    merged = ga_ref[...].astype(F32) * ua + gb_ref[...].astype(F32) * ub
    xn = x_ref[...] + _dot(merged.astype(BF16), wo_ref[...])
    h2 = _rms(xn, g2_ref[...])
    if router:
        rcat_ref, xn_ref, h2_ref, lg_ref = rest
        hi = h2.astype(BF16)
        lo = (h2 - hi.astype(F32)).astype(BF16)
        prod = _dot(jnp.concatenate([hi, lo], axis=1), rcat_ref[...])
        lg_ref[...] = prod[:, :LANE] + prod[:, LANE:]
        h2_ref[...] = h2
        xn_ref[...] = xn
    else:
        wg_ref, wu_ref, wd_ref, out_ref = rest
        hb = h2.astype(BF16)
        act = (jax.nn.silu(_dot(hb, wg_ref[...])) * _dot(hb, wu_ref[...])).astype(BF16)
        out_ref[...] = xn + _dot(act, wd_ref[...])


def _merge(x, o, y, ga, gb, wa, wb, wo, g2, tm, router=None, ffn=None):
    n, d = x.shape
    row = lambda a: pl.BlockSpec((tm, a.shape[1]), lambda i: (i, 0))
    ins = [x, o, y, ga, gb]
    consts = [wa, wb, wo, g2] + list(router or ffn)
    if router:
        out_shape = [jax.ShapeDtypeStruct((n, d), F32), jax.ShapeDtypeStruct((n, d), F32),
                     jax.ShapeDtypeStruct((n, LANE), F32)]
    else:
        out_shape = [jax.ShapeDtypeStruct((n, d), F32)]
    return pl.pallas_call(
        functools.partial(_merge_body, router=bool(router)),
        grid=(n // tm,),
        in_specs=[row(a) for a in ins] + [_resident(a) for a in consts],
        out_specs=[pl.BlockSpec((tm, sh.shape[1]), lambda i: (i, 0)) for sh in out_shape],
        out_shape=out_shape,
        compiler_params=_params("parallel"),
        name="merge_router" if router else "merge_ffn",
    )(*ins, *consts)


def _resident(a):
    return pl.BlockSpec(a.shape, lambda *_: (0,) * a.ndim, pipeline_mode=pl.Buffered(1))


def _row_copy(src_hbm, dst, src_row, dst_row, sem):
    return pltpu.make_async_copy(src_hbm.at[pl.ds(src_row, 1), :], dst.at[pl.ds(dst_row, 1), :], sem)


MOE_FF_STEPS = 2


WD_RING = 3


def _moe_ffn_body(te_ref, nv_ref, tok_ref, tokn_ref, sidx_ref, w_ref, h_hbm, wg_ref, wu_ref, wd_hbm, y_hbm,
                  xbuf, xb16, acc_ref, obuf, wd_buf, gsem, ssem, wsem, *, tm):
    i = pl.program_id(0)
    j = pl.program_id(1)
    n_i = pl.num_programs(0)
    slot = i % 2
    tf = wd_buf.shape[1]
    step = i * MOE_FF_STEPS + j
    wslot = step % WD_RING

    def wd_copy(tile, half, ring_slot):
        return pltpu.make_async_copy(wd_hbm.at[te_ref[tile], pl.ds(pl.multiple_of(half * tf, tf), tf), :],
                                     wd_buf.at[ring_slot], wsem.at[ring_slot])

    def wait_gather(s):
        pltpu.make_async_copy(h_hbm.at[pl.ds(0, tm), :], xbuf.at[s], gsem.at[s]).wait()

    def wait_scatter(s):
        pltpu.make_async_copy(obuf.at[s], y_hbm.at[pl.ds(0, tm), :], ssem.at[s]).wait()

    @pl.when((i == 0) & (j == 0))
    def _():
        obuf[...] = jnp.zeros_like(obuf)
        for half in range(MOE_FF_STEPS):
            wd_copy(0, half, half).start()

        def issue(r, carry):
            _row_copy(h_hbm, xbuf.at[0], tok_ref[r], r, gsem.at[0]).start()
            return carry

        lax.fori_loop(0, tm, issue, 0, unroll=8)

    @pl.when(i + 1 < n_i)
    def _():
        wd_copy(i + 1, j, (step + MOE_FF_STEPS) % WD_RING).start()

    wd_copy(i, j, wslot).wait()

    @pl.when(j == 0)
    def _():
        wait_gather(slot)
        xb16[...] = xbuf[slot].astype(BF16)

    def gather_next():
        for r in range(tm):
            _row_copy(h_hbm, xbuf.at[1 - slot], tokn_ref[r], r, gsem.at[1 - slot]).start()

    def scatter_prev():
        for r in range(tm):
            _row_copy(obuf.at[1 - slot], y_hbm, r, sidx_ref[r], ssem.at[1 - slot]).start()

    def compute(first):
        h = xb16[...]
        act = (jax.nn.silu(_dot(h, wg_ref[0])) * _dot(h, wu_ref[0])).astype(BF16)
        part = _dot(act, wd_buf[wslot])
        if first:
            acc_ref[...] = part
        else:
            obuf[slot] = (acc_ref[...] + part) * w_ref[...]

    valid = i < nv_ref[0]

    @pl.when(valid & (j == 0))
    def _():
        gather_next()
        compute(first=True)

    @pl.when(jnp.logical_not(valid) & (j == 0))
    def _():
        gather_next()

    @pl.when((j == 1) & (i >= 1))
    def _():
        wait_scatter(slot)

    @pl.when(valid & (j == 1))
    def _():
        scatter_prev()
        compute(first=False)

    @pl.when(jnp.logical_not(valid) & (j == 1) & (i < n_i - 1))
    def _():
        scatter_prev()

    @pl.when((i == n_i - 1) & (j == 1))
    def _():
        scatter_prev()
        wait_scatter(1 - slot)
        wait_gather(1 - slot)


def _moe_ffn(tile_e, n_valid, tok, sidx, h, w_rows, wg, wu, wd, y_rows, tm):
    d = h.shape[1]
    f = wg.shape[2]
    tf = f // MOE_FF_STEPS
    n_grid = tok.shape[0] // tm
    nxt = lambda i, j, te, nv: (jnp.minimum(i + 1, n_grid - 1),)
    grid_spec = pltpu.PrefetchScalarGridSpec(
        num_scalar_prefetch=2,
        grid=(n_grid, MOE_FF_STEPS),
        in_specs=[pl.BlockSpec((tm,), lambda i, j, te, nv: (i,), memory_space=pltpu.SMEM),
                  pl.BlockSpec((tm,), nxt, memory_space=pltpu.SMEM),
                  pl.BlockSpec((tm,), lambda i, j, te, nv: (i,), memory_space=pltpu.SMEM),
                  pl.BlockSpec((tm, 1), lambda i, j, te, nv: (i, 0)),
                  pl.BlockSpec(memory_space=pl.ANY),
                  pl.BlockSpec((1, d, tf), lambda i, j, te, nv: (te[i], 0, j)),
                  pl.BlockSpec((1, d, tf), lambda i, j, te, nv: (te[i], 0, j)),
                  pl.BlockSpec(memory_space=pl.ANY)],
        out_specs=pl.BlockSpec(memory_space=pl.ANY),
        scratch_shapes=[pltpu.VMEM((2, tm, d), F32), pltpu.VMEM((tm, d), BF16), pltpu.VMEM((tm, d), F32),
                        pltpu.VMEM((2, tm, d), F32), pltpu.VMEM((WD_RING, tf, d), BF16),
                        pltpu.SemaphoreType.DMA((2,)), pltpu.SemaphoreType.DMA((2,)),
                        pltpu.SemaphoreType.DMA((WD_RING,))],
    )
    return pl.pallas_call(
        functools.partial(_moe_ffn_body, tm=tm),
        grid_spec=grid_spec,
        out_shape=jax.ShapeDtypeStruct((y_rows, d), F32),
        compiler_params=_params("arbitrary", "arbitrary"),
        name="moe_ffn",
    )(tile_e, n_valid, tok, tok, sidx, w_rows, h, wg, wu, wd)


def _moe_add_body(x_ref, ya_ref, yb_ref, o_ref):
    o_ref[...] = x_ref[...] + ya_ref[...] + yb_ref[...]


def _moe_add(x, y2, tm):
    n, d = x.shape
    return pl.pallas_call(
        _moe_add_body,
        grid=(n // tm,),
        in_specs=[pl.BlockSpec((tm, d), lambda i: (i, 0)),
                  pl.BlockSpec((tm, d), lambda i: (i, 0)),
                  pl.BlockSpec((tm, d), lambda i: (i + n // tm, 0))],
        out_specs=pl.BlockSpec((tm, d), lambda i: (i, 0)),
        out_shape=jax.ShapeDtypeStruct((n, d), F32),
        compiler_params=_params("parallel"),
        name="moe_add",
    )(x, y2, y2)


def _topk_body(lg_ref, tri_ref, info_ref, cnt_ref):
    tm = lg_ref.shape[0]
    l8 = lg_ref[...].T[0:N_EXPERTS, :]
    eidx = lax.broadcasted_iota(jnp.int32, l8.shape, 0)
    m1 = jnp.max(l8, axis=0, keepdims=True)
    e1 = jnp.min(jnp.where(l8 == m1, eidx, N_EXPERTS), axis=0, keepdims=True)
    sel1 = eidx == e1
    rest = jnp.where(sel1, -jnp.inf, l8)
    m2 = jnp.max(rest, axis=0, keepdims=True)
    e2 = jnp.min(jnp.where(rest == m2, eidx, N_EXPERTS), axis=0, keepdims=True)
    sel2 = eidx == e2
    ex = jnp.exp(m2 - m1)
    w1 = 1.0 / (1.0 + ex)
    w2 = ex / (1.0 + ex)
    onehot = jnp.where(sel1 | sel2, 1.0, 0.0)
    incl = _dot(onehot.astype(BF16), tri_ref[...])
    excl = incl - onehot
    r1 = jnp.sum(jnp.where(sel1, excl, 0.0), axis=0, keepdims=True)
    r2 = jnp.sum(jnp.where(sel2, excl, 0.0), axis=0, keepdims=True)
    as_i32 = lambda v: lax.bitcast_convert_type(v, jnp.int32)
    rows = (e1, e2, as_i32(w1), as_i32(w2), r1.astype(jnp.int32), r2.astype(jnp.int32))
    for k, val in enumerate(rows):
        info_ref[k:k + 1, :] = val
    info_ref[len(rows):, :] = jnp.zeros((info_ref.shape[0] - len(rows), tm), jnp.int32)
    cnt_ref[0] = jnp.broadcast_to(incl[:, tm - 1:tm], (N_EXPERTS, LANE)).astype(jnp.int32)


def _topk(logits, tm):
    n = logits.shape[0]
    tri = jnp.asarray(np.triu(np.ones((tm, tm), np.float32)), BF16)
    return pl.pallas_call(
        _topk_body,
        grid=(n // tm,),
        in_specs=[pl.BlockSpec((tm, LANE), lambda i: (i, 0)), _resident(tri)],
        out_specs=[pl.BlockSpec((F32_SUBLANES, tm), lambda i: (0, i)),
                   pl.BlockSpec((1, N_EXPERTS, LANE), lambda i: (i, 0, 0))],
        out_shape=[jax.ShapeDtypeStruct((F32_SUBLANES, n), jnp.int32),
                   jax.ShapeDtypeStruct((n // tm, N_EXPERTS, LANE), jnp.int32)],
        compiler_params=_params("parallel"),
        name="route_topk",
    )(logits, tri)


def _route(logits, tm, tr):
    n = logits.shape[0]
    info, cnt = _topk(logits, tr)
    cnt = cnt[:, :, 0]
    counts = jnp.sum(cnt, axis=0)
    padded = (counts + tm - 1) // tm * tm
    pend = jnp.cumsum(padded)
    pstart = pend - padded
    base = (pstart[None, :] + jnp.cumsum(cnt, axis=0) - cnt).T
    base_tok = jnp.repeat(base, tr, axis=1)
    eidx = jnp.arange(N_EXPERTS, dtype=jnp.int32)[:, None]
    slot = lambda e, r: jnp.sum(jnp.where(eidx == e[None, :], base_tok, 0), axis=0) + r
    dest = jnp.concatenate([slot(info[0], info[4]), slot(info[1], info[5])]).astype(jnp.int32)
    n_rows = (-(-(n * TOP_K + N_EXPERTS * (tm - 1)) // tm) + 1) * tm
    assign = jnp.arange(n * TOP_K, dtype=jnp.int32)
    packed = jnp.stack([assign, jnp.concatenate([info[2], info[3]])], axis=1)
    fill = jnp.broadcast_to(jnp.array([-1, 0], jnp.int32), (n_rows, 2))
    buf = fill.at[dest].set(packed, unique_indices=True, mode="promise_in_bounds")
    a_buf = buf[:, 0]
    w_buf = lax.bitcast_convert_type(buf[:, 1], F32)
    real = a_buf >= 0
    tok_buf = jnp.where(real, a_buf % n, 0)
    spare = TOP_K * n + jnp.arange(n_rows, dtype=jnp.int32) % tm
    sidx = jnp.where(real, a_buf, spare)
    sidx = jnp.concatenate([spare[:tm], sidx[:-tm]])
    tile_start = jnp.arange(n_rows // tm, dtype=jnp.int32) * tm
    tile_e = jnp.minimum(jnp.sum(tile_start[:, None] >= pend[None, :], axis=1), N_EXPERTS - 1)
    n_valid = (pend[-1] // tm).astype(jnp.int32).reshape(1)
    return tok_buf, sidx, w_buf.reshape(-1, 1), tile_e.astype(jnp.int32), n_valid


def _head_slots(w, heads, width):
    k = w.shape[0]
    return jnp.pad(w.reshape(k, heads, width), ((0, 0), (0, 0), (0, LANE - width)))


def _rotate_half_slots(w3):
    half = QK_ROPE_DIM // 2
    lo, mid, hi = QK_NOPE_DIM, QK_NOPE_DIM + half, QK_HEAD_DIM
    out = jnp.zeros_like(w3)
    out = out.at[:, :, lo:mid].set(-w3[:, :, mid:hi])
    out = out.at[:, :, mid:hi].set(w3[:, :, lo:mid])
    return out


def _rope_gains(g):
    half = QK_ROPE_DIM // 2
    lo, mid, hi = QK_NOPE_DIM, QK_NOPE_DIM + half, QK_HEAD_DIM
    main = jnp.pad(g, (0, LANE - QK_HEAD_DIM))
    rot = jnp.zeros((LANE,), F32).at[lo:mid].set(g[mid:hi]).at[mid:hi].set(g[lo:mid])
    return main.reshape(1, LANE), rot.reshape(1, LANE)


def _rope_tables(positions):
    half = QK_ROPE_DIM // 2
    inv_freq = ROPE_BASE ** (-jnp.arange(0, QK_ROPE_DIM, 2, dtype=F32) / QK_ROPE_DIM)
    ang = positions.astype(F32).reshape(-1, 1) * inv_freq[None, :]
    n = ang.shape[0]
    dense = ang.reshape(n * half // LANE, LANE)
    cos_h = jnp.cos(dense).reshape(n, half)
    sin_h = jnp.sin(dense).reshape(n, half)
    pad = jnp.zeros((n, LANE - QK_HEAD_DIM), F32)
    cos = jnp.concatenate([jnp.ones((n, QK_NOPE_DIM), F32), cos_h, cos_h, pad], axis=-1)
    sin = jnp.concatenate([jnp.zeros((n, QK_NOPE_DIM), F32), sin_h, sin_h, pad], axis=-1)
    return cos, sin


def _block_diag(w):
    h, a, b = w.shape
    eye = jnp.eye(h, dtype=w.dtype)
    return (eye[:, None, :, None] * w[:, :, None, :]).reshape(h * a, h * b)


def kernel(x, positions, norm1_g, w_in, q_norm_g, w_uq, kv_norm_g, w_ukv, qk_q_g, qk_k_g, w_up_attn, conv_w, conv_b, w_rg, b_rg, w_ig, b_ig, lru_lambda, w_up_lru, w_o, norm2_g, ffn_w_gate, ffn_w_up, ffn_w_down, moe_router, moe_w_gate, moe_w_up, moe_w_down):
    batch, seq, d = x.shape
    n = batch * seq
    depth = norm1_g.shape[0]
    q_rank = q_norm_g.shape[1]
    kv_rank = kv_norm_g.shape[1]
    c_lru = conv_w.shape[2]
    head_w = d - 0
    lat = q_rank + kv_rank + QK_ROPE_DIM
    seg_w = -(-lat // LANE) * LANE
    widths = (seg_w, c_lru, c_lru, head_w, head_w)

    tm = 512
    tq = 2048
    t_lru = 128
    tm_moe = 512

    cos, sin = _rope_tables(positions)
    cos_t, sin_t = cos.T, sin.T
    xf = x.reshape(n, d)
    for layer in range(depth):
        w_in_l = w_in[layer]
        w_in_p = jnp.concatenate(
            [w_in_l[:, :lat], jnp.zeros((d, seg_w - lat), F32), w_in_l[:, lat:]], axis=1).astype(BF16)
        wq3 = _head_slots(w_uq[layer], MLA_HEADS, QK_HEAD_DIM)
        wkv3 = w_ukv[layer].reshape(kv_rank, MLA_HEADS, QK_NOPE_DIM + V_HEAD_DIM)
        ck_w = seg_w - q_rank
        wk3 = jnp.zeros((ck_w, MLA_HEADS, LANE), F32)
        wk3 = wk3.at[:kv_rank, :, :QK_NOPE_DIM].set(wkv3[:, :, :QK_NOPE_DIM])
        eye = jnp.eye(QK_ROPE_DIM, dtype=F32)
        wk3 = wk3.at[kv_rank:kv_rank + QK_ROPE_DIM, :, QK_NOPE_DIM:QK_HEAD_DIM].set(
            jnp.broadcast_to(eye[:, None, :], (QK_ROPE_DIM, MLA_HEADS, QK_ROPE_DIM)))
        wv = jnp.zeros((ck_w, MLA_HEADS * V_HEAD_DIM), F32).at[:kv_rank].set(
            wkv3[:, :, QK_NOPE_DIM:].reshape(kv_rank, MLA_HEADS * V_HEAD_DIM))
        flat = lambda w3: w3.reshape(w3.shape[0], MLA_HEADS * LANE).astype(BF16)
        gqm, gqr = _rope_gains(qk_q_g[layer])
        gkm, gkr = (g.reshape(LANE, 1) for g in _rope_gains(qk_k_g[layer]))
        xl, gl, ga, gb, q, kt, v = _in_qkv(
            xf, norm1_g[layer].reshape(1, d), w_in_p, widths, cos, sin, cos_t, sin_t,
            q_norm_g[layer].reshape(1, -1), kv_norm_g[layer].reshape(1, -1), flat(wq3),
            flat(_rotate_half_slots(wq3)), flat(wk3).T, flat(_rotate_half_slots(wk3)).T, wv.astype(BF16),
            gqm, gqr, gkm, gkr, tm, seq)
        o = _attention(q.reshape(batch, seq, -1), kt, v.reshape(batch, seq, -1), tq)

        wgate = jnp.concatenate([_block_diag(w_rg[layer]), _block_diag(w_ig[layer])], axis=1).astype(BF16)
        bgate = jnp.concatenate([b_rg[layer], b_ig[layer]]).reshape(1, -1)
        y = _lru(xl.reshape(batch, seq, c_lru), gl.reshape(batch, seq, c_lru), conv_w[layer],
                 conv_b[layer].reshape(1, -1), wgate, bgate, lru_lambda[layer].reshape(1, -1), t_lru)

        j = layer // 2
        is_moe = layer % 2 == 1
        merge_args = (xf, o.reshape(n, -1), y.reshape(n, -1), ga, gb, w_up_attn[layer].astype(BF16),
                      w_up_lru[layer].astype(BF16), w_o[layer].astype(BF16), norm2_g[layer].reshape(1, d), tm)
        if not is_moe:
            (xf,) = _merge(*merge_args, ffn=(ffn_w_gate[j].astype(BF16), ffn_w_up[j].astype(BF16),
                                             ffn_w_down[j].astype(BF16)))
        else:
            r_pad = jnp.pad(moe_router[j], ((0, 0), (0, LANE - N_EXPERTS)))
            r_hi = r_pad.astype(BF16)
            r_lo = (r_pad - r_hi.astype(F32)).astype(BF16)
            rcat = jnp.concatenate([jnp.concatenate([r_hi, r_lo], axis=1),
                                    jnp.concatenate([r_hi, jnp.zeros_like(r_lo)], axis=1)], axis=0)
            xn, h2, logits = _merge(*merge_args, router=(rcat,))
            tok_buf, sidx, w_rows, tile_e, n_valid = _route(logits, tm_moe, tm)
            y2 = _moe_ffn(tile_e, n_valid, tok_buf, sidx, h2, w_rows, moe_w_gate[j].astype(BF16),
                          moe_w_up[j].astype(BF16), moe_w_down[j].astype(BF16), TOP_K * n + tm_moe, tm_moe)
            xf = _moe_add(xn, y2, tm)
    return xf.reshape(batch, seq, d)
```

```python
import functools

import numpy as np
import jax
import jax.numpy as jnp
from jax import lax
from jax.experimental import pallas as pl
from jax.experimental.pallas import tpu as pltpu

BF16 = jnp.bfloat16
F32 = jnp.float32

MLA_HEADS = 8
QK_NOPE_DIM = 64
QK_ROPE_DIM = 32
V_HEAD_DIM = 64
QK_HEAD_DIM = QK_NOPE_DIM + QK_ROPE_DIM
ROPE_BASE = 10000.0
LRU_HEADS = 8
CONV_WIDTH = 4
LRU_C = 8.0
N_EXPERTS = 8
TOP_K = 2
RMS_EPS = 1e-6

LANE = 128
VMEM_LIMIT = 56 * 1024 * 1024


def _params(*sem):
    return pltpu.CompilerParams(dimension_semantics=sem, vmem_limit_bytes=VMEM_LIMIT)


def _rms(x, g):
    ms = jnp.mean(x * x, axis=-1, keepdims=True)
    return x * lax.rsqrt(ms + RMS_EPS) * g


def _dot(a, b):
    return jnp.dot(a, b, preferred_element_type=F32)


def _in_qkv_body(x_ref, g_ref, w_ref, c_ref, s_ref, ct_ref, st_ref, gq_ref, gkv_ref, wq_ref, wqr_ref, wkt_ref,
                 wkrt_ref, wv_ref, gqm_ref, gqr_ref, gkm_ref, gkr_ref, xl_ref, gl_ref, ga_ref, gb_ref,
                 q_ref, kt_ref, v_ref, *, widths, q_rank, kv_rank):
    hn = _rms(x_ref[...], g_ref[...]).astype(BF16)
    o = np.cumsum((0,) + widths)
    seg = _dot(hn, w_ref[:, o[0]:o[1]])
    xl_ref[...] = _dot(hn, w_ref[:, o[1]:o[2]])
    gl_ref[...] = jax.nn.gelu(_dot(hn, w_ref[:, o[2]:o[3]]), approximate=True).astype(BF16)
    ga_ref[...] = jax.nn.sigmoid(_dot(hn, w_ref[:, o[3]:o[4]])).astype(BF16)
    gb_ref[...] = jax.nn.sigmoid(_dot(hn, w_ref[:, o[4]:o[5]])).astype(BF16)

    cqn = _rms(seg[:, 0:q_rank], gq_ref[...]).astype(BF16)
    ckvn = _rms(seg[:, q_rank:q_rank + kv_rank], gkv_ref[...])
    ck = jnp.concatenate([ckvn, seg[:, q_rank + kv_rank:]], axis=-1).astype(BF16)
    scale = QK_HEAD_DIM ** -0.5 * np.log2(np.e)
    inv_d = 1.0 / QK_HEAD_DIM

    a_all = _dot(cqn, wq_ref[...])
    b_all = _dot(cqn, wqr_ref[...])
    cm = gqm_ref[...] * c_ref[...] * scale
    sm = gqr_ref[...] * s_ref[...] * scale
    for h in range(MLA_HEADS):
        a = a_all[:, h * LANE:(h + 1) * LANE]
        b = b_all[:, h * LANE:(h + 1) * LANE]
        inv = lax.rsqrt(jnp.sum(a * a, axis=-1, keepdims=True) * inv_d + RMS_EPS)
        q_ref[:, h * LANE:(h + 1) * LANE] = (inv * (a * cm + b * sm)).astype(BF16)

    nt = (((1,), (1,)), ((), ()))
    at_all = lax.dot_general(wkt_ref[...], ck, nt, preferred_element_type=F32)
    bt_all = lax.dot_general(wkrt_ref[...], ck, nt, preferred_element_type=F32)
    cmt = gkm_ref[...] * ct_ref[...]
    smt = gkr_ref[...] * st_ref[...]
    for h in range(MLA_HEADS):
        a = at_all[h * LANE:(h + 1) * LANE, :]
        b = bt_all[h * LANE:(h + 1) * LANE, :]
        inv = lax.rsqrt(jnp.sum(a * a, axis=0, keepdims=True) * inv_d + RMS_EPS)
        kt_ref[0, h * LANE:(h + 1) * LANE, :] = (inv * (a * cmt + b * smt)).astype(BF16)
    v_ref[...] = _dot(ck, wv_ref[...]).astype(BF16)


def _in_qkv(x, g, w, widths, cos, sin, cos_t, sin_t, gq, gkv, wq, wqr, wkt, wkrt, wv, gqm, gqr, gkm, gkr, tm, seq):
    n, d = x.shape
    q_rank = gq.shape[1]
    kv_rank = gkv.shape[1]
    hq = wq.shape[1]
    hv = wv.shape[1]
    per_seq = seq // tm
    row = lambda wd: pl.BlockSpec((tm, wd), lambda i: (i, 0))
    col = pl.BlockSpec((LANE, tm), lambda i: (0, i))
    consts = (gq, gkv, wq, wqr, wkt, wkrt, wv, gqm, gqr, gkm, gkr)
    out_w = widths[1:] + (hq,)
    out_dt = (F32, BF16, BF16, BF16, BF16)
    return pl.pallas_call(
        functools.partial(_in_qkv_body, widths=widths, q_rank=q_rank, kv_rank=kv_rank),
        grid=(n // tm,),
        in_specs=[row(d), _resident(g), _resident(w), row(LANE), row(LANE), col, col]
        + [_resident(a) for a in consts],
        out_specs=[row(wd) for wd in out_w]
        + [pl.BlockSpec((1, hq, tm), lambda i: (i // per_seq, 0, i % per_seq)), row(hv)],
        out_shape=[jax.ShapeDtypeStruct((n, wd), dt) for wd, dt in zip(out_w, out_dt)]
        + [jax.ShapeDtypeStruct((n // seq, hq, seq), BF16), jax.ShapeDtypeStruct((n, hv), BF16)],
        compiler_params=_params("parallel"),
        name="in_qkv",
    )(x, g, w, cos, sin, cos_t, sin_t, *consts)


ATTN_ROW_CHUNK = 256


def _attn_body(q_ref, kt_ref, v_ref, hm_ref, o_ref, m_s, l_s, acc_s, *, tq, heads):
    i = pl.program_id(2)
    rc = ATTN_ROW_CHUNK
    n_c = tq // rc
    m_s[...] = jnp.full(m_s.shape, -jnp.inf, F32)
    l_s[...] = jnp.zeros(l_s.shape, F32)
    acc_s[...] = jnp.zeros(acc_s.shape, F32)
    lane = lax.broadcasted_iota(jnp.int32, (rc, LANE), 1)
    tri = lax.broadcasted_iota(jnp.int32, (rc, rc), 1) <= lax.broadcasted_iota(jnp.int32, (rc, rc), 0)

    def per_head(vals):
        out = vals[heads - 1]
        for hh in range(heads - 2, -1, -1):
            out = jnp.where(lane < (hh + 1) * V_HEAD_DIM, vals[hh], out)
        return out

    def load_kv(start, ncols):
        kts = [kt_ref[0, hh * LANE:(hh + 1) * LANE, pl.ds(start, ncols)] for hh in range(heads)]
        vs = v_ref[0, pl.ds(start, ncols), :]
        parts = []
        for hh in range(heads):
            ind = jnp.broadcast_to(hm_ref[hh], vs.shape)
            parts.append(jnp.concatenate([vs * ind, ind], axis=1))
        return kts, jnp.concatenate(parts, axis=0)

    def chunk_update(c, kts, vcat, ncols, masked):
        rows = pl.ds(c * rc, rc)
        ps, alphas = [], []
        for hh in range(heads):
            s = _dot(q_ref[0, rows, hh * LANE:(hh + 1) * LANE], kts[hh])
            if masked:
                tail = jnp.where(tri, s[:, ncols - rc:], -jnp.inf)
                s = tail if ncols == rc else jnp.concatenate([s[:, :ncols - rc], tail], axis=1)
            m_old = m_s[hh, rows, :]
            m_new = jnp.maximum(m_old, jnp.max(s, axis=-1, keepdims=True))
            ps.append(jnp.exp2(s - jnp.concatenate([m_new] * (ncols // LANE), axis=1)).astype(BF16))
            alphas.append(jnp.exp2(m_old - m_new))
            m_s[hh, rows, :] = m_new
        pv = _dot(jnp.concatenate(ps, axis=1), vcat)
        alpha = per_head(alphas)
        acc_s[rows, :] = alpha * acc_s[rows, :] + pv[:, :LANE]
        l_s[rows, :] = alpha * l_s[rows, :] + pv[:, LANE:]

    def kv_step(j, carry):
        kts, vcat = load_kv(pl.multiple_of(j * tq, tq), tq)
        for c in range(n_c):
            chunk_update(c, kts, vcat, tq, masked=False)
        return carry

    lax.fori_loop(0, i, kv_step, 0)
    start = pl.multiple_of(i * tq, tq)
    for c in range(n_c):
        ncols = (c + 1) * rc
        kts, vcat = load_kv(start, ncols)
        chunk_update(c, kts, vcat, ncols, masked=True)
    o_ref[0] = (acc_s[...] / l_s[...]).astype(BF16)


def _attention(q, kt, v, tq):
    b, s, hq = q.shape
    heads = LANE // V_HEAD_DIM
    n_hp = hq // (heads * LANE)
    head_mask = jnp.asarray(np.arange(LANE)[None, None, :] // V_HEAD_DIM == np.arange(heads)[:, None, None], BF16)
    return pl.pallas_call(
        functools.partial(_attn_body, tq=tq, heads=heads),
        grid=(b, n_hp, s // tq),
        in_specs=[pl.BlockSpec((1, tq, heads * LANE), lambda bi, hp, i: (bi, i, hp)),
                  pl.BlockSpec((1, heads * LANE, s), lambda bi, hp, i: (bi, hp, 0)),
                  pl.BlockSpec((1, s, LANE), lambda bi, hp, i: (bi, 0, hp)),
                  pl.BlockSpec((heads, 1, LANE), lambda bi, hp, i: (0, 0, 0))],
        out_specs=pl.BlockSpec((1, tq, LANE), lambda bi, hp, i: (bi, i, hp)),
        out_shape=jax.ShapeDtypeStruct((b, s, v.shape[2]), BF16),
        scratch_shapes=[pltpu.VMEM((heads, tq, LANE), F32), pltpu.VMEM((tq, LANE), F32),
                        pltpu.VMEM((tq, LANE), F32)],
        compiler_params=_params("parallel", "parallel", "arbitrary"),
        name="attention",
    )(q, kt, v, head_mask)


F32_SUBLANES = 8


def _lru_body(x_ref, gl_ref, cw_ref, cb_ref, wg_ref, bg_ref, lam_ref, y_ref, ext_s, h_s, *, nb, t):
    c = x_ref.shape[2]
    sub = F32_SUBLANES

    @pl.when(pl.program_id(0) == 0)
    def _():
        ext_s[:, 0:sub, :] = jnp.zeros((nb, sub, c), F32)
        h_s[...] = jnp.zeros_like(h_s)

    ext_s[:, sub:sub + t, :] = x_ref[...]
    xc = cb_ref[...][None]
    for kk in range(CONV_WIDTH):
        off = sub - (CONV_WIDTH - 1) + kk
        xc = xc + ext_s[:, off:off + t, :] * cw_ref[kk:kk + 1, :][None]
    ext_s[:, 0:sub, :] = ext_s[:, t:t + sub, :]

    xc2 = xc.reshape(nb * t, c)
    g = _dot(xc2.astype(BF16), wg_ref[...]) + bg_ref[...]
    r = jax.nn.sigmoid(g[:, :c])
    ig = jax.nn.sigmoid(g[:, c:])
    log_a = -LRU_C * r * jax.nn.softplus(-lam_ref[...])
    a = jnp.exp(log_a)
    mult = jnp.sqrt(-jnp.tanh(log_a) * (a * a + 1.0))
    bv = mult * ig * xc2

    n_g = t // sub
    a3 = a.reshape(nb * n_g, sub, c)
    b3 = bv.reshape(nb * n_g, sub, c)
    row = lax.broadcasted_iota(jnp.int32, (1, sub, c), 1)
    d = 1
    while d < sub:
        keep = row >= d
        b3 = jnp.where(keep, a3 * pltpu.roll(b3, d, axis=1) + b3, b3)
        a3 = jnp.where(keep, a3 * pltpu.roll(a3, d, axis=1), a3)
        d *= 2
    a4 = a3.reshape(nb, n_g, sub, c)
    b4 = b3.reshape(nb, n_g, sub, c)
    h = h_s[...]
    groups = []
    for g in range(n_g):
        hg = a4[:, g] * h + b4[:, g]
        groups.append(hg)
        h = hg[:, sub - 1:sub, :]
    h_s[...] = h
    hseq = jnp.stack(groups, axis=1).reshape(nb, t, c)
    y_ref[...] = (hseq * gl_ref[...].astype(F32)).astype(BF16)


def _lru(xl, gl, cw, cb, wg, bg, lam, t):
    nb, s, c = xl.shape
    blk = pl.BlockSpec((nb, t, c), lambda i: (0, i, 0))
    full = lambda a: pl.BlockSpec(a.shape, lambda i: (0, 0))
    return pl.pallas_call(
        functools.partial(_lru_body, nb=nb, t=t),
        grid=(s // t,),
        in_specs=[blk, blk, full(cw), full(cb), full(wg), full(bg), full(lam)],
        out_specs=blk,
        out_shape=jax.ShapeDtypeStruct((nb, s, c), BF16),
        scratch_shapes=[pltpu.VMEM((nb, t + F32_SUBLANES, c), F32), pltpu.VMEM((nb, 1, c), F32)],
        compiler_params=_params("arbitrary"),
        name="rg_lru",
    )(xl, gl, cw, cb, wg, bg, lam)


def _merge_body(x_ref, o_ref, y_ref, ga_ref, gb_ref, wa_ref, wb_ref, wo_ref, g2_ref, *rest, router):
    ua = _dot(o_ref[...], wa_ref[...])
    ub = _dot(y_ref[...], wb_ref[...])
    merged = ga_ref[...].astype(F32) * ua + gb_ref[...].astype(F32) * ub
    xn = x_ref[...] + _dot(merged.astype(BF16), wo_ref[...])
    h2 = _rms(xn, g2_ref[...])
    if router:
        rcat_ref, tri_ref, xn_ref, h2_ref, info_ref, cnt_ref = rest
        hi = h2.astype(BF16)
        lo = (h2 - hi.astype(F32)).astype(BF16)
        prod = _dot(jnp.concatenate([hi, lo], axis=1), rcat_ref[...])
        _topk_rows(prod[:, :LANE] + prod[:, LANE:], tri_ref, info_ref, cnt_ref)
        h2_ref[...] = h2
        xn_ref[...] = xn
    else:
        wg_ref, wu_ref, wd_ref, out_ref = rest
        hb = h2.astype(BF16)
        act = (jax.nn.silu(_dot(hb, wg_ref[...])) * _dot(hb, wu_ref[...])).astype(BF16)
        out_ref[...] = xn + _dot(act, wd_ref[...])


def _merge(x, o, y, ga, gb, wa, wb, wo, g2, tm, router=None, ffn=None):
    n, d = x.shape
    row = lambda a: pl.BlockSpec((tm, a.shape[1]), lambda i: (i, 0))
    ins = [x, o, y, ga, gb]
    consts = [wa, wb, wo, g2] + list(router or ffn)
    out_shape = [jax.ShapeDtypeStruct((n, d), F32)]
    out_specs = [pl.BlockSpec((tm, d), lambda i: (i, 0))]
    if router:
        consts.append(jnp.asarray(np.triu(np.ones((tm, tm), np.float32)), BF16))
        out_shape += [jax.ShapeDtypeStruct((n, d), F32), jax.ShapeDtypeStruct((F32_SUBLANES, n), jnp.int32),
                      jax.ShapeDtypeStruct((n // tm, N_EXPERTS, LANE), jnp.int32)]
        out_specs += [pl.BlockSpec((tm, d), lambda i: (i, 0)), pl.BlockSpec((F32_SUBLANES, tm), lambda i: (0, i)),
                      pl.BlockSpec((1, N_EXPERTS, LANE), lambda i: (i, 0, 0))]
    return pl.pallas_call(
        functools.partial(_merge_body, router=bool(router)),
        grid=(n // tm,),
        in_specs=[row(a) for a in ins] + [_resident(a) for a in consts],
        out_specs=out_specs,
        out_shape=out_shape,
        compiler_params=_params("parallel"),
        name="merge_router" if router else "merge_ffn",
    )(*ins, *consts)


def _resident(a):
    return pl.BlockSpec(a.shape, lambda *_: (0,) * a.ndim, pipeline_mode=pl.Buffered(1))


def _row_copy(src_hbm, dst, src_row, dst_row, sem):
    return pltpu.make_async_copy(src_hbm.at[pl.ds(src_row, 1), :], dst.at[pl.ds(dst_row, 1), :], sem)


MOE_FF_STEPS = 2


WD_RING = 3


def _moe_ffn_body(te_ref, nv_ref, tok_ref, tokn_ref, sidx_ref, w_ref, h_hbm, wg_ref, wu_ref, wd_hbm, y_hbm,
                  xbuf, xb16, acc_ref, obuf, wd_buf, gsem, ssem, wsem, *, tm):
    i = pl.program_id(0)
    j = pl.program_id(1)
    n_i = pl.num_programs(0)
    slot = i % 2
    tf = wd_buf.shape[1]
    step = i * MOE_FF_STEPS + j
    wslot = step % WD_RING

    def wd_copy(tile, half, ring_slot):
        return pltpu.make_async_copy(wd_hbm.at[te_ref[tile], pl.ds(pl.multiple_of(half * tf, tf), tf), :],
                                     wd_buf.at[ring_slot], wsem.at[ring_slot])

    def wait_gather(s):
        pltpu.make_async_copy(h_hbm.at[pl.ds(0, tm), :], xbuf.at[s], gsem.at[s]).wait()

    def wait_scatter(s):
        pltpu.make_async_copy(obuf.at[s], y_hbm.at[pl.ds(0, tm), :], ssem.at[s]).wait()

    @pl.when((i == 0) & (j == 0))
    def _():
        obuf[...] = jnp.zeros_like(obuf)
        for half in range(MOE_FF_STEPS):
            wd_copy(0, half, half).start()

        def issue(r, carry):
            _row_copy(h_hbm, xbuf.at[0], tok_ref[r], r, gsem.at[0]).start()
            return carry

        lax.fori_loop(0, tm, issue, 0, unroll=8)

    @pl.when(i + 1 < n_i)
    def _():
        wd_copy(i + 1, j, (step + MOE_FF_STEPS) % WD_RING).start()

    wd_copy(i, j, wslot).wait()

    @pl.when(j == 0)
    def _():
        wait_gather(slot)
        xb16[...] = xbuf[slot].astype(BF16)

    def gather_next():
        for r in range(tm):
            _row_copy(h_hbm, xbuf.at[1 - slot], tokn_ref[r], r, gsem.at[1 - slot]).start()

    def scatter_prev():
        for r in range(tm):
            _row_copy(obuf.at[1 - slot], y_hbm, r, sidx_ref[r], ssem.at[1 - slot]).start()

    def compute(first):
        h = xb16[...]
        act = (jax.nn.silu(_dot(h, wg_ref[0])) * _dot(h, wu_ref[0])).astype(BF16)
        part = _dot(act, wd_buf[wslot])
        if first:
            acc_ref[...] = part
        else:
            obuf[slot] = (acc_ref[...] + part) * w_ref[...]

    valid = i < nv_ref[0]

    @pl.when(valid & (j == 0))
    def _():
        gather_next()
        compute(first=True)

    @pl.when(jnp.logical_not(valid) & (j == 0))
    def _():
        gather_next()

    @pl.when((j == 1) & (i >= 1))
    def _():
        wait_scatter(slot)

    @pl.when(valid & (j == 1))
    def _():
        scatter_prev()
        compute(first=False)

    @pl.when(jnp.logical_not(valid) & (j == 1) & (i < n_i - 1))
    def _():
        scatter_prev()

    @pl.when((i == n_i - 1) & (j == 1))
    def _():
        scatter_prev()
        wait_scatter(1 - slot)
        wait_gather(1 - slot)


def _moe_ffn(tile_e, n_valid, tok, sidx, h, w_rows, wg, wu, wd, y_rows, tm):
    d = h.shape[1]
    f = wg.shape[2]
    tf = f // MOE_FF_STEPS
    n_grid = tok.shape[0] // tm
    nxt = lambda i, j, te, nv: (jnp.minimum(i + 1, n_grid - 1),)
    grid_spec = pltpu.PrefetchScalarGridSpec(
        num_scalar_prefetch=2,
        grid=(n_grid, MOE_FF_STEPS),
        in_specs=[pl.BlockSpec((tm,), lambda i, j, te, nv: (i,), memory_space=pltpu.SMEM),
                  pl.BlockSpec((tm,), nxt, memory_space=pltpu.SMEM),
                  pl.BlockSpec((tm,), lambda i, j, te, nv: (i,), memory_space=pltpu.SMEM),
                  pl.BlockSpec((tm, 1), lambda i, j, te, nv: (i, 0)),
                  pl.BlockSpec(memory_space=pl.ANY),
                  pl.BlockSpec((1, d, tf), lambda i, j, te, nv: (te[i], 0, j)),
                  pl.BlockSpec((1, d, tf), lambda i, j, te, nv: (te[i], 0, j)),
                  pl.BlockSpec(memory_space=pl.ANY)],
        out_specs=pl.BlockSpec(memory_space=pl.ANY),
        scratch_shapes=[pltpu.VMEM((2, tm, d), F32), pltpu.VMEM((tm, d), BF16), pltpu.VMEM((tm, d), F32),
                        pltpu.VMEM((2, tm, d), F32), pltpu.VMEM((WD_RING, tf, d), BF16),
                        pltpu.SemaphoreType.DMA((2,)), pltpu.SemaphoreType.DMA((2,)),
                        pltpu.SemaphoreType.DMA((WD_RING,))],
    )
    return pl.pallas_call(
        functools.partial(_moe_ffn_body, tm=tm),
        grid_spec=grid_spec,
        out_shape=jax.ShapeDtypeStruct((y_rows, d), F32),
        compiler_params=_params("arbitrary", "arbitrary"),
        name="moe_ffn",
    )(tile_e, n_valid, tok, tok, sidx, w_rows, h, wg, wu, wd)


def _moe_add_body(x_ref, ya_ref, yb_ref, o_ref):
    o_ref[...] = x_ref[...] + ya_ref[...] + yb_ref[...]


def _moe_add(x, y2, tm):
    n, d = x.shape
    return pl.pallas_call(
        _moe_add_body,
        grid=(n // tm,),
        in_specs=[pl.BlockSpec((tm, d), lambda i: (i, 0)),
                  pl.BlockSpec((tm, d), lambda i: (i, 0)),
                  pl.BlockSpec((tm, d), lambda i: (i + n // tm, 0))],
        out_specs=pl.BlockSpec((tm, d), lambda i: (i, 0)),
        out_shape=jax.ShapeDtypeStruct((n, d), F32),
        compiler_params=_params("parallel"),
        name="moe_add",
    )(x, y2, y2)


def _topk_rows(logits, tri_ref, info_ref, cnt_ref):
    tm = logits.shape[0]
    l8 = logits.T[0:N_EXPERTS, :]
    eidx = lax.broadcasted_iota(jnp.int32, l8.shape, 0)
    m1 = jnp.max(l8, axis=0, keepdims=True)
    e1 = jnp.min(jnp.where(l8 == m1, eidx, N_EXPERTS), axis=0, keepdims=True)
    sel1 = eidx == e1
    rest = jnp.where(sel1, -jnp.inf, l8)
    m2 = jnp.max(rest, axis=0, keepdims=True)
    e2 = jnp.min(jnp.where(rest == m2, eidx, N_EXPERTS), axis=0, keepdims=True)
    sel2 = eidx == e2
    ex = jnp.exp(m2 - m1)
    w1 = 1.0 / (1.0 + ex)
    w2 = ex / (1.0 + ex)
    onehot = jnp.where(sel1 | sel2, 1.0, 0.0)
    incl = _dot(onehot.astype(BF16), tri_ref[...])
    excl = incl - onehot
    r1 = jnp.sum(jnp.where(sel1, excl, 0.0), axis=0, keepdims=True)
    r2 = jnp.sum(jnp.where(sel2, excl, 0.0), axis=0, keepdims=True)
    as_i32 = lambda v: lax.bitcast_convert_type(v, jnp.int32)
    rows = (e1, e2, as_i32(w1), as_i32(w2), r1.astype(jnp.int32), r2.astype(jnp.int32))
    for k, val in enumerate(rows):
        info_ref[k:k + 1, :] = val
    info_ref[len(rows):, :] = jnp.zeros((info_ref.shape[0] - len(rows), tm), jnp.int32)
    cnt_ref[0] = jnp.broadcast_to(incl[:, tm - 1:tm], (N_EXPERTS, LANE)).astype(jnp.int32)


def _route(info, cnt, tm):
    n = info.shape[1]
    tr = n // cnt.shape[0]
    cnt = cnt[:, :, 0]
    counts = jnp.sum(cnt, axis=0)
    padded = (counts + tm - 1) // tm * tm
    pend = jnp.cumsum(padded)
    pstart = pend - padded
    base = (pstart[None, :] + jnp.cumsum(cnt, axis=0) - cnt).T
    base_tok = jnp.repeat(base, tr, axis=1)
    eidx = jnp.arange(N_EXPERTS, dtype=jnp.int32)[:, None]
    slot = lambda e, r: jnp.sum(jnp.where(eidx == e[None, :], base_tok, 0), axis=0) + r
    dest = jnp.concatenate([slot(info[0], info[4]), slot(info[1], info[5])]).astype(jnp.int32)
    n_rows = (-(-(n * TOP_K + N_EXPERTS * (tm - 1)) // tm) + 1) * tm
    assign = jnp.arange(n * TOP_K, dtype=jnp.int32)
    packed = jnp.stack([assign, jnp.concatenate([info[2], info[3]])], axis=1)
    fill = jnp.broadcast_to(jnp.array([-1, 0], jnp.int32), (n_rows, 2))
    buf = fill.at[dest].set(packed, unique_indices=True, mode="promise_in_bounds")
    a_buf = buf[:, 0]
    w_buf = lax.bitcast_convert_type(buf[:, 1], F32)
    real = a_buf >= 0
    tok_buf = jnp.where(real, a_buf % n, 0)
    spare = TOP_K * n + jnp.arange(n_rows, dtype=jnp.int32) % tm
    sidx = jnp.where(real, a_buf, spare)
    sidx = jnp.concatenate([spare[:tm], sidx[:-tm]])
    tile_start = jnp.arange(n_rows // tm, dtype=jnp.int32) * tm
    tile_e = jnp.minimum(jnp.sum(tile_start[:, None] >= pend[None, :], axis=1), N_EXPERTS - 1)
    n_valid = (pend[-1] // tm).astype(jnp.int32).reshape(1)
    return tok_buf, sidx, w_buf.reshape(-1, 1), tile_e.astype(jnp.int32), n_valid


def _head_slots(w, heads, width):
    k = w.shape[0]
    return jnp.pad(w.reshape(k, heads, width), ((0, 0), (0, 0), (0, LANE - width)))


def _rotate_half_slots(w3):
    half = QK_ROPE_DIM // 2
    lo, mid, hi = QK_NOPE_DIM, QK_NOPE_DIM + half, QK_HEAD_DIM
    out = jnp.zeros_like(w3)
    out = out.at[:, :, lo:mid].set(-w3[:, :, mid:hi])
    out = out.at[:, :, mid:hi].set(w3[:, :, lo:mid])
    return out


def _rope_gains(g):
    half = QK_ROPE_DIM // 2
    lo, mid, hi = QK_NOPE_DIM, QK_NOPE_DIM + half, QK_HEAD_DIM
    main = jnp.pad(g, (0, LANE - QK_HEAD_DIM))
    rot = jnp.zeros((LANE,), F32).at[lo:mid].set(g[mid:hi]).at[mid:hi].set(g[lo:mid])
    return main.reshape(1, LANE), rot.reshape(1, LANE)


def _rope_tables(positions):
    half = QK_ROPE_DIM // 2
    inv_freq = ROPE_BASE ** (-jnp.arange(0, QK_ROPE_DIM, 2, dtype=F32) / QK_ROPE_DIM)
    ang = positions.astype(F32).reshape(-1, 1) * inv_freq[None, :]
    n = ang.shape[0]
    dense = ang.reshape(n * half // LANE, LANE)
    cos_h = jnp.cos(dense).reshape(n, half)
    sin_h = jnp.sin(dense).reshape(n, half)
    pad = jnp.zeros((n, LANE - QK_HEAD_DIM), F32)
    cos = jnp.concatenate([jnp.ones((n, QK_NOPE_DIM), F32), cos_h, cos_h, pad], axis=-1)
    sin = jnp.concatenate([jnp.zeros((n, QK_NOPE_DIM), F32), sin_h, sin_h, pad], axis=-1)
    return cos, sin


def _block_diag(w):
    h, a, b = w.shape
    eye = jnp.eye(h, dtype=w.dtype)
    return (eye[:, None, :, None] * w[:, :, None, :]).reshape(h * a, h * b)


def kernel(x, positions, norm1_g, w_in, q_norm_g, w_uq, kv_norm_g, w_ukv, qk_q_g, qk_k_g, w_up_attn, conv_w, conv_b, w_rg, b_rg, w_ig, b_ig, lru_lambda, w_up_lru, w_o, norm2_g, ffn_w_gate, ffn_w_up, ffn_w_down, moe_router, moe_w_gate, moe_w_up, moe_w_down):
    batch, seq, d = x.shape
    n = batch * seq
    depth = norm1_g.shape[0]
    q_rank = q_norm_g.shape[1]
    kv_rank = kv_norm_g.shape[1]
    c_lru = conv_w.shape[2]
    lat = q_rank + kv_rank + QK_ROPE_DIM
    seg_w = -(-lat // LANE) * LANE
    widths = (seg_w, c_lru, c_lru, d, d)

    tm = 512
    tq = min(2048, seq)
    t_lru = 128
    tm_moe = 512

    cos, sin = _rope_tables(positions)
    cos_t, sin_t = cos.T, sin.T
    xf = x.reshape(n, d)
    for layer in range(depth):
        w_in_l = w_in[layer]
        w_in_p = jnp.concatenate(
            [w_in_l[:, :lat], jnp.zeros((d, seg_w - lat), F32), w_in_l[:, lat:]], axis=1).astype(BF16)
        wq3 = _head_slots(w_uq[layer], MLA_HEADS, QK_HEAD_DIM)
        wkv3 = w_ukv[layer].reshape(kv_rank, MLA_HEADS, QK_NOPE_DIM + V_HEAD_DIM)
        ck_w = seg_w - q_rank
        wk3 = jnp.zeros((ck_w, MLA_HEADS, LANE), F32)
        wk3 = wk3.at[:kv_rank, :, :QK_NOPE_DIM].set(wkv3[:, :, :QK_NOPE_DIM])
        eye = jnp.eye(QK_ROPE_DIM, dtype=F32)
        wk3 = wk3.at[kv_rank:kv_rank + QK_ROPE_DIM, :, QK_NOPE_DIM:QK_HEAD_DIM].set(
            jnp.broadcast_to(eye[:, None, :], (QK_ROPE_DIM, MLA_HEADS, QK_ROPE_DIM)))
        wv = jnp.zeros((ck_w, MLA_HEADS * V_HEAD_DIM), F32).at[:kv_rank].set(
            wkv3[:, :, QK_NOPE_DIM:].reshape(kv_rank, MLA_HEADS * V_HEAD_DIM))
        flat = lambda w3: w3.reshape(w3.shape[0], MLA_HEADS * LANE).astype(BF16)
        gqm, gqr = _rope_gains(qk_q_g[layer])
        gkm, gkr = (g.reshape(LANE, 1) for g in _rope_gains(qk_k_g[layer]))
        xl, gl, ga, gb, q, kt, v = _in_qkv(
            xf, norm1_g[layer].reshape(1, d), w_in_p, widths, cos, sin, cos_t, sin_t,
            q_norm_g[layer].reshape(1, -1), kv_norm_g[layer].reshape(1, -1), flat(wq3),
            flat(_rotate_half_slots(wq3)), flat(wk3).T, flat(_rotate_half_slots(wk3)).T, wv.astype(BF16),
            gqm, gqr, gkm, gkr, tm, seq)
        o = _attention(q.reshape(batch, seq, -1), kt, v.reshape(batch, seq, -1), tq)

        wgate = jnp.concatenate([_block_diag(w_rg[layer]), _block_diag(w_ig[layer])], axis=1).astype(BF16)
        bgate = jnp.concatenate([b_rg[layer], b_ig[layer]]).reshape(1, -1)
        y = _lru(xl.reshape(batch, seq, c_lru), gl.reshape(batch, seq, c_lru), conv_w[layer],
                 conv_b[layer].reshape(1, -1), wgate, bgate, lru_lambda[layer].reshape(1, -1), t_lru)

        j = layer // 2
        is_moe = layer % 2 == 1
        merge_args = (xf, o.reshape(n, -1), y.reshape(n, -1), ga, gb, w_up_attn[layer].astype(BF16),
                      w_up_lru[layer].astype(BF16), w_o[layer].astype(BF16), norm2_g[layer].reshape(1, d), tm)
        if not is_moe:
            (xf,) = _merge(*merge_args, ffn=(ffn_w_gate[j].astype(BF16), ffn_w_up[j].astype(BF16),
                                             ffn_w_down[j].astype(BF16)))
        else:
            r_pad = jnp.pad(moe_router[j], ((0, 0), (0, LANE - N_EXPERTS)))
            r_hi = r_pad.astype(BF16)
            r_lo = (r_pad - r_hi.astype(F32)).astype(BF16)
            rcat = jnp.concatenate([jnp.concatenate([r_hi, r_lo], axis=1),
                                    jnp.concatenate([r_hi, jnp.zeros_like(r_lo)], axis=1)], axis=0)
            xn, h2, info, cnt = _merge(*merge_args, router=(rcat,))
            tok_buf, sidx, w_rows, tile_e, n_valid = _route(info, cnt, tm_moe)
            y2 = _moe_ffn(tile_e, n_valid, tok_buf, sidx, h2, w_rows, moe_w_gate[j].astype(BF16),
                          moe_w_up[j].astype(BF16), moe_w_down[j].astype(BF16), TOP_K * n + tm_moe, tm_moe)
            xf = _moe_add(xn, y2, tm)
    return xf.reshape(batch, seq, d)
```

```python
import functools

import numpy as np
import jax
import jax.numpy as jnp
from jax import lax
from jax.experimental import pallas as pl
from jax.experimental.pallas import tpu as pltpu

BF16 = jnp.bfloat16
F32 = jnp.float32

MLA_HEADS = 8
QK_NOPE_DIM = 64
QK_ROPE_DIM = 32
V_HEAD_DIM = 64
QK_HEAD_DIM = QK_NOPE_DIM + QK_ROPE_DIM
ROPE_BASE = 10000.0
LRU_HEADS = 8
CONV_WIDTH = 4
LRU_C = 8.0
N_EXPERTS = 8
TOP_K = 2
RMS_EPS = 1e-6

LANE = 128
VMEM_LIMIT = 56 * 1024 * 1024


def _params(*sem):
    return pltpu.CompilerParams(dimension_semantics=sem, vmem_limit_bytes=VMEM_LIMIT)


def _rms(x, g):
    ms = jnp.mean(x * x, axis=-1, keepdims=True)
    return x * lax.rsqrt(ms + RMS_EPS) * g


def _dot(a, b):
    return jnp.dot(a, b, preferred_element_type=F32)


def _in_qkv_body(x_ref, g_ref, w_ref, c_ref, s_ref, ct_ref, st_ref, gq_ref, gkv_ref, wq_ref, wqr_ref, wkt_ref,
                 wkrt_ref, wv_ref, gqm_ref, gqr_ref, gkm_ref, gkr_ref, xl_ref, gl_ref, ga_ref, gb_ref,
                 q_ref, kt_ref, v_ref, *, widths, q_rank, kv_rank):
    o = np.cumsum((0,) + widths)
    scale = QK_HEAD_DIM ** -0.5 * np.log2(np.e)
    inv_d = 1.0 / QK_HEAD_DIM
    nt = (((1,), (1,)), ((), ()))
    n_part = 2
    pr = x_ref.shape[0] // n_part
    for part in range(n_part):
        r = pl.ds(part * pr, pr)
        hn = _rms(x_ref[r, :], g_ref[...]).astype(BF16)
        seg = _dot(hn, w_ref[:, o[0]:o[1]])
        xl_ref[r, :] = _dot(hn, w_ref[:, o[1]:o[2]])
        gl_ref[r, :] = jax.nn.gelu(_dot(hn, w_ref[:, o[2]:o[3]]), approximate=True).astype(BF16)
        ga_ref[r, :] = jax.nn.sigmoid(_dot(hn, w_ref[:, o[3]:o[4]])).astype(BF16)
        gb_ref[r, :] = jax.nn.sigmoid(_dot(hn, w_ref[:, o[4]:o[5]])).astype(BF16)

        cqn = _rms(seg[:, 0:q_rank], gq_ref[...]).astype(BF16)
        ckvn = _rms(seg[:, q_rank:q_rank + kv_rank], gkv_ref[...])
        ck = jnp.concatenate([ckvn, seg[:, q_rank + kv_rank:]], axis=-1).astype(BF16)

        a_all = _dot(cqn, wq_ref[...])
        b_all = _dot(cqn, wqr_ref[...])
        cm = gqm_ref[...] * c_ref[r, :] * scale
        sm = gqr_ref[...] * s_ref[r, :] * scale
        for h in range(MLA_HEADS):
            a = a_all[:, h * LANE:(h + 1) * LANE]
            b = b_all[:, h * LANE:(h + 1) * LANE]
            inv = lax.rsqrt(jnp.sum(a * a, axis=-1, keepdims=True) * inv_d + RMS_EPS)
            q_ref[r, h * LANE:(h + 1) * LANE] = (inv * (a * cm + b * sm)).astype(BF16)

        at_all = lax.dot_general(wkt_ref[...], ck, nt, preferred_element_type=F32)
        bt_all = lax.dot_general(wkrt_ref[...], ck, nt, preferred_element_type=F32)
        cmt = gkm_ref[...] * ct_ref[:, r]
        smt = gkr_ref[...] * st_ref[:, r]
        for h in range(MLA_HEADS):
            a = at_all[h * LANE:(h + 1) * LANE, :]
            b = bt_all[h * LANE:(h + 1) * LANE, :]
            inv = lax.rsqrt(jnp.sum(a * a, axis=0, keepdims=True) * inv_d + RMS_EPS)
            kt_ref[0, h * LANE:(h + 1) * LANE, r] = (inv * (a * cmt + b * smt)).astype(BF16)
        v_ref[r, :] = _dot(ck, wv_ref[...]).astype(BF16)


def _in_qkv(x, g, w, widths, cos, sin, cos_t, sin_t, gq, gkv, wq, wqr, wkt, wkrt, wv, gqm, gqr, gkm, gkr, tm, seq):
    n, d = x.shape
    q_rank = gq.shape[1]
    kv_rank = gkv.shape[1]
    hq = wq.shape[1]
    hv = wv.shape[1]
    per_seq = seq // tm
    row = lambda wd: pl.BlockSpec((tm, wd), lambda i: (i, 0))
    col = pl.BlockSpec((LANE, tm), lambda i: (0, i))
    consts = (gq, gkv, wq, wqr, wkt, wkrt, wv, gqm, gqr, gkm, gkr)
    out_w = widths[1:] + (hq,)
    out_dt = (F32, BF16, BF16, BF16, BF16)
    return pl.pallas_call(
        functools.partial(_in_qkv_body, widths=widths, q_rank=q_rank, kv_rank=kv_rank),
        grid=(n // tm,),
        in_specs=[row(d), _resident(g), _resident(w), row(LANE), row(LANE), col, col]
        + [_resident(a) for a in consts],
        out_specs=[row(wd) for wd in out_w]
        + [pl.BlockSpec((1, hq, tm), lambda i: (i // per_seq, 0, i % per_seq)), row(hv)],
        out_shape=[jax.ShapeDtypeStruct((n, wd), dt) for wd, dt in zip(out_w, out_dt)]
        + [jax.ShapeDtypeStruct((n // seq, hq, seq), BF16), jax.ShapeDtypeStruct((n, hv), BF16)],
        compiler_params=_params("parallel"),
        name="in_qkv",
    )(x, g, w, cos, sin, cos_t, sin_t, *consts)


ATTN_ROW_CHUNK = 256


def _attn_body(q_ref, kt_ref, v_ref, hm_ref, o_ref, m_s, l_s, acc_s, *, tq, heads):
    i = pl.program_id(2)
    rc = ATTN_ROW_CHUNK
    n_c = tq // rc
    m_s[...] = jnp.full(m_s.shape, -jnp.inf, F32)
    l_s[...] = jnp.zeros(l_s.shape, F32)
    acc_s[...] = jnp.zeros(acc_s.shape, F32)
    lane = lax.broadcasted_iota(jnp.int32, (rc, LANE), 1)
    tri = lax.broadcasted_iota(jnp.int32, (rc, rc), 1) <= lax.broadcasted_iota(jnp.int32, (rc, rc), 0)

    def per_head(vals):
        out = vals[heads - 1]
        for hh in range(heads - 2, -1, -1):
            out = jnp.where(lane < (hh + 1) * V_HEAD_DIM, vals[hh], out)
        return out

    def load_kv(start, ncols):
        kts = [kt_ref[0, hh * LANE:(hh + 1) * LANE, pl.ds(start, ncols)] for hh in range(heads)]
        vs = v_ref[0, pl.ds(start, ncols), :]
        parts = []
        for hh in range(heads):
            ind = jnp.broadcast_to(hm_ref[hh], vs.shape)
            parts.append(jnp.concatenate([vs * ind, ind], axis=1))
        return kts, jnp.concatenate(parts, axis=0)

    def chunk_update(c, kts, vcat, ncols, masked):
        rows = pl.ds(c * rc, rc)
        ps, alphas = [], []
        for hh in range(heads):
            s = _dot(q_ref[0, rows, hh * LANE:(hh + 1) * LANE], kts[hh])
            if masked:
                tail = jnp.where(tri, s[:, ncols - rc:], -jnp.inf)
                s = tail if ncols == rc else jnp.concatenate([s[:, :ncols - rc], tail], axis=1)
            m_old = m_s[hh, rows, :]
            m_new = jnp.maximum(m_old, jnp.max(s, axis=-1, keepdims=True))
            ps.append(jnp.exp2(s - jnp.concatenate([m_new] * (ncols // LANE), axis=1)).astype(BF16))
            alphas.append(jnp.exp2(m_old - m_new))
            m_s[hh, rows, :] = m_new
        pv = _dot(jnp.concatenate(ps, axis=1), vcat)
        alpha = per_head(alphas)
        acc_s[rows, :] = alpha * acc_s[rows, :] + pv[:, :LANE]
        l_s[rows, :] = alpha * l_s[rows, :] + pv[:, LANE:]

    def kv_step(j, carry):
        kts, vcat = load_kv(pl.multiple_of(j * tq, tq), tq)
        for c in range(n_c):
            chunk_update(c, kts, vcat, tq, masked=False)
        return carry

    lax.fori_loop(0, i, kv_step, 0)
    start = pl.multiple_of(i * tq, tq)
    for c in range(n_c):
        ncols = (c + 1) * rc
        kts, vcat = load_kv(start, ncols)
        chunk_update(c, kts, vcat, ncols, masked=True)
    o_ref[0] = (acc_s[...] / l_s[...]).astype(BF16)


def _attention(q, kt, v, tq):
    b, s, hq = q.shape
    heads = LANE // V_HEAD_DIM
    n_hp = hq // (heads * LANE)
    head_mask = jnp.asarray(np.arange(LANE)[None, None, :] // V_HEAD_DIM == np.arange(heads)[:, None, None], BF16)
    return pl.pallas_call(
        functools.partial(_attn_body, tq=tq, heads=heads),
        grid=(b, n_hp, s // tq),
        in_specs=[pl.BlockSpec((1, tq, heads * LANE), lambda bi, hp, i: (bi, i, hp)),
                  pl.BlockSpec((1, heads * LANE, s), lambda bi, hp, i: (bi, hp, 0)),
                  pl.BlockSpec((1, s, LANE), lambda bi, hp, i: (bi, 0, hp)),
                  pl.BlockSpec((heads, 1, LANE), lambda bi, hp, i: (0, 0, 0))],
        out_specs=pl.BlockSpec((1, tq, LANE), lambda bi, hp, i: (bi, i, hp)),
        out_shape=jax.ShapeDtypeStruct((b, s, v.shape[2]), BF16),
        scratch_shapes=[pltpu.VMEM((heads, tq, LANE), F32), pltpu.VMEM((tq, LANE), F32),
                        pltpu.VMEM((tq, LANE), F32)],
        compiler_params=_params("parallel", "parallel", "arbitrary"),
        name="attention",
    )(q, kt, v, head_mask)


F32_SUBLANES = 8


def _lru_body(x_ref, gl_ref, cw_ref, cb_ref, wg_ref, bg_ref, lam_ref, y_ref, ext_s, h_s, *, nb, t):
    c = x_ref.shape[2]
    sub = F32_SUBLANES

    @pl.when(pl.program_id(0) == 0)
    def _():
        ext_s[:, 0:sub, :] = jnp.zeros((nb, sub, c), F32)
        h_s[...] = jnp.zeros_like(h_s)

    ext_s[:, sub:sub + t, :] = x_ref[...]
    xc = cb_ref[...][None]
    for kk in range(CONV_WIDTH):
        off = sub - (CONV_WIDTH - 1) + kk
        xc = xc + ext_s[:, off:off + t, :] * cw_ref[kk:kk + 1, :][None]
    ext_s[:, 0:sub, :] = ext_s[:, t:t + sub, :]

    xc2 = xc.reshape(nb * t, c)
    g = _dot(xc2.astype(BF16), wg_ref[...]) + bg_ref[...]
    r = jax.nn.sigmoid(g[:, :c])
    ig = jax.nn.sigmoid(g[:, c:])
    log_a = -LRU_C * r * jax.nn.softplus(-lam_ref[...])
    a = jnp.exp(log_a)
    mult = jnp.sqrt(-jnp.tanh(log_a) * (a * a + 1.0))
    bv = mult * ig * xc2

    n_g = t // sub
    a3 = a.reshape(nb * n_g, sub, c)
    b3 = bv.reshape(nb * n_g, sub, c)
    row = lax.broadcasted_iota(jnp.int32, (1, sub, c), 1)
    d = 1
    while d < sub:
        keep = row >= d
        b3 = jnp.where(keep, a3 * pltpu.roll(b3, d, axis=1) + b3, b3)
        a3 = jnp.where(keep, a3 * pltpu.roll(a3, d, axis=1), a3)
        d *= 2
    a4 = a3.reshape(nb, n_g, sub, c)
    b4 = b3.reshape(nb, n_g, sub, c)
    h = h_s[...]
    groups = []
    for g in range(n_g):
        hg = a4[:, g] * h + b4[:, g]
        groups.append(hg)
        h = hg[:, sub - 1:sub, :]
    h_s[...] = h
    hseq = jnp.stack(groups, axis=1).reshape(nb, t, c)
    y_ref[...] = (hseq * gl_ref[...].astype(F32)).astype(BF16)


def _lru(xl, gl, cw, cb, wg, bg, lam, t):
    nb, s, c = xl.shape
    blk = pl.BlockSpec((nb, t, c), lambda i: (0, i, 0))
    full = lambda a: pl.BlockSpec(a.shape, lambda i: (0, 0))
    return pl.pallas_call(
        functools.partial(_lru_body, nb=nb, t=t),
        grid=(s // t,),
        in_specs=[blk, blk, full(cw), full(cb), full(wg), full(bg), full(lam)],
        out_specs=blk,
        out_shape=jax.ShapeDtypeStruct((nb, s, c), BF16),
        scratch_shapes=[pltpu.VMEM((nb, t + F32_SUBLANES, c), F32), pltpu.VMEM((nb, 1, c), F32)],
        compiler_params=_params("arbitrary"),
        name="rg_lru",
    )(xl, gl, cw, cb, wg, bg, lam)


def _merge_body(x_ref, o_ref, y_ref, ga_ref, gb_ref, wa_ref, wb_ref, wo_ref, g2_ref, *rest, router):
    ua = _dot(o_ref[...], wa_ref[...])
    ub = _dot(y_ref[...], wb_ref[...])
    merged = ga_ref[...].astype(F32) * ua + gb_ref[...].astype(F32) * ub
    xn = x_ref[...] + _dot(merged.astype(BF16), wo_ref[...])
    h2 = _rms(xn, g2_ref[...])
    if router:
        rcat_ref, tri_ref, xn_ref, h2_ref, info_ref, cnt_ref = rest
        hi = h2.astype(BF16)
        lo = (h2 - hi.astype(F32)).astype(BF16)
        prod = _dot(jnp.concatenate([hi, lo], axis=1), rcat_ref[...])
        _topk_rows(prod[:, :LANE] + prod[:, LANE:], tri_ref, info_ref, cnt_ref)
        h2_ref[...] = h2
        xn_ref[...] = xn
    else:
        wg_ref, wu_ref, wd_ref, out_ref = rest
        hb = h2.astype(BF16)
        act = (jax.nn.silu(_dot(hb, wg_ref[...])) * _dot(hb, wu_ref[...])).astype(BF16)
        out_ref[...] = xn + _dot(act, wd_ref[...])


def _merge(x, o, y, ga, gb, wa, wb, wo, g2, tm, router=None, ffn=None):
    n, d = x.shape
    row = lambda a: pl.BlockSpec((tm, a.shape[1]), lambda i: (i, 0))
    ins = [x, o, y, ga, gb]
    consts = [wa, wb, wo, g2] + list(router or ffn)
    out_shape = [jax.ShapeDtypeStruct((n, d), F32)]
    out_specs = [pl.BlockSpec((tm, d), lambda i: (i, 0))]
    if router:
        consts.append(jnp.asarray(np.triu(np.ones((tm, tm), np.float32)), BF16))
        out_shape += [jax.ShapeDtypeStruct((n, d), F32), jax.ShapeDtypeStruct((F32_SUBLANES, n), jnp.int32),
                      jax.ShapeDtypeStruct((n // tm, N_EXPERTS, LANE), jnp.int32)]
        out_specs += [pl.BlockSpec((tm, d), lambda i: (i, 0)), pl.BlockSpec((F32_SUBLANES, tm), lambda i: (0, i)),
                      pl.BlockSpec((1, N_EXPERTS, LANE), lambda i: (i, 0, 0))]
    return pl.pallas_call(
        functools.partial(_merge_body, router=bool(router)),
        grid=(n // tm,),
        in_specs=[row(a) for a in ins] + [_resident(a) for a in consts],
        out_specs=out_specs,
        out_shape=out_shape,
        compiler_params=_params("parallel"),
        name="merge_router" if router else "merge_ffn",
    )(*ins, *consts)


def _resident(a):
    return pl.BlockSpec(a.shape, lambda *_: (0,) * a.ndim, pipeline_mode=pl.Buffered(1))


def _row_copy(src_hbm, dst, src_row, dst_row, sem):
    return pltpu.make_async_copy(src_hbm.at[pl.ds(src_row, 1), :], dst.at[pl.ds(dst_row, 1), :], sem)


MOE_FF_STEPS = 2


WD_RING = 3


def _moe_ffn_body(te_ref, nv_ref, tok_ref, tokn_ref, sidx_ref, w_ref, h_hbm, wg_ref, wu_ref, wd_hbm, y_hbm,
                  xbuf, xb16, acc_ref, obuf, wd_buf, gsem, ssem, wsem, *, tm):
    i = pl.program_id(0)
    j = pl.program_id(1)
    n_i = pl.num_programs(0)
    slot = i % 2
    tf = wd_buf.shape[1]
    step = i * MOE_FF_STEPS + j
    wslot = step % WD_RING

    def wd_copy(tile, half, ring_slot):
        return pltpu.make_async_copy(wd_hbm.at[te_ref[tile], pl.ds(pl.multiple_of(half * tf, tf), tf), :],
                                     wd_buf.at[ring_slot], wsem.at[ring_slot])

    def wait_gather(s):
        pltpu.make_async_copy(h_hbm.at[pl.ds(0, tm), :], xbuf.at[s], gsem.at[s]).wait()

    def wait_scatter(s):
        pltpu.make_async_copy(obuf.at[s], y_hbm.at[pl.ds(0, tm), :], ssem.at[s]).wait()

    @pl.when((i == 0) & (j == 0))
    def _():
        obuf[...] = jnp.zeros_like(obuf)
        for half in range(MOE_FF_STEPS):
            wd_copy(0, half, half).start()

        def issue(r, carry):
            _row_copy(h_hbm, xbuf.at[0], tok_ref[r], r, gsem.at[0]).start()
            return carry

        lax.fori_loop(0, tm, issue, 0, unroll=8)

    @pl.when(i + 1 < n_i)
    def _():
        wd_copy(i + 1, j, (step + MOE_FF_STEPS) % WD_RING).start()

    wd_copy(i, j, wslot).wait()

    @pl.when(j == 0)
    def _():
        wait_gather(slot)
        xb16[...] = xbuf[slot].astype(BF16)

    def gather_next():
        for r in range(tm):
            _row_copy(h_hbm, xbuf.at[1 - slot], tokn_ref[r], r, gsem.at[1 - slot]).start()

    def scatter_prev():
        for r in range(tm):
            _row_copy(obuf.at[1 - slot], y_hbm, r, sidx_ref[r], ssem.at[1 - slot]).start()

    def compute(first):
        h = xb16[...]
        act = (jax.nn.silu(_dot(h, wg_ref[0])) * _dot(h, wu_ref[0])).astype(BF16)
        part = _dot(act, wd_buf[wslot])
        if first:
            acc_ref[...] = part
        else:
            obuf[slot] = (acc_ref[...] + part) * w_ref[...]

    valid = i < nv_ref[0]

    @pl.when(valid & (j == 0))
    def _():
        gather_next()
        compute(first=True)

    @pl.when(jnp.logical_not(valid) & (j == 0))
    def _():
        gather_next()

    @pl.when((j == 1) & (i >= 1))
    def _():
        wait_scatter(slot)

    @pl.when(valid & (j == 1))
    def _():
        scatter_prev()
        compute(first=False)

    @pl.when(jnp.logical_not(valid) & (j == 1) & (i < n_i - 1))
    def _():
        scatter_prev()

    @pl.when((i == n_i - 1) & (j == 1))
    def _():
        scatter_prev()
        wait_scatter(1 - slot)
        wait_gather(1 - slot)


def _moe_ffn(tile_e, n_valid, tok, sidx, h, w_rows, wg, wu, wd, y_rows, tm):
    d = h.shape[1]
    f = wg.shape[2]
    tf = f // MOE_FF_STEPS
    n_grid = tok.shape[0] // tm
    nxt = lambda i, j, te, nv: (jnp.minimum(i + 1, n_grid - 1),)
    grid_spec = pltpu.PrefetchScalarGridSpec(
        num_scalar_prefetch=2,
        grid=(n_grid, MOE_FF_STEPS),
        in_specs=[pl.BlockSpec((tm,), lambda i, j, te, nv: (i,), memory_space=pltpu.SMEM),
                  pl.BlockSpec((tm,), nxt, memory_space=pltpu.SMEM),
                  pl.BlockSpec((tm,), lambda i, j, te, nv: (i,), memory_space=pltpu.SMEM),
                  pl.BlockSpec((tm, 1), lambda i, j, te, nv: (i, 0)),
                  pl.BlockSpec(memory_space=pl.ANY),
                  pl.BlockSpec((1, d, tf), lambda i, j, te, nv: (te[i], 0, j)),
                  pl.BlockSpec((1, d, tf), lambda i, j, te, nv: (te[i], 0, j)),
                  pl.BlockSpec(memory_space=pl.ANY)],
        out_specs=pl.BlockSpec(memory_space=pl.ANY),
        scratch_shapes=[pltpu.VMEM((2, tm, d), F32), pltpu.VMEM((tm, d), BF16), pltpu.VMEM((tm, d), F32),
                        pltpu.VMEM((2, tm, d), F32), pltpu.VMEM((WD_RING, tf, d), BF16),
                        pltpu.SemaphoreType.DMA((2,)), pltpu.SemaphoreType.DMA((2,)),
                        pltpu.SemaphoreType.DMA((WD_RING,))],
    )
    return pl.pallas_call(
        functools.partial(_moe_ffn_body, tm=tm),
        grid_spec=grid_spec,
        out_shape=jax.ShapeDtypeStruct((y_rows, d), F32),
        compiler_params=_params("arbitrary", "arbitrary"),
        name="moe_ffn",
    )(tile_e, n_valid, tok, tok, sidx, w_rows, h, wg, wu, wd)


def _moe_add_body(x_ref, ya_ref, yb_ref, o_ref):
    o_ref[...] = x_ref[...] + ya_ref[...] + yb_ref[...]


def _moe_add(x, y2, tm):
    n, d = x.shape
    return pl.pallas_call(
        _moe_add_body,
        grid=(n // tm,),
        in_specs=[pl.BlockSpec((tm, d), lambda i: (i, 0)),
                  pl.BlockSpec((tm, d), lambda i: (i, 0)),
                  pl.BlockSpec((tm, d), lambda i: (i + n // tm, 0))],
        out_specs=pl.BlockSpec((tm, d), lambda i: (i, 0)),
        out_shape=jax.ShapeDtypeStruct((n, d), F32),
        compiler_params=_params("parallel"),
        name="moe_add",
    )(x, y2, y2)


def _topk_rows(logits, tri_ref, info_ref, cnt_ref):
    tm = logits.shape[0]
    l8 = logits.T[0:N_EXPERTS, :]
    eidx = lax.broadcasted_iota(jnp.int32, l8.shape, 0)
    m1 = jnp.max(l8, axis=0, keepdims=True)
    e1 = jnp.min(jnp.where(l8 == m1, eidx, N_EXPERTS), axis=0, keepdims=True)
    sel1 = eidx == e1
    rest = jnp.where(sel1, -jnp.inf, l8)
    m2 = jnp.max(rest, axis=0, keepdims=True)
    e2 = jnp.min(jnp.where(rest == m2, eidx, N_EXPERTS), axis=0, keepdims=True)
    sel2 = eidx == e2
    ex = jnp.exp(m2 - m1)
    w1 = 1.0 / (1.0 + ex)
    w2 = ex / (1.0 + ex)
    onehot = jnp.where(sel1 | sel2, 1.0, 0.0)
    incl = _dot(onehot.astype(BF16), tri_ref[...])
    excl = incl - onehot
    r1 = jnp.sum(jnp.where(sel1, excl, 0.0), axis=0, keepdims=True)
    r2 = jnp.sum(jnp.where(sel2, excl, 0.0), axis=0, keepdims=True)
    as_i32 = lambda v: lax.bitcast_convert_type(v, jnp.int32)
    rows = (e1, e2, as_i32(w1), as_i32(w2), r1.astype(jnp.int32), r2.astype(jnp.int32))
    for k, val in enumerate(rows):
        info_ref[k:k + 1, :] = val
    info_ref[len(rows):, :] = jnp.zeros((info_ref.shape[0] - len(rows), tm), jnp.int32)
    cnt_ref[0] = jnp.broadcast_to(incl[:, tm - 1:tm], (N_EXPERTS, LANE)).astype(jnp.int32)


def _route(info, cnt, tm):
    n = info.shape[1]
    tr = n // cnt.shape[0]
    cnt = cnt[:, :, 0]
    counts = jnp.sum(cnt, axis=0)
    padded = (counts + tm - 1) // tm * tm
    pend = jnp.cumsum(padded)
    pstart = pend - padded
    base = (pstart[None, :] + jnp.cumsum(cnt, axis=0) - cnt).T
    base_tok = jnp.repeat(base, tr, axis=1)
    eidx = jnp.arange(N_EXPERTS, dtype=jnp.int32)[:, None]
    slot = lambda e, r: jnp.sum(jnp.where(eidx == e[None, :], base_tok, 0), axis=0) + r
    dest = jnp.concatenate([slot(info[0], info[4]), slot(info[1], info[5])]).astype(jnp.int32)
    n_rows = (-(-(n * TOP_K + N_EXPERTS * (tm - 1)) // tm) + 1) * tm
    assign = jnp.arange(n * TOP_K, dtype=jnp.int32)
    packed = jnp.stack([assign, jnp.concatenate([info[2], info[3]])], axis=1)
    fill = jnp.broadcast_to(jnp.array([-1, 0], jnp.int32), (n_rows, 2))
    buf = fill.at[dest].set(packed, unique_indices=True, mode="promise_in_bounds")
    a_buf = buf[:, 0]
    w_buf = lax.bitcast_convert_type(buf[:, 1], F32)
    real = a_buf >= 0
    tok_buf = jnp.where(real, a_buf % n, 0)
    spare = TOP_K * n + jnp.arange(n_rows, dtype=jnp.int32) % tm
    sidx = jnp.where(real, a_buf, spare)
    sidx = jnp.concatenate([spare[:tm], sidx[:-tm]])
    tile_start = jnp.arange(n_rows // tm, dtype=jnp.int32) * tm
    tile_e = jnp.minimum(jnp.sum(tile_start[:, None] >= pend[None, :], axis=1), N_EXPERTS - 1)
    n_valid = (pend[-1] // tm).astype(jnp.int32).reshape(1)
    return tok_buf, sidx, w_buf.reshape(-1, 1), tile_e.astype(jnp.int32), n_valid


def _head_slots(w, heads, width):
    k = w.shape[0]
    return jnp.pad(w.reshape(k, heads, width), ((0, 0), (0, 0), (0, LANE - width)))


def _rotate_half_slots(w3):
    half = QK_ROPE_DIM // 2
    lo, mid, hi = QK_NOPE_DIM, QK_NOPE_DIM + half, QK_HEAD_DIM
    out = jnp.zeros_like(w3)
    out = out.at[:, :, lo:mid].set(-w3[:, :, mid:hi])
    out = out.at[:, :, mid:hi].set(w3[:, :, lo:mid])
    return out


def _rope_gains(g):
    half = QK_ROPE_DIM // 2
    lo, mid, hi = QK_NOPE_DIM, QK_NOPE_DIM + half, QK_HEAD_DIM
    main = jnp.pad(g, (0, LANE - QK_HEAD_DIM))
    rot = jnp.zeros((LANE,), F32).at[lo:mid].set(g[mid:hi]).at[mid:hi].set(g[lo:mid])
    return main.reshape(1, LANE), rot.reshape(1, LANE)


def _rope_tables(positions):
    half = QK_ROPE_DIM // 2
    inv_freq = ROPE_BASE ** (-jnp.arange(0, QK_ROPE_DIM, 2, dtype=F32) / QK_ROPE_DIM)
    ang = positions.astype(F32).reshape(-1, 1) * inv_freq[None, :]
    n = ang.shape[0]
    dense = ang.reshape(n * half // LANE, LANE)
    cos_h = jnp.cos(dense).reshape(n, half)
    sin_h = jnp.sin(dense).reshape(n, half)
    pad = jnp.zeros((n, LANE - QK_HEAD_DIM), F32)
    cos = jnp.concatenate([jnp.ones((n, QK_NOPE_DIM), F32), cos_h, cos_h, pad], axis=-1)
    sin = jnp.concatenate([jnp.zeros((n, QK_NOPE_DIM), F32), sin_h, sin_h, pad], axis=-1)
    return cos, sin


def _block_diag(w):
    h, a, b = w.shape
    eye = jnp.eye(h, dtype=w.dtype)
    return (eye[:, None, :, None] * w[:, :, None, :]).reshape(h * a, h * b)


def kernel(x, positions, norm1_g, w_in, q_norm_g, w_uq, kv_norm_g, w_ukv, qk_q_g, qk_k_g, w_up_attn, conv_w, conv_b, w_rg, b_rg, w_ig, b_ig, lru_lambda, w_up_lru, w_o, norm2_g, ffn_w_gate, ffn_w_up, ffn_w_down, moe_router, moe_w_gate, moe_w_up, moe_w_down):
    batch, seq, d = x.shape
    n = batch * seq
    depth = norm1_g.shape[0]
    q_rank = q_norm_g.shape[1]
    kv_rank = kv_norm_g.shape[1]
    c_lru = conv_w.shape[2]
    lat = q_rank + kv_rank + QK_ROPE_DIM
    seg_w = -(-lat // LANE) * LANE
    widths = (seg_w, c_lru, c_lru, d, d)

    tm = 512
    tq = min(2048, seq)
    t_lru = 128
    tm_moe = 512

    cos, sin = _rope_tables(positions)
    cos_t, sin_t = cos.T, sin.T
    xf = x.reshape(n, d)
    for layer in range(depth):
        w_in_l = w_in[layer]
        w_in_p = jnp.concatenate(
            [w_in_l[:, :lat], jnp.zeros((d, seg_w - lat), F32), w_in_l[:, lat:]], axis=1).astype(BF16)
        wq3 = _head_slots(w_uq[layer], MLA_HEADS, QK_HEAD_DIM)
        wkv3 = w_ukv[layer].reshape(kv_rank, MLA_HEADS, QK_NOPE_DIM + V_HEAD_DIM)
        ck_w = seg_w - q_rank
        wk3 = jnp.zeros((ck_w, MLA_HEADS, LANE), F32)
        wk3 = wk3.at[:kv_rank, :, :QK_NOPE_DIM].set(wkv3[:, :, :QK_NOPE_DIM])
        eye = jnp.eye(QK_ROPE_DIM, dtype=F32)
        wk3 = wk3.at[kv_rank:kv_rank + QK_ROPE_DIM, :, QK_NOPE_DIM:QK_HEAD_DIM].set(
            jnp.broadcast_to(eye[:, None, :], (QK_ROPE_DIM, MLA_HEADS, QK_ROPE_DIM)))
        wv = jnp.zeros((ck_w, MLA_HEADS * V_HEAD_DIM), F32).at[:kv_rank].set(
            wkv3[:, :, QK_NOPE_DIM:].reshape(kv_rank, MLA_HEADS * V_HEAD_DIM))
        flat = lambda w3: w3.reshape(w3.shape[0], MLA_HEADS * LANE).astype(BF16)
        gqm, gqr = _rope_gains(qk_q_g[layer])
        gkm, gkr = (g.reshape(LANE, 1) for g in _rope_gains(qk_k_g[layer]))
        xl, gl, ga, gb, q, kt, v = _in_qkv(
            xf, norm1_g[layer].reshape(1, d), w_in_p, widths, cos, sin, cos_t, sin_t,
            q_norm_g[layer].reshape(1, -1), kv_norm_g[layer].reshape(1, -1), flat(wq3),
            flat(_rotate_half_slots(wq3)), flat(wk3).T, flat(_rotate_half_slots(wk3)).T, wv.astype(BF16),
            gqm, gqr, gkm, gkr, tm, seq)
        o = _attention(q.reshape(batch, seq, -1), kt, v.reshape(batch, seq, -1), tq)

        wgate = jnp.concatenate([_block_diag(w_rg[layer]), _block_diag(w_ig[layer])], axis=1).astype(BF16)
        bgate = jnp.concatenate([b_rg[layer], b_ig[layer]]).reshape(1, -1)
        y = _lru(xl.reshape(batch, seq, c_lru), gl.reshape(batch, seq, c_lru), conv_w[layer],
                 conv_b[layer].reshape(1, -1), wgate, bgate, lru_lambda[layer].reshape(1, -1), t_lru)

        j = layer // 2
        is_moe = layer % 2 == 1
        merge_args = (xf, o.reshape(n, -1), y.reshape(n, -1), ga, gb, w_up_attn[layer].astype(BF16),
                      w_up_lru[layer].astype(BF16), w_o[layer].astype(BF16), norm2_g[layer].reshape(1, d), tm)
        if not is_moe:
            (xf,) = _merge(*merge_args, ffn=(ffn_w_gate[j].astype(BF16), ffn_w_up[j].astype(BF16),
                                             ffn_w_down[j].astype(BF16)))
        else:
            r_pad = jnp.pad(moe_router[j], ((0, 0), (0, LANE - N_EXPERTS)))
            r_hi = r_pad.astype(BF16)
            r_lo = (r_pad - r_hi.astype(F32)).astype(BF16)
            rcat = jnp.concatenate([jnp.concatenate([r_hi, r_lo], axis=1),
                                    jnp.concatenate([r_hi, jnp.zeros_like(r_lo)], axis=1)], axis=0)
            xn, h2, info, cnt = _merge(*merge_args, router=(rcat,))
            tok_buf, sidx, w_rows, tile_e, n_valid = _route(info, cnt, tm_moe)
            y2 = _moe_ffn(tile_e, n_valid, tok_buf, sidx, h2, w_rows, moe_w_gate[j].astype(BF16),
                          moe_w_up[j].astype(BF16), moe_w_down[j].astype(BF16), TOP_K * n + tm_moe, tm_moe)
            xf = _moe_add(xn, y2, tm)
    return xf.reshape(batch, seq, d)
```
